```python
import math
import jax
import jax.numpy as jnp
from jax import lax
import numpy as np

D_MODEL = 1024
BATCH = 1
SEQ = 16384
DEPTH = 2
DEC_BATCH = 128
DEC_SEQ = 8
PAST_LEN = 16384
PAGE_SIZE = 128

N_HEADS = 16
N_KV_HEADS = 4
GROUP = N_HEADS // N_KV_HEADS
HEAD_DIM = D_MODEL // N_HEADS
WINDOW = 128
ROPE_THETA = 10000.0
ATTN_SCALE = 1.0 / math.sqrt(HEAD_DIM)
RET_HEADS = 4
RET_QK_DIM = D_MODEL // RET_HEADS
RET_V_DIM = 2 * D_MODEL // RET_HEADS
RET_CHUNK = 128
RET_THETA = 10000.0
FFN_DIM = 2816
LN_EPS = 1e-5
ALPHA = (2.0 * DEPTH) ** 0.25
BETA = (8.0 * DEPTH) ** -0.25
N_MIXERS = 2
N_ATTN_LAYERS = (DEPTH + 1) // 2
N_RET_LAYERS = DEPTH // 2

kernel_name = "hybrid_swa_sink_retention_macaron_deepnorm_step"

F32 = jnp.float32


def _layer_norm(x, g, b):
    xf = x.astype(F32)
    mu = jnp.mean(xf, axis=-1, keepdims=True)
    var = jnp.mean(jnp.square(xf - mu), axis=-1, keepdims=True)
    y = (xf - mu) * lax.rsqrt(var + LN_EPS)
    return (y * g.astype(F32) + b.astype(F32)).astype(x.dtype)


def _swiglu(x, wg, wu, wd):
    return (jax.nn.silu(x @ wg) * (x @ wu)) @ wd


def _half_ffn(x, wg, wu, wd, g, b):
    return _layer_norm(ALPHA * x + 0.5 * _swiglu(x, wg, wu, wd), g, b)


def _rope_half(x, pos):
    d = x.shape[-1]
    inv = ROPE_THETA ** (-jnp.arange(0, d, 2, dtype=F32) / d)
    ang = pos.astype(F32)[:, None] * inv[None, :]
    c = jnp.cos(ang)[:, None, :]
    s = jnp.sin(ang)[:, None, :]
    xf = x.astype(F32)
    x1, x2 = xf[..., : d // 2], xf[..., d // 2:]
    return jnp.concatenate([x1 * c - x2 * s, x2 * c + x1 * s], axis=-1).astype(x.dtype)


def _xpos_rotate(x, pos):
    d = x.shape[-1]
    inv = 1.0 / (RET_THETA ** jnp.linspace(0.0, 1.0, d // 2, dtype=F32))
    ang = pos.astype(F32)[:, None] * inv[None, :]
    c = jnp.cos(ang)[:, None, :]
    s = jnp.sin(ang)[:, None, :]
    xf = x.astype(F32).reshape(x.shape[:-1] + (d // 2, 2))
    xe, xo = xf[..., 0], xf[..., 1]
    return jnp.stack([xe * c - xo * s, xo * c + xe * s], axis=-1).reshape(x.shape)


def _sink_softmax(scores, valid, sinks):
    sink_b = sinks.astype(F32).reshape(N_KV_HEADS, GROUP)[:, :, None, None]
    s = jnp.where(valid, scores, -jnp.inf)
    m = jnp.maximum(jnp.max(s, axis=-1, keepdims=True), sink_b)
    e = jnp.exp(s - m)
    return e / (jnp.sum(e, axis=-1, keepdims=True) + jnp.exp(sink_b - m))


def _attn_proj(x, pos, w_qkv):
    B, T, _ = x.shape
    h = x @ w_qkv
    q = h[..., : N_HEADS * HEAD_DIM].reshape(B, T, N_HEADS, HEAD_DIM)
    k = h[..., N_HEADS * HEAD_DIM:(N_HEADS + N_KV_HEADS) * HEAD_DIM].reshape(B, T, N_KV_HEADS, HEAD_DIM)
    v = h[..., (N_HEADS + N_KV_HEADS) * HEAD_DIM:].reshape(B, T, N_KV_HEADS, HEAD_DIM)
    return _rope_half(q, pos), _rope_half(k, pos), v


def _swa_prompt(x, w_qkv, w_o, sinks):
    B, T, _ = x.shape
    q, k, v = _attn_proj(x, jnp.arange(T), w_qkv)
    nb = T // WINDOW
    qb = q.reshape(B, nb, WINDOW, N_KV_HEADS, GROUP, HEAD_DIM)
    kb = k.reshape(B, nb, WINDOW, N_KV_HEADS, HEAD_DIM)
    vb = v.reshape(B, nb, WINDOW, N_KV_HEADS, HEAD_DIM)

    def with_prev(a):
        prev = jnp.pad(a[:, :-1], ((0, 0), (1, 0), (0, 0), (0, 0), (0, 0)))
        return jnp.concatenate([prev, a], axis=2)

    kc, vc = with_prev(kb), with_prev(vb)
    scores = jnp.einsum('bnqkgd,bnskd->bnkgqs', qb, kc).astype(F32) * ATTN_SCALE
    blk = jnp.arange(nb)[:, None, None]
    qi = jnp.arange(WINDOW)[None, :, None]
    si = jnp.arange(2 * WINDOW)[None, None, :]
    dist = qi + WINDOW - si
    kpos = blk * WINDOW - WINDOW + si
    valid = (dist >= 0) & (dist < WINDOW) & (kpos >= 0)
    p = _sink_softmax(scores, valid[None, :, None, None], sinks)
    o = jnp.einsum('bnkgqs,bnskd->bnqkgd', p.astype(vc.dtype), vc).reshape(B, T, N_HEADS * HEAD_DIM)
    return o @ w_o, k[:, T - WINDOW:], v[:, T - WINDOW:]


def _swa_sample(x, k_buf, v_buf, w_qkv, w_o, sinks):
    B, L, _ = x.shape
    pos = PAST_LEN + jnp.arange(L)
    q, k, v = _attn_proj(x, pos, w_qkv)
    kall = jnp.concatenate([k_buf.astype(k.dtype), k], axis=1)
    vall = jnp.concatenate([v_buf.astype(v.dtype), v], axis=1)
    kpos = jnp.concatenate([PAST_LEN - WINDOW + jnp.arange(WINDOW), pos])
    dist = pos[:, None] - kpos[None, :]
    valid = (dist >= 0) & (dist < WINDOW) & (kpos >= 0)[None, :]
    qg = q.reshape(B, L, N_KV_HEADS, GROUP, HEAD_DIM)
    scores = jnp.einsum('bqkgd,bskd->bkgqs', qg, kall).astype(F32) * ATTN_SCALE
    p = _sink_softmax(scores, valid, sinks)
    o = jnp.einsum('bkgqs,bskd->bqkgd', p.astype(vall.dtype), vall).reshape(B, L, N_HEADS * HEAD_DIM)
    return o @ w_o, kall[:, -WINDOW:], vall[:, -WINDOW:]


def _ret_log_decay():
    return jnp.log(1.0 - 2.0 ** (-5.0 - jnp.arange(RET_HEADS, dtype=F32)))


def _ret_proj(x, pos, w_in):
    B, T, _ = x.shape
    h = x @ w_in
    nq = RET_HEADS * RET_QK_DIM
    nv = RET_HEADS * RET_V_DIM
    q = h[..., :nq].reshape(B, T, RET_HEADS, RET_QK_DIM)
    k = h[..., nq:2 * nq].reshape(B, T, RET_HEADS, RET_QK_DIM)
    v = h[..., 2 * nq:2 * nq + nv].reshape(B, T, RET_HEADS, RET_V_DIM).astype(F32)
    g = h[..., 2 * nq + nv:]
    q = _xpos_rotate(q, pos)
    k = _xpos_rotate(k, pos) * (RET_QK_DIM ** -0.5)
    return q, k, v, g


def _retention_chunk(S, q, k, v, lg):
    L = q.shape[1]
    idx = jnp.arange(L, dtype=F32)
    diff = idx[:, None] - idx[None, :]
    causal = diff >= 0
    dmat = jnp.where(causal[None], jnp.exp(lg[:, None, None] * jnp.where(causal, diff, 0.0)[None]), 0.0)
    att = jnp.einsum('bihd,bjhd->bhij', q, k) * dmat[None]
    o = jnp.einsum('bhij,bjhe->bihe', att, v)
    xi = jnp.exp(lg[None, :] * (idx[:, None] + 1.0))
    o = o + jnp.einsum('bihd,bhde->bihe', q, S) * xi[None, :, :, None]
    zeta = jnp.exp(lg[None, :] * (L - 1.0 - idx[:, None]))
    S_new = jnp.exp(lg * L)[None, :, None, None] * S + jnp.einsum('bjhd,bjhe->bhde', k * zeta[None, :, :, None], v)
    return S_new, o


def _retention_out(o, g, w_o, dtype):
    B, T = o.shape[:2]
    mu = jnp.mean(o, axis=-1, keepdims=True)
    var = jnp.mean(jnp.square(o - mu), axis=-1, keepdims=True)
    on = ((o - mu) * lax.rsqrt(var + LN_EPS)).reshape(B, T, RET_HEADS * RET_V_DIM)
    y = jax.nn.silu(g.astype(F32)) * on
    return y.astype(dtype) @ w_o


def _retention_prompt(x, w_in, w_o, lg):
    B, T, _ = x.shape
    q, k, v, g = _ret_proj(x, jnp.arange(T), w_in)
    nc = T // RET_CHUNK

    def to_chunks(a):
        return a.reshape((B, nc, RET_CHUNK) + a.shape[2:]).swapaxes(0, 1)

    S0 = jnp.zeros((B, RET_HEADS, RET_QK_DIM, RET_V_DIM), F32)
    S_fin, o = lax.scan(lambda S, c: _retention_chunk(S, c[0], c[1], c[2], lg), S0,
                        (to_chunks(q), to_chunks(k), to_chunks(v)))
    o = o.swapaxes(0, 1).reshape(B, T, RET_HEADS, RET_V_DIM)
    return _retention_out(o, g, w_o, x.dtype), S_fin


def _retention_sample(x, S, w_in, w_o, lg):
    B, L, _ = x.shape
    q, k, v, g = _ret_proj(x, PAST_LEN + jnp.arange(L), w_in)
    S_new, o = _retention_chunk(S.astype(F32), q, k, v, lg)
    return _retention_out(o, g, w_o, x.dtype), S_new


def setup_inputs(seed: int = 0) -> dict:
    key = jax.random.key(seed)
    ks = jax.random.split(key, 20)
    n = jax.random.normal
    D, F = D_MODEL, FFN_DIM
    qkv_w = (N_HEADS + 2 * N_KV_HEADS) * HEAD_DIM
    ret_in_w = RET_HEADS * (2 * RET_QK_DIM + 2 * RET_V_DIM)
    return {
        "x_prompt": n(ks[0], (BATCH, SEQ, D), F32),
        "x_sample": n(ks[1], (DEC_BATCH, DEC_SEQ, D), F32),
        "cache_k_win": n(ks[2], (N_ATTN_LAYERS, DEC_BATCH, WINDOW, N_KV_HEADS, HEAD_DIM), F32),
        "cache_v_win": n(ks[3], (N_ATTN_LAYERS, DEC_BATCH, WINDOW, N_KV_HEADS, HEAD_DIM), F32),
        "state_ret": 0.1 * n(ks[4], (N_RET_LAYERS, DEC_BATCH, RET_HEADS, RET_QK_DIM, RET_V_DIM), F32),
        "ffn1_w_gate": n(ks[5], (DEPTH, D, F), F32) * D ** -0.5,
        "ffn1_w_up": n(ks[6], (DEPTH, D, F), F32) * D ** -0.5,
        "ffn1_w_down": n(ks[7], (DEPTH, F, D), F32) * (F ** -0.5 * BETA),
        "ffn2_w_gate": n(ks[8], (DEPTH, D, F), F32) * D ** -0.5,
        "ffn2_w_up": n(ks[9], (DEPTH, D, F), F32) * D ** -0.5,
        "ffn2_w_down": n(ks[10], (DEPTH, F, D), F32) * (F ** -0.5 * BETA),
        "ln_g": 1.0 + 0.02 * n(ks[11], (DEPTH, 3, D), F32),
        "ln_b": 0.02 * n(ks[12], (DEPTH, 3, D), F32),
        "attn_w_qkv": n(ks[13], (N_ATTN_LAYERS, D, qkv_w), F32) * D ** -0.5,
        "attn_w_o": n(ks[14], (N_ATTN_LAYERS, N_HEADS * HEAD_DIM, D), F32) * ((N_HEADS * HEAD_DIM) ** -0.5 * BETA),
        "attn_sinks": n(ks[15], (N_ATTN_LAYERS, N_HEADS), F32),
        "ret_w_in": n(ks[16], (N_RET_LAYERS, D, ret_in_w), F32) * D ** -0.5,
        "ret_w_o": n(ks[17], (N_RET_LAYERS, RET_HEADS * RET_V_DIM, D), F32) * ((RET_HEADS * RET_V_DIM) ** -0.5 * BETA),
    }


def reference(x_prompt, x_sample, cache_k_win, cache_v_win, state_ret,
              ffn1_w_gate, ffn1_w_up, ffn1_w_down, ffn2_w_gate, ffn2_w_up, ffn2_w_down,
              ln_g, ln_b, attn_w_qkv, attn_w_o, attn_sinks, ret_w_in, ret_w_o):
    xp, xs = x_prompt, x_sample
    lg = _ret_log_decay()
    kp_l, vp_l, ks_l, vs_l, sp_l, ss_l = [], [], [], [], [], []
    for i in range(DEPTH):
        xp = _half_ffn(xp, ffn1_w_gate[i], ffn1_w_up[i], ffn1_w_down[i], ln_g[i, 0], ln_b[i, 0])
        xs = _half_ffn(xs, ffn1_w_gate[i], ffn1_w_up[i], ffn1_w_down[i], ln_g[i, 0], ln_b[i, 0])
        j = i // N_MIXERS
        if i % N_MIXERS == 0:
            mp, kp, vp = _swa_prompt(xp, attn_w_qkv[j], attn_w_o[j], attn_sinks[j])
            ms, kS, vS = _swa_sample(xs, cache_k_win[j], cache_v_win[j], attn_w_qkv[j], attn_w_o[j], attn_sinks[j])
            kp_l.append(kp.astype(x_prompt.dtype))
            vp_l.append(vp.astype(x_prompt.dtype))
            ks_l.append(kS.astype(cache_k_win.dtype))
            vs_l.append(vS.astype(cache_v_win.dtype))
        else:
            mp, sp = _retention_prompt(xp, ret_w_in[j], ret_w_o[j], lg)
            ms, sS = _retention_sample(xs, state_ret[j], ret_w_in[j], ret_w_o[j], lg)
            sp_l.append(sp.astype(x_prompt.dtype))
            ss_l.append(sS.astype(state_ret.dtype))
        xp = _layer_norm(ALPHA * xp + mp, ln_g[i, 1], ln_b[i, 1])
        xs = _layer_norm(ALPHA * xs + ms, ln_g[i, 1], ln_b[i, 1])
        xp = _half_ffn(xp, ffn2_w_gate[i], ffn2_w_up[i], ffn2_w_down[i], ln_g[i, 2], ln_b[i, 2])
        xs = _half_ffn(xs, ffn2_w_gate[i], ffn2_w_up[i], ffn2_w_down[i], ln_g[i, 2], ln_b[i, 2])
    return (xp, xs, jnp.stack(kp_l), jnp.stack(vp_l), jnp.stack(ks_l), jnp.stack(vs_l), jnp.stack(sp_l), jnp.stack(ss_l))
```

```python
import functools
import math

import jax
import jax.numpy as jnp
from jax import lax
from jax.experimental import pallas as pl
from jax.experimental.pallas import tpu as pltpu

F32 = jnp.float32
BF16 = jnp.bfloat16

D_MODEL = 1024
SEQ = 16384
DEPTH = 2
DEC_BATCH = 128
DEC_SEQ = 8
PAST_LEN = 16384
N_HEADS = 16
N_KV_HEADS = 4
GROUP = N_HEADS // N_KV_HEADS
HEAD_DIM = D_MODEL // N_HEADS
WINDOW = 128
ROPE_THETA = 10000.0
ATTN_SCALE = 1.0 / math.sqrt(HEAD_DIM)
RET_HEADS = 4
RET_QK_DIM = D_MODEL // RET_HEADS
RET_V_DIM = 2 * D_MODEL // RET_HEADS
RET_CHUNK = 128
RET_THETA = 10000.0
FFN_DIM = 2816
LN_EPS = 1e-5
ALPHA = (2.0 * DEPTH) ** 0.25

N_SAMPLE = DEC_BATCH * DEC_SEQ
N_TOK = SEQ + N_SAMPLE
KV_DIM = N_KV_HEADS * HEAD_DIM
RET_QK = RET_HEADS * RET_QK_DIM
RET_V = RET_HEADS * RET_V_DIM

LANES = 128
ROW_TILE = 512
FFN_CHUNK = 256
VMEM_LIMIT = 56 * 1024 * 1024

NT_DIMS = (((1,), (1,)), ((), ()))
TN_DIMS = (((0,), (0,)), ((), ()))


def _params(*sem):
    return pltpu.CompilerParams(dimension_semantics=sem, vmem_limit_bytes=VMEM_LIMIT)


def _resident(shape):
    return pl.BlockSpec(shape, lambda *_: (0,) * len(shape), pipeline_mode=pl.Buffered(1))


def _rows(width, offset_blocks=0, tile=ROW_TILE):
    return pl.BlockSpec((tile, width), lambda i: (i + offset_blocks, 0))


def _layer_norm(y, g, b):
    mu = jnp.mean(y, axis=-1, keepdims=True)
    d = y - mu
    var = jnp.mean(d * d, axis=-1, keepdims=True)
    return d * lax.rsqrt(var + LN_EPS) * g + b


def _silu(x):
    return x / (1.0 + jnp.exp(-x))


def _ffn_kernel(x_ref, wg_ref, wu_ref, wd_ref, g_ref, b_ref, o_ref):
    x = x_ref[...]
    xb = x.astype(BF16)
    acc = None
    for c in range(FFN_DIM // FFN_CHUNK):
        sl = slice(c * FFN_CHUNK, (c + 1) * FFN_CHUNK)
        gate = jnp.dot(xb, wg_ref[:, sl], preferred_element_type=F32)
        up = jnp.dot(xb, wu_ref[:, sl], preferred_element_type=F32)
        h = (_silu(gate) * up).astype(BF16)
        d = jnp.dot(h, wd_ref[sl, :], preferred_element_type=F32)
        acc = d if acc is None else acc + d
    o_ref[...] = _layer_norm(ALPHA * x + 0.5 * acc, g_ref[...], b_ref[...])


def _half_ffn(x, wg, wu, wd, g, b):
    n = x.shape[0]
    return pl.pallas_call(
        _ffn_kernel,
        grid=(n // ROW_TILE,),
        in_specs=[_rows(D_MODEL), _resident((D_MODEL, FFN_DIM)), _resident((D_MODEL, FFN_DIM)),
                  _resident((FFN_DIM, D_MODEL)), _resident((1, D_MODEL)), _resident((1, D_MODEL))],
        out_specs=_rows(D_MODEL),
        out_shape=jax.ShapeDtypeStruct((n, D_MODEL), F32),
        compiler_params=_params("parallel"),
        name="half_ffn",
    )(x, wg, wu, wd, g, b)


def _mix_out_kernel(y_ref, x_ref, w_ref, g_ref, b_ref, o_ref):
    m = jnp.dot(y_ref[...], w_ref[...], preferred_element_type=F32)
    o_ref[...] = _layer_norm(ALPHA * x_ref[...] + m, g_ref[...], b_ref[...])


def _mix_out(y, x, w, g, b):
    n, k = y.shape
    return pl.pallas_call(
        _mix_out_kernel,
        grid=(n // ROW_TILE,),
        in_specs=[_rows(k), _rows(D_MODEL), _resident((k, D_MODEL)),
                  _resident((1, D_MODEL)), _resident((1, D_MODEL))],
        out_specs=_rows(D_MODEL),
        out_shape=jax.ShapeDtypeStruct((n, D_MODEL), F32),
        compiler_params=_params("parallel"),
        name="mix_out",
    )(y, x, w, g, b)


def _rotate_blocks(h, n_blocks, cos_ref, sa_ref, sb_ref, shift, table_blocks):
    out = []
    for j in range(n_blocks):
        t = (j % table_blocks) * LANES
        blk = h[:, j * LANES:(j + 1) * LANES]
        out.append(blk * cos_ref[:, t:t + LANES]
                   + pltpu.roll(blk, LANES - shift, 1) * sa_ref[:, t:t + LANES]
                   + pltpu.roll(blk, shift, 1) * sb_ref[:, t:t + LANES])
    return out


def _qkv_kernel(x_ref, w_ref, cos_ref, sa_ref, sb_ref, q_ref, k_ref, v_ref):
    xb = x_ref[...].astype(BF16)
    h = jnp.dot(xb, w_ref[...], preferred_element_type=F32)
    n_rot = (D_MODEL + KV_DIM) // LANES
    rot = _rotate_blocks(h, n_rot, cos_ref, sa_ref, sb_ref, HEAD_DIM // 2, 1)
    nq = D_MODEL // LANES
    for j in range(nq):
        q_ref[:, j * LANES:(j + 1) * LANES] = (rot[j] * ATTN_SCALE).astype(BF16)
    for j in range(nq, n_rot):
        k_ref[:, (j - nq) * LANES:(j - nq + 1) * LANES] = rot[j]
    v_ref[...] = h[:, D_MODEL + KV_DIM:]


def _qkv_proj(x, w, cos, sa, sb):
    n = x.shape[0]
    width = D_MODEL + 2 * KV_DIM
    return pl.pallas_call(
        _qkv_kernel,
        grid=(n // ROW_TILE,),
        in_specs=[_rows(D_MODEL), _resident((D_MODEL, width)), _rows(LANES), _rows(LANES), _rows(LANES)],
        out_specs=[_rows(D_MODEL), _rows(KV_DIM), _rows(KV_DIM)],
        out_shape=[jax.ShapeDtypeStruct((n, D_MODEL), BF16), jax.ShapeDtypeStruct((n, KV_DIM), F32),
                   jax.ShapeDtypeStruct((n, KV_DIM), F32)],
        compiler_params=_params("parallel"),
        name="attn_qkv",
    )(x, w, cos, sa, sb)


def _rope_tables(pos):
    half = HEAD_DIM // 2
    inv = ROPE_THETA ** (-jnp.arange(0, HEAD_DIM, 2, dtype=F32) / HEAD_DIM)
    ang = pos.astype(F32)[:, None] * inv[None, :]
    reps = LANES // half
    c = jnp.tile(jnp.cos(ang), (1, reps))
    s = jnp.tile(jnp.sin(ang), (1, reps))
    low = (jnp.arange(LANES) % HEAD_DIM) < half
    return c, jnp.where(low, -s, 0.0), jnp.where(low, 0.0, s)


def _sink_column(sinks_ref, kv, rows_per_head):
    return jnp.concatenate(
        [jnp.full((rows_per_head, 1), sinks_ref[kv * GROUP + g], F32) for g in range(GROUP)], axis=0)


def _swa_prompt_kernel(sinks_ref, q_ref, kp_ref, kc_ref, vp_ref, vc_ref, o_ref):
    i = pl.program_id(0)
    rows = GROUP * WINDOW
    qi = lax.broadcasted_iota(jnp.int32, (rows, 2 * WINDOW), 0) % WINDOW
    si = lax.broadcasted_iota(jnp.int32, (rows, 2 * WINDOW), 1)
    dist = qi + WINDOW - si
    kpos = (i - 1) * WINDOW + si
    valid = (dist >= jnp.where(kpos >= 0, 0, WINDOW)) & (dist < WINDOW)
    kall = jnp.concatenate([kp_ref[...], kc_ref[...]], axis=0).astype(BF16)
    vall = jnp.concatenate([vp_ref[...], vc_ref[...]], axis=0).astype(BF16)
    for kv in range(N_KV_HEADS):
        lo = kv * HEAD_DIM
        qg = jnp.concatenate(
            [q_ref[:, (kv * GROUP + g) * HEAD_DIM:(kv * GROUP + g + 1) * HEAD_DIM] for g in range(GROUP)],
            axis=0)
        s = lax.dot_general(qg, kall[:, lo:lo + HEAD_DIM], NT_DIMS, preferred_element_type=F32)
        s = jnp.where(valid, s, -jnp.inf)
        sink = _sink_column(sinks_ref, kv, WINDOW)
        m = jnp.maximum(jnp.max(s, axis=-1, keepdims=True), sink)
        e = jnp.exp(s - m)
        denom = jnp.sum(e, axis=-1, keepdims=True) + jnp.exp(sink - m)
        o = jnp.dot(e.astype(BF16), vall[:, lo:lo + HEAD_DIM], preferred_element_type=F32) / denom
        for g in range(GROUP):
            h = kv * GROUP + g
            o_ref[:, h * HEAD_DIM:(h + 1) * HEAD_DIM] = o[g * WINDOW:(g + 1) * WINDOW].astype(BF16)


def _swa_prompt(q, k, v, sinks):
    nb = SEQ // WINDOW
    cur = lambda i: (i, 0)
    prev = lambda i: (jnp.maximum(i - 1, 0), 0)
    return pl.pallas_call(
        _swa_prompt_kernel,
        grid=(nb,),
        in_specs=[pl.BlockSpec(memory_space=pltpu.SMEM),
                  pl.BlockSpec((WINDOW, D_MODEL), cur),
                  pl.BlockSpec((WINDOW, KV_DIM), prev), pl.BlockSpec((WINDOW, KV_DIM), cur),
                  pl.BlockSpec((WINDOW, KV_DIM), prev), pl.BlockSpec((WINDOW, KV_DIM), cur)],
        out_specs=pl.BlockSpec((WINDOW, D_MODEL), cur),
        out_shape=jax.ShapeDtypeStruct((SEQ, D_MODEL), BF16),
        compiler_params=_params("parallel"),
        name="swa_prompt",
    )(sinks, q, k, k, v, v)


SWA_SAMPLE_BATCH = 8


def _swa_sample_kernel(sinks_ref, q_ref, kn_ref, vn_ref, ck_ref, cv_ref, o_ref, nk_ref, nv_ref):
    L = DEC_SEQ
    rows = GROUP * L
    qi = lax.broadcasted_iota(jnp.int32, (rows, WINDOW), 0) % L
    ji = lax.broadcasted_iota(jnp.int32, (rows, WINDOW), 1)
    dist_c = qi + WINDOW - ji
    valid_c = (dist_c >= jnp.where(ji + (PAST_LEN - WINDOW) >= 0, 0, WINDOW)) & (dist_c < WINDOW)
    qn = lax.broadcasted_iota(jnp.int32, (rows, L), 0) % L
    jn = lax.broadcasted_iota(jnp.int32, (rows, L), 1)
    valid_n = (qn - jn >= 0) & (qn - jn < WINDOW)
    qall = q_ref[...].astype(F32)
    for b in range(SWA_SAMPLE_BATCH):
        r0 = b * L
        kc = ck_ref[b]
        vc = cv_ref[b]
        kn = kn_ref[r0:r0 + L, :]
        vn = vn_ref[r0:r0 + L, :]
        nk_ref[b, 0:WINDOW - L, :] = kc[L:, :]
        nk_ref[b, WINDOW - L:WINDOW, :] = kn
        nv_ref[b, 0:WINDOW - L, :] = vc[L:, :]
        nv_ref[b, WINDOW - L:WINDOW, :] = vn
        qb = qall[r0:r0 + L, :]
        kcb, vcb, knb, vnb = kc.astype(BF16), vc.astype(BF16), kn.astype(BF16), vn.astype(BF16)
        outs = []
        for kv in range(N_KV_HEADS):
            lo = kv * HEAD_DIM
            qg = jnp.concatenate(
                [qb[:, (kv * GROUP + g) * HEAD_DIM:(kv * GROUP + g + 1) * HEAD_DIM] for g in range(GROUP)],
                axis=0).astype(BF16)
            sc = lax.dot_general(qg, kcb[:, lo:lo + HEAD_DIM], NT_DIMS, preferred_element_type=F32)
            sn = lax.dot_general(qg, knb[:, lo:lo + HEAD_DIM], NT_DIMS, preferred_element_type=F32)
            sc = jnp.where(valid_c, sc, -jnp.inf)
            sn = jnp.where(valid_n, sn, -jnp.inf)
            sink = _sink_column(sinks_ref, kv, L)
            m = jnp.maximum(jnp.maximum(jnp.max(sc, axis=-1, keepdims=True),
                                        jnp.max(sn, axis=-1, keepdims=True)), sink)
            ec = jnp.exp(sc - m)
            en = jnp.exp(sn - m)
            denom = (jnp.sum(ec, axis=-1, keepdims=True) + jnp.sum(en, axis=-1, keepdims=True)
                     + jnp.exp(sink - m))
            o = (jnp.dot(ec.astype(BF16), vcb[:, lo:lo + HEAD_DIM], preferred_element_type=F32)
                 + jnp.dot(en.astype(BF16), vnb[:, lo:lo + HEAD_DIM], preferred_element_type=F32)) / denom
            outs.extend(o[g * L:(g + 1) * L] for g in range(GROUP))
        o_ref[r0:r0 + L, :] = jnp.concatenate(outs, axis=1)


def _swa_sample(q, k, v, cache_k, cache_v, sinks):
    bb = SWA_SAMPLE_BATCH
    rows = bb * DEC_SEQ
    off = SEQ // rows
    tok = lambda i: (i + off, 0)
    cache = lambda i: (i, 0, 0)
    return pl.pallas_call(
        _swa_sample_kernel,
        grid=(DEC_BATCH // bb,),
        in_specs=[pl.BlockSpec(memory_space=pltpu.SMEM),
                  pl.BlockSpec((rows, D_MODEL), tok), pl.BlockSpec((rows, KV_DIM), tok),
                  pl.BlockSpec((rows, KV_DIM), tok),
                  pl.BlockSpec((bb, WINDOW, KV_DIM), cache), pl.BlockSpec((bb, WINDOW, KV_DIM), cache)],
        out_specs=[pl.BlockSpec((rows, D_MODEL), lambda i: (i, 0)),
                   pl.BlockSpec((bb, WINDOW, KV_DIM), cache), pl.BlockSpec((bb, WINDOW, KV_DIM), cache)],
        out_shape=[jax.ShapeDtypeStruct((N_SAMPLE, D_MODEL), F32),
                   jax.ShapeDtypeStruct((DEC_BATCH, WINDOW, KV_DIM), F32),
                   jax.ShapeDtypeStruct((DEC_BATCH, WINDOW, KV_DIM), F32)],
        compiler_params=_params("parallel"),
        name="swa_sample",
    )(sinks, q, k, v, cache_k, cache_v)


def _ret_proj_kernel(x_ref, w_ref, cos_ref, sa_ref, sb_ref, q_ref, k_ref, v_ref, g_ref):
    xb = x_ref[...].astype(BF16)
    hqk = jnp.dot(xb, w_ref[:, :2 * RET_QK], preferred_element_type=F32)
    table_blocks = RET_QK_DIM // LANES
    rot = _rotate_blocks(hqk, 2 * RET_QK // LANES, cos_ref, sa_ref, sb_ref, 1, table_blocks)
    nq = RET_QK // LANES
    for j in range(nq):
        q_ref[:, j * LANES:(j + 1) * LANES] = rot[j].astype(BF16)
        k_ref[:, j * LANES:(j + 1) * LANES] = rot[nq + j] * (RET_QK_DIM ** -0.5)
    v_ref[...] = jnp.dot(xb, w_ref[:, 2 * RET_QK:2 * RET_QK + RET_V],
                         preferred_element_type=F32).astype(BF16)
    g_ref[...] = jnp.dot(xb, w_ref[:, 2 * RET_QK + RET_V:], preferred_element_type=F32)


def _ret_proj(x, w, cos, sa, sb):
    n = x.shape[0]
    width = 2 * RET_QK + 2 * RET_V
    return pl.pallas_call(
        _ret_proj_kernel,
        grid=(n // ROW_TILE,),
        in_specs=[_rows(D_MODEL), _resident((D_MODEL, width)),
                  _rows(RET_QK_DIM), _rows(RET_QK_DIM), _rows(RET_QK_DIM)],
        out_specs=[_rows(RET_QK), _rows(RET_QK), _rows(RET_V), _rows(RET_V)],
        out_shape=[jax.ShapeDtypeStruct((n, RET_QK), BF16), jax.ShapeDtypeStruct((n, RET_QK), F32),
                   jax.ShapeDtypeStruct((n, RET_V), BF16), jax.ShapeDtypeStruct((n, RET_V), F32)],
        compiler_params=_params("parallel"),
        name="ret_proj",
    )(x, w, cos, sa, sb)


def _xpos_tables(pos):
    d = RET_QK_DIM
    inv = 1.0 / (RET_THETA ** jnp.linspace(0.0, 1.0, d // 2, dtype=F32))
    ang = pos.astype(F32)[:, None] * inv[None, :]
    c = jnp.repeat(jnp.cos(ang), 2, axis=1)
    s = jnp.repeat(jnp.sin(ang), 2, axis=1)
    even = (jnp.arange(d) % 2) == 0
    return c, jnp.where(even, -s, 0.0), jnp.where(even, 0.0, s)


def _ret_log_decay():
    return jnp.log(1.0 - 2.0 ** (-5.0 - jnp.arange(RET_HEADS, dtype=F32)))


def _ret_tables(L):
    lg = _ret_log_decay()
    idx = jnp.arange(L, dtype=F32)
    diff = idx[:, None] - idx[None, :]
    causal = diff >= 0
    dmat = jnp.where(causal[None], jnp.exp(lg[:, None, None] * jnp.where(causal, diff, 0.0)[None]), 0.0)
    xi = jnp.exp(lg[:, None] * (idx[None, :] + 1.0))
    zeta = jnp.exp(lg[:, None] * (L - 1.0 - idx[None, :]))
    decay = jnp.exp(lg * L)
    rep = lambda a: jnp.broadcast_to(a[:, :, None], (RET_HEADS, L, LANES))
    return dmat, rep(xi), rep(zeta), decay


def _lanes(t, width):
    return jnp.concatenate([t] * (width // LANES), axis=1)


def _group_norm_gate(o, g):
    mu = jnp.mean(o, axis=-1, keepdims=True)
    d = o - mu
    var = jnp.mean(d * d, axis=-1, keepdims=True)
    return (_silu(g) * (d * lax.rsqrt(var + LN_EPS))).astype(BF16)


def _ret_prompt_kernel(q_ref, k_ref, v_ref, g_ref, dmat_ref, xi_ref, zeta_ref, decay_ref,
                       y_ref, sfin_ref, s_ref):
    i = pl.program_id(0)

    @pl.when(i == 0)
    def _():
        s_ref[...] = jnp.zeros_like(s_ref)

    for h in range(RET_HEADS):
        qh = q_ref[:, h * RET_QK_DIM:(h + 1) * RET_QK_DIM]
        kh = k_ref[:, h * RET_QK_DIM:(h + 1) * RET_QK_DIM]
        vh = v_ref[:, h * RET_V_DIM:(h + 1) * RET_V_DIM]
        s_old = s_ref[h]
        att = lax.dot_general(qh, kh.astype(BF16), NT_DIMS, preferred_element_type=F32) * dmat_ref[h]
        o = jnp.dot(att.astype(BF16), vh, preferred_element_type=F32)
        o = o + (jnp.dot(qh, s_old.astype(BF16), preferred_element_type=F32)
                 * _lanes(xi_ref[h], RET_V_DIM))
        kz = (kh * _lanes(zeta_ref[h], RET_QK_DIM)).astype(BF16)
        upd = lax.dot_general(kz, vh, TN_DIMS, preferred_element_type=F32)
        s_ref[h] = decay_ref[h] * s_old + upd
        y_ref[:, h * RET_V_DIM:(h + 1) * RET_V_DIM] = _group_norm_gate(
            o, g_ref[:, h * RET_V_DIM:(h + 1) * RET_V_DIM])

    @pl.when(i == pl.num_programs(0) - 1)
    def _():
        sfin_ref[...] = s_ref[...]


def _ret_prompt(q, k, v, g, tables):
    nc = SEQ // RET_CHUNK
    cur = lambda i: (i, 0)
    state_shape = (RET_HEADS, RET_QK_DIM, RET_V_DIM)
    return pl.pallas_call(
        _ret_prompt_kernel,
        grid=(nc,),
        in_specs=[pl.BlockSpec((RET_CHUNK, RET_QK), cur), pl.BlockSpec((RET_CHUNK, RET_QK), cur),
                  pl.BlockSpec((RET_CHUNK, RET_V), cur), pl.BlockSpec((RET_CHUNK, RET_V), cur),
                  _resident((RET_HEADS, RET_CHUNK, RET_CHUNK)), _resident((RET_HEADS, RET_CHUNK, LANES)),
                  _resident((RET_HEADS, RET_CHUNK, LANES)), pl.BlockSpec(memory_space=pltpu.SMEM)],
        out_specs=[pl.BlockSpec((RET_CHUNK, RET_V), cur), pl.BlockSpec(state_shape, lambda i: (0, 0, 0))],
        out_shape=[jax.ShapeDtypeStruct((SEQ, RET_V), BF16), jax.ShapeDtypeStruct(state_shape, F32)],
        scratch_shapes=[pltpu.VMEM(state_shape, F32)],
        compiler_params=_params("arbitrary"),
        name="ret_prompt",
    )(q, k, v, g, *tables)


RET_SAMPLE_BATCH = 2


def _ret_sample_kernel(q_ref, k_ref, v_ref, g_ref, s_in_ref, dmat_ref, xi_ref, zeta_ref, decay_ref,
                       y_ref, s_out_ref):
    L = DEC_SEQ
    qa = q_ref[...].astype(F32)
    va = v_ref[...].astype(F32)
    for b in range(RET_SAMPLE_BATCH):
        r0 = b * L
        for h in range(RET_HEADS):
            qh = qa[r0:r0 + L, h * RET_QK_DIM:(h + 1) * RET_QK_DIM].astype(BF16)
            kh = k_ref[r0:r0 + L, h * RET_QK_DIM:(h + 1) * RET_QK_DIM]
            vh = va[r0:r0 + L, h * RET_V_DIM:(h + 1) * RET_V_DIM].astype(BF16)
            s_old = s_in_ref[b, h]
            att = (lax.dot_general(qh, kh.astype(BF16), NT_DIMS, preferred_element_type=F32)
                   * dmat_ref[h])
            o = jnp.dot(att.astype(BF16), vh, preferred_element_type=F32)
            o = o + (jnp.dot(qh, s_old.astype(BF16), preferred_element_type=F32)
                     * _lanes(xi_ref[h], RET_V_DIM))
            kz = (kh * _lanes(zeta_ref[h], RET_QK_DIM)).astype(BF16)
            upd = lax.dot_general(kz, vh, TN_DIMS, preferred_element_type=F32)
            s_out_ref[b, h] = decay_ref[h] * s_old + upd
            y_ref[r0:r0 + L, h * RET_V_DIM:(h + 1) * RET_V_DIM] = _group_norm_gate(
                o, g_ref[r0:r0 + L, h * RET_V_DIM:(h + 1) * RET_V_DIM]).astype(F32)


def _ret_sample(q, k, v, g, state, tables):
    bb = RET_SAMPLE_BATCH
    rows = bb * DEC_SEQ
    off = SEQ // rows
    tok = lambda i: (i + off, 0)
    st = lambda i: (i, 0, 0, 0)
    state_block = (bb, RET_HEADS, RET_QK_DIM, RET_V_DIM)
    return pl.pallas_call(
        _ret_sample_kernel,
        grid=(DEC_BATCH // bb,),
        in_specs=[pl.BlockSpec((rows, RET_QK), tok), pl.BlockSpec((rows, RET_QK), tok),
                  pl.BlockSpec((rows, RET_V), tok), pl.BlockSpec((rows, RET_V), tok),
                  pl.BlockSpec(state_block, st),
                  _resident((RET_HEADS, DEC_SEQ, DEC_SEQ)), _resident((RET_HEADS, DEC_SEQ, LANES)),
                  _resident((RET_HEADS, DEC_SEQ, LANES)), pl.BlockSpec(memory_space=pltpu.SMEM)],
        out_specs=[pl.BlockSpec((rows, RET_V), lambda i: (i, 0)), pl.BlockSpec(state_block, st)],
        out_shape=[jax.ShapeDtypeStruct((N_SAMPLE, RET_V), F32),
                   jax.ShapeDtypeStruct((DEC_BATCH,) + state_block[1:], F32)],
        compiler_params=_params("parallel"),
        name="ret_sample",
    )(q, k, v, g, state, *tables)


def kernel(x_prompt, x_sample, cache_k_win, cache_v_win, state_ret, ffn1_w_gate, ffn1_w_up, ffn1_w_down,
           ffn2_w_gate, ffn2_w_up, ffn2_w_down, ln_g, ln_b, attn_w_qkv, attn_w_o, attn_sinks, ret_w_in,
           ret_w_o):
    bf = lambda w: w.astype(BF16)
    ln = lambda i, j: (ln_g[i, j][None, :], ln_b[i, j][None, :])
    pos = jnp.concatenate([jnp.arange(SEQ), jnp.tile(PAST_LEN + jnp.arange(DEC_SEQ), DEC_BATCH)])

    x = jnp.concatenate([x_prompt.reshape(SEQ, D_MODEL), x_sample.reshape(N_SAMPLE, D_MODEL)], axis=0)

    x = _half_ffn(x, bf(ffn1_w_gate[0]), bf(ffn1_w_up[0]), bf(ffn1_w_down[0]), *ln(0, 0))
    q, k, v = _qkv_proj(x, bf(attn_w_qkv[0]), *_rope_tables(pos))
    sinks = attn_sinks[0]
    o_p = _swa_prompt(q, k, v, sinks)
    o_s, nk_s, nv_s = _swa_sample(q, k, v, cache_k_win[0].reshape(DEC_BATCH, WINDOW, KV_DIM),
                                  cache_v_win[0].reshape(DEC_BATCH, WINDOW, KV_DIM), sinks)
    o = jnp.concatenate([o_p, o_s.astype(BF16)], axis=0)
    x = _mix_out(o, x, bf(attn_w_o[0]), *ln(0, 1))
    x = _half_ffn(x, bf(ffn2_w_gate[0]), bf(ffn2_w_up[0]), bf(ffn2_w_down[0]), *ln(0, 2))

    kv_shape = (1, 1, WINDOW, N_KV_HEADS, HEAD_DIM)
    new_k_p = k[SEQ - WINDOW:SEQ].reshape(kv_shape)
    new_v_p = v[SEQ - WINDOW:SEQ].reshape(kv_shape)
    new_k_s = nk_s.reshape(1, DEC_BATCH, WINDOW, N_KV_HEADS, HEAD_DIM)
    new_v_s = nv_s.reshape(1, DEC_BATCH, WINDOW, N_KV_HEADS, HEAD_DIM)

    x = _half_ffn(x, bf(ffn1_w_gate[1]), bf(ffn1_w_up[1]), bf(ffn1_w_down[1]), *ln(1, 0))
    rq, rk, rv, rg = _ret_proj(x, bf(ret_w_in[0]), *_xpos_tables(pos))
    y_p, s_p = _ret_prompt(rq, rk, rv, rg, _ret_tables(RET_CHUNK))
    y_s, s_s = _ret_sample(rq, rk, rv, rg, state_ret[0], _ret_tables(DEC_SEQ))
    y = jnp.concatenate([y_p, y_s.astype(BF16)], axis=0)
    x = _mix_out(y, x, bf(ret_w_o[0]), *ln(1, 1))
    x = _half_ffn(x, bf(ffn2_w_gate[1]), bf(ffn2_w_up[1]), bf(ffn2_w_down[1]), *ln(1, 2))

    return (x[:SEQ].reshape(1, SEQ, D_MODEL), x[SEQ:].reshape(DEC_BATCH, DEC_SEQ, D_MODEL),
            new_k_p, new_v_p, new_k_s, new_v_s, s_p[None, None], s_s[None])
```

```python
import math

import jax
import jax.numpy as jnp
import numpy as np
from jax import lax
from jax.experimental import pallas as pl
from jax.experimental.pallas import tpu as pltpu

F32 = jnp.float32
BF16 = jnp.bfloat16

D_MODEL = 1024
SEQ = 16384
DEPTH = 2
DEC_BATCH = 128
DEC_SEQ = 8
PAST_LEN = 16384
N_HEADS = 16
N_KV_HEADS = 4
GROUP = N_HEADS // N_KV_HEADS
HEAD_DIM = D_MODEL // N_HEADS
WINDOW = 128
ROPE_THETA = 10000.0
ATTN_SCALE = 1.0 / math.sqrt(HEAD_DIM)
RET_HEADS = 4
RET_QK_DIM = D_MODEL // RET_HEADS
RET_V_DIM = 2 * D_MODEL // RET_HEADS
RET_CHUNK = 128
RET_THETA = 10000.0
FFN_DIM = 2816
LN_EPS = 1e-5
ALPHA = (2.0 * DEPTH) ** 0.25

N_SAMPLE = DEC_BATCH * DEC_SEQ
N_TOK = SEQ + N_SAMPLE
KV_DIM = N_KV_HEADS * HEAD_DIM
RET_QK = RET_HEADS * RET_QK_DIM
RET_V = RET_HEADS * RET_V_DIM

LANES = 128
ROW_TILE = 512
FFN_CHUNK = 256
VMEM_LIMIT = 56 * 1024 * 1024

N_TILES = N_TOK // ROW_TILE
N_PROMPT_TILES = SEQ // ROW_TILE
KV_PAIRS = N_KV_HEADS // 2

NT_DIMS = (((1,), (1,)), ((), ()))
TN_DIMS = (((0,), (0,)), ((), ()))


def _params(*sem):
    return pltpu.CompilerParams(dimension_semantics=sem, vmem_limit_bytes=VMEM_LIMIT)


def _resident(shape):
    return pl.BlockSpec(shape, lambda *_: (0,) * len(shape), pipeline_mode=pl.Buffered(1))


def _rows(width, offset_blocks=0, tile=ROW_TILE):
    return pl.BlockSpec((tile, width), lambda i: (i + offset_blocks, 0))


_ALIASED = pl.BlockSpec(memory_space=pl.ANY)
_SMEM = pl.BlockSpec(memory_space=pltpu.SMEM)


def _layer_norm(y, g, b):
    mu = jnp.mean(y, axis=-1, keepdims=True)
    d = y - mu
    var = jnp.mean(d * d, axis=-1, keepdims=True)
    return d * lax.rsqrt(var + LN_EPS) * g + b


def _silu(x):
    return x / (1.0 + jnp.exp(-x))


def _ffn_kernel(x_ref, wg_ref, wu_ref, wd_ref, g_ref, b_ref, *rest):
    o_ref = rest[-1]
    x = x_ref[...]
    xb = x.astype(BF16)
    acc = None
    for c in range(FFN_DIM // FFN_CHUNK):
        sl = slice(c * FFN_CHUNK, (c + 1) * FFN_CHUNK)
        gate = jnp.dot(xb, wg_ref[:, sl], preferred_element_type=F32)
        up = jnp.dot(xb, wu_ref[:, sl], preferred_element_type=F32)
        h = (_silu(gate) * up).astype(BF16)
        d = jnp.dot(h, wd_ref[sl, :], preferred_element_type=F32)
        acc = d if acc is None else acc + d
    o_ref[...] = _layer_norm(ALPHA * x + 0.5 * acc, g_ref[...], b_ref[...])


def _half_ffn(x, weights, ln, *, in_off=0, n_blocks=N_TILES, out_rows=N_TOK, out_off=0, into=None):
    wg, wu, wd = weights
    in_specs = [_rows(D_MODEL, in_off), _resident((D_MODEL, FFN_DIM)), _resident((D_MODEL, FFN_DIM)),
                _resident((FFN_DIM, D_MODEL)), _resident((1, D_MODEL)), _resident((1, D_MODEL))]
    args = [x, wg, wu, wd, *ln]
    aliases = {}
    if into is not None:
        in_specs.append(_ALIASED)
        args.append(into)
        aliases = {len(args) - 1: 0}
    return pl.pallas_call(
        _ffn_kernel,
        grid=(n_blocks,),
        in_specs=in_specs,
        out_specs=_rows(D_MODEL, out_off),
        out_shape=jax.ShapeDtypeStruct((out_rows, D_MODEL), F32),
        input_output_aliases=aliases,
        compiler_params=_params("parallel"),
        name="half_ffn",
    )(*args)


def _mix_out_kernel(y_ref, x_ref, w_ref, g_ref, b_ref, o_ref):
    m = jnp.dot(y_ref[...], w_ref[...], preferred_element_type=F32)
    o_ref[...] = _layer_norm(ALPHA * x_ref[...] + m, g_ref[...], b_ref[...])


def _mix_out(y, x, w, ln):
    n, k = y.shape
    return pl.pallas_call(
        _mix_out_kernel,
        grid=(n // ROW_TILE,),
        in_specs=[_rows(k), _rows(D_MODEL), _resident((k, D_MODEL)),
                  _resident((1, D_MODEL)), _resident((1, D_MODEL))],
        out_specs=_rows(D_MODEL),
        out_shape=jax.ShapeDtypeStruct((n, D_MODEL), F32),
        compiler_params=_params("parallel"),
        name="mix_out",
    )(y, x, w, *ln)


def _angle_tables(inv_lane):
    base = jnp.concatenate([jnp.arange(N_PROMPT_TILES) * ROW_TILE,
                            jnp.full((N_TILES - N_PROMPT_TILES,), PAST_LEN)]).astype(F32)
    row = jnp.stack([jnp.arange(ROW_TILE), jnp.arange(ROW_TILE) % DEC_SEQ]).astype(F32)
    ta = base[:, None] * inv_lane[None, :]
    ra = row[:, :, None] * inv_lane[None, None, :]
    return jnp.stack([jnp.cos(ta), jnp.sin(ta)], axis=1), jnp.stack([jnp.cos(ra), jnp.sin(ra)], axis=1)


def _table_specs(width):
    return [pl.BlockSpec((1, 2, width), lambda i: (i, 0, 0)),
            pl.BlockSpec((1, 2, ROW_TILE, width), lambda i: (i // N_PROMPT_TILES, 0, 0, 0))]


def _tile_cos_sin(tt_ref, rt_ref):
    cb, sb = tt_ref[0, 0:1, :], tt_ref[0, 1:2, :]
    cr, sr = rt_ref[0, 0], rt_ref[0, 1]
    return cb * cr - sb * sr, sb * cr + cb * sr


def _rotate_blocks(h, n_blocks, cos, sin, shift, partner_above):
    width = cos.shape[1]
    sign = jnp.where(partner_above, -1.0, 1.0).astype(F32)
    sin = sin * jnp.concatenate([sign] * (width // LANES), axis=1)
    out = []
    for j in range(n_blocks):
        t = (j * LANES) % width
        blk = h[:, j * LANES:(j + 1) * LANES]
        partner = jnp.where(partner_above, pltpu.roll(blk, LANES - shift, 1), pltpu.roll(blk, shift, 1))
        out.append(blk * cos[:, t:t + LANES] + partner * sin[:, t:t + LANES])
    return out


def _qkv_kernel(x_ref, w_ref, tt_ref, rt_ref, q_ref, k_ref, v_ref):
    xb = x_ref[...].astype(BF16)
    h = jnp.dot(xb, w_ref[...], preferred_element_type=F32)
    cos, sin = _tile_cos_sin(tt_ref, rt_ref)
    lane = lax.broadcasted_iota(jnp.int32, (1, LANES), 1)
    n_rot = (D_MODEL + KV_DIM) // LANES
    rot = _rotate_blocks(h, n_rot, cos, sin, HEAD_DIM // 2, (lane % HEAD_DIM) < HEAD_DIM // 2)
    nq = D_MODEL // LANES
    for j in range(nq):
        q_ref[:, j * LANES:(j + 1) * LANES] = (rot[j] * ATTN_SCALE).astype(BF16)
    for j in range(nq, n_rot):
        k_ref[:, (j - nq) * LANES:(j - nq + 1) * LANES] = rot[j]
    v_ref[...] = h[:, D_MODEL + KV_DIM:]


def _qkv_proj(x, w, tables):
    n = x.shape[0]
    width = D_MODEL + 2 * KV_DIM
    return pl.pallas_call(
        _qkv_kernel,
        grid=(n // ROW_TILE,),
        in_specs=[_rows(D_MODEL), _resident((D_MODEL, width)), *_table_specs(LANES)],
        out_specs=[_rows(D_MODEL), _rows(KV_DIM), _rows(KV_DIM)],
        out_shape=[jax.ShapeDtypeStruct((n, D_MODEL), BF16), jax.ShapeDtypeStruct((n, KV_DIM), F32),
                   jax.ShapeDtypeStruct((n, KV_DIM), F32)],
        compiler_params=_params("parallel"),
        name="attn_qkv",
    )(x, w, *tables)


def _rope_inv_lane():
    inv = ROPE_THETA ** (-jnp.arange(0, HEAD_DIM, 2, dtype=F32) / HEAD_DIM)
    return jnp.tile(inv, LANES // (HEAD_DIM // 2))


def _pair_heads(w, axis):
    shape = w.shape
    w = w.reshape(shape[:axis] + (KV_PAIRS, 2, GROUP, HEAD_DIM) + shape[axis + 1:])
    return jnp.swapaxes(w, axis + 1, axis + 2).reshape(shape)


def _pair_operands(kall, vall):
    s = kall.shape[0]
    low = lax.broadcasted_iota(jnp.int32, (s, LANES), 1) < HEAD_DIM
    kb, vb = kall.astype(BF16), vall.astype(BF16)
    zero = jnp.zeros_like(kb)
    one_lo = jnp.where(low, 1.0, 0.0).astype(BF16)
    one_hi = jnp.where(low, 0.0, 1.0).astype(BF16)
    kcat = jnp.concatenate([jnp.where(low, kb, zero), jnp.where(low, zero, kb)], axis=0)
    vcat = jnp.concatenate([jnp.concatenate([jnp.where(low, vb, zero), one_lo], axis=1),
                            jnp.concatenate([jnp.where(low, zero, vb), one_hi], axis=1)], axis=0)
    return kcat, vcat


def _pair_softmax(s, sink_a, sink_b):
    half = s.shape[1] // 2
    sa, sb = s[:, :half], s[:, half:]
    ma = jnp.maximum(jnp.max(sa, axis=-1, keepdims=True), sink_a)
    mb = jnp.maximum(jnp.max(sb, axis=-1, keepdims=True), sink_b)
    e = jnp.concatenate([jnp.exp(sa - ma), jnp.exp(sb - mb)], axis=1).astype(BF16)
    low = lax.broadcasted_iota(jnp.int32, (s.shape[0], LANES), 1) < HEAD_DIM
    return e, jnp.where(low, jnp.exp(sink_a - ma), jnp.exp(sink_b - mb))


def _swa_prompt_kernel(sinks_ref, bias_ref, q_ref, kp_ref, kc_ref, vp_ref, vc_ref, *rest):
    o_ref, s_ref, e_ref, t_ref = rest[-4:]
    bias = bias_ref[0]
    for p in range(KV_PAIRS):
        lanes = slice(p * LANES, (p + 1) * LANES)
        kcat, vcat = _pair_operands(jnp.concatenate([kp_ref[:, lanes], kc_ref[:, lanes]], axis=0),
                                    jnp.concatenate([vp_ref[:, lanes], vc_ref[:, lanes]], axis=0))
        q = q_ref[:, p * GROUP * LANES:(p + 1) * GROUP * LANES]
        qs = jnp.concatenate([q[:, g * LANES:(g + 1) * LANES] for g in range(GROUP)], axis=0)
        s_ref[...] = lax.dot_general(qs, kcat, NT_DIMS, preferred_element_type=F32)
        for g in range(GROUP):
            rows = slice(g * WINDOW, (g + 1) * WINDOW)
            head_a = (2 * p) * GROUP + g
            e, t = _pair_softmax(s_ref[rows, :] + bias, sinks_ref[head_a], sinks_ref[head_a + GROUP])
            e_ref[rows, :] = e
            t_ref[rows, :] = t
        r = jnp.dot(e_ref[...], vcat, preferred_element_type=F32)
        o = r[:, :LANES] / (r[:, LANES:] + t_ref[...])
        for g in range(GROUP):
            blk = p * GROUP + g
            o_ref[:, blk * LANES:(blk + 1) * LANES] = o[g * WINDOW:(g + 1) * WINDOW].astype(BF16)


def _window_bias(n_rows, row_of, key_dist_and_pos):
    dist, kpos = key_dist_and_pos(row_of(np.arange(n_rows))[:, None])
    valid = (dist >= 0) & (dist < WINDOW) & (kpos >= 0)
    m = np.where(valid, 0.0, -np.inf).astype(np.float32)
    return np.concatenate([m, m], axis=1)


def _swa_prompt(q, k, v, sinks):
    nb = SEQ // WINDOW
    rows = GROUP * WINDOW
    si = np.arange(2 * WINDOW)[None, :]
    bias = jnp.asarray(np.stack([
        _window_bias(WINDOW, lambda r: r, lambda qi: (qi + WINDOW - si, blk * WINDOW - WINDOW + si))
        for blk in (0, 1)]))
    cur = lambda i: (i, 0)
    prev = lambda i: (jnp.maximum(i - 1, 0), 0)
    return pl.pallas_call(
        _swa_prompt_kernel,
        grid=(nb,),
        in_specs=[_SMEM, pl.BlockSpec((1, WINDOW, 4 * WINDOW), lambda i: (jnp.minimum(i, 1), 0, 0)),
                  pl.BlockSpec((WINDOW, D_MODEL), cur),
                  pl.BlockSpec((WINDOW, KV_DIM), prev), pl.BlockSpec((WINDOW, KV_DIM), cur),
                  pl.BlockSpec((WINDOW, KV_DIM), prev), pl.BlockSpec((WINDOW, KV_DIM), cur)],
        out_specs=pl.BlockSpec((WINDOW, D_MODEL), cur),
        out_shape=jax.ShapeDtypeStruct((N_TOK, D_MODEL), BF16),
        scratch_shapes=[pltpu.VMEM((rows, 4 * WINDOW), F32), pltpu.VMEM((rows, 4 * WINDOW), BF16),
                        pltpu.VMEM((rows, LANES), F32)],
        compiler_params=_params("parallel"),
        name="swa_prompt",
    )(sinks, bias, q, k, k, v, v)


SWA_SAMPLE_BATCH = 8
SWA_SAMPLE_KEYS = 2 * WINDOW


def _swa_sample_kernel(sinks_ref, bias_ref, q_ref, kn_ref, vn_ref, ck_ref, cv_ref, o_in_ref,
                       o_ref, nk_ref, nv_ref, o_scr):
    del o_in_ref
    L = DEC_SEQ
    bias = bias_ref[...]
    qall = q_ref[...].astype(F32)
    pad = jnp.zeros((SWA_SAMPLE_KEYS - WINDOW - L, LANES), F32)
    sink_cols = []
    for p in range(KV_PAIRS):
        col = lambda first: jnp.concatenate(
            [jnp.full((L, 1), sinks_ref[first + g], F32) for g in range(GROUP)], axis=0)
        sink_cols.append((col(2 * p * GROUP), col((2 * p + 1) * GROUP)))
    for b in range(SWA_SAMPLE_BATCH):
        r0 = b * L
        kc, vc = ck_ref[b], cv_ref[b]
        kn, vn = kn_ref[r0:r0 + L, :], vn_ref[r0:r0 + L, :]
        nk_ref[b, 0:WINDOW - L, :] = kc[L:, :]
        nk_ref[b, WINDOW - L:WINDOW, :] = kn
        nv_ref[b, 0:WINDOW - L, :] = vc[L:, :]
        nv_ref[b, WINDOW - L:WINDOW, :] = vn
        for p in range(KV_PAIRS):
            lanes = slice(p * LANES, (p + 1) * LANES)
            kcat, vcat = _pair_operands(jnp.concatenate([kc[:, lanes], kn[:, lanes], pad], axis=0),
                                        jnp.concatenate([vc[:, lanes], vn[:, lanes], pad], axis=0))
            qs = jnp.concatenate(
                [qall[r0:r0 + L, (p * GROUP + g) * LANES:(p * GROUP + g + 1) * LANES] for g in range(GROUP)],
                axis=0).astype(BF16)
            s = lax.dot_general(qs, kcat, NT_DIMS, preferred_element_type=F32) + bias
            e, t = _pair_softmax(s, *sink_cols[p])
            r = jnp.dot(e, vcat, preferred_element_type=F32)
            o = r[:, :LANES] / (r[:, LANES:] + t)
            for g in range(GROUP):
                blk = p * GROUP + g
                o_scr[r0:r0 + L, blk * LANES:(blk + 1) * LANES] = o[g * L:(g + 1) * L]
    o_ref[...] = o_scr[...].astype(BF16)


def _swa_sample(q, k, v, cache_k, cache_v, sinks, o_buf):
    bb = SWA_SAMPLE_BATCH
    rows = bb * DEC_SEQ
    off = SEQ // rows
    tok = lambda i: (i + off, 0)
    cache = lambda i: (i, 0, 0)
    cj = np.arange(SWA_SAMPLE_KEYS)[None, :]
    bias = jnp.asarray(_window_bias(
        GROUP * DEC_SEQ, lambda r: r % DEC_SEQ,
        lambda qi: (np.where(cj < WINDOW + DEC_SEQ, PAST_LEN + qi - (PAST_LEN - WINDOW + cj), -1),
                    PAST_LEN - WINDOW + cj)))
    return pl.pallas_call(
        _swa_sample_kernel,
        grid=(DEC_BATCH // bb,),
        in_specs=[_SMEM, _resident((GROUP * DEC_SEQ, 2 * SWA_SAMPLE_KEYS)),
                  pl.BlockSpec((rows, D_MODEL), tok), pl.BlockSpec((rows, KV_DIM), tok),
                  pl.BlockSpec((rows, KV_DIM), tok),
                  pl.BlockSpec((bb, WINDOW, KV_DIM), cache), pl.BlockSpec((bb, WINDOW, KV_DIM), cache),
                  _ALIASED],
        out_specs=[pl.BlockSpec((rows, D_MODEL), tok),
                   pl.BlockSpec((bb, WINDOW, KV_DIM), cache), pl.BlockSpec((bb, WINDOW, KV_DIM), cache)],
        out_shape=[jax.ShapeDtypeStruct((N_TOK, D_MODEL), BF16),
                   jax.ShapeDtypeStruct((DEC_BATCH, WINDOW, KV_DIM), F32),
                   jax.ShapeDtypeStruct((DEC_BATCH, WINDOW, KV_DIM), F32)],
        scratch_shapes=[pltpu.VMEM((rows, D_MODEL), F32)],
        input_output_aliases={7: 0},
        compiler_params=_params("parallel"),
        name="swa_sample",
    )(sinks, bias, q, k, v, cache_k, cache_v, o_buf)


def _ret_proj_kernel(x_ref, w_ref, tt_ref, rt_ref, q_ref, k_ref, v_ref, g_ref):
    xb = x_ref[...].astype(BF16)
    hqk = jnp.dot(xb, w_ref[:, :2 * RET_QK], preferred_element_type=F32)
    cos, sin = _tile_cos_sin(tt_ref, rt_ref)
    lane = lax.broadcasted_iota(jnp.int32, (1, LANES), 1)
    rot = _rotate_blocks(hqk, 2 * RET_QK // LANES, cos, sin, 1, (lane % 2) == 0)
    nq = RET_QK // LANES
    for j in range(nq):
        q_ref[:, j * LANES:(j + 1) * LANES] = rot[j].astype(BF16)
        k_ref[:, j * LANES:(j + 1) * LANES] = rot[nq + j] * (RET_QK_DIM ** -0.5)
    v_ref[...] = jnp.dot(xb, w_ref[:, 2 * RET_QK:2 * RET_QK + RET_V],
                         preferred_element_type=F32).astype(BF16)
    g_ref[...] = jnp.dot(xb, w_ref[:, 2 * RET_QK + RET_V:], preferred_element_type=F32)


def _ret_proj(x, w, tables):
    n = x.shape[0]
    width = 2 * RET_QK + 2 * RET_V
    return pl.pallas_call(
        _ret_proj_kernel,
        grid=(n // ROW_TILE,),
        in_specs=[_rows(D_MODEL), _resident((D_MODEL, width)), *_table_specs(RET_QK_DIM)],
        out_specs=[_rows(RET_QK), _rows(RET_QK), _rows(RET_V), _rows(RET_V)],
        out_shape=[jax.ShapeDtypeStruct((n, RET_QK), BF16), jax.ShapeDtypeStruct((n, RET_QK), F32),
                   jax.ShapeDtypeStruct((n, RET_V), BF16), jax.ShapeDtypeStruct((n, RET_V), F32)],
        compiler_params=_params("parallel"),
        name="ret_proj",
    )(x, w, *tables)


def _xpos_inv_lane():
    inv = 1.0 / (RET_THETA ** jnp.linspace(0.0, 1.0, RET_QK_DIM // 2, dtype=F32))
    return jnp.repeat(inv, 2)


def _ret_log_decay():
    return jnp.log(1.0 - 2.0 ** (-5.0 - jnp.arange(RET_HEADS, dtype=F32)))


def _ret_tables(L):
    lg = _ret_log_decay()
    idx = jnp.arange(L, dtype=F32)
    diff = idx[:, None] - idx[None, :]
    causal = diff >= 0
    dmat = jnp.where(causal[None], jnp.exp(lg[:, None, None] * jnp.where(causal, diff, 0.0)[None]), 0.0)
    xi = jnp.exp(lg[:, None] * (idx[None, :] + 1.0))
    zeta = jnp.exp(lg[:, None] * (L - 1.0 - idx[None, :]))
    decay = jnp.exp(lg * L)
    rep = lambda a: jnp.broadcast_to(a[:, :, None], (RET_HEADS, L, LANES))
    return dmat, rep(xi), rep(zeta), decay


def _lanes(t, width):
    return jnp.concatenate([t] * (width // LANES), axis=1)


def _group_norm_gate(o, g):
    mu = jnp.mean(o, axis=-1, keepdims=True)
    d = o - mu
    var = jnp.mean(d * d, axis=-1, keepdims=True)
    return _silu(g) * (d * lax.rsqrt(var + LN_EPS))


def _retention_head(qh, kh, vh, s_old, dmat, xi, zeta, decay):
    att = lax.dot_general(qh, kh.astype(BF16), NT_DIMS, preferred_element_type=F32) * dmat
    o = jnp.dot(att.astype(BF16), vh, preferred_element_type=F32)
    o = o + jnp.dot(qh, s_old.astype(BF16), preferred_element_type=F32) * _lanes(xi, RET_V_DIM)
    kz = (kh * _lanes(zeta, RET_QK_DIM)).astype(BF16)
    upd = lax.dot_general(kz, vh, TN_DIMS, preferred_element_type=F32)
    return o, decay * s_old + upd


def _ret_prompt_kernel(q_ref, k_ref, v_ref, g_ref, dmat_ref, xi_ref, zeta_ref, decay_ref,
                       y_ref, sfin_ref, s_ref):
    i = pl.program_id(0)

    @pl.when(i == 0)
    def _():
        s_ref[...] = jnp.zeros_like(s_ref)

    for h in range(RET_HEADS):
        qk = slice(h * RET_QK_DIM, (h + 1) * RET_QK_DIM)
        vg = slice(h * RET_V_DIM, (h + 1) * RET_V_DIM)
        o, s_new = _retention_head(q_ref[:, qk], k_ref[:, qk], v_ref[:, vg], s_ref[h],
                                   dmat_ref[h], xi_ref[h], zeta_ref[h], decay_ref[h])
        s_ref[h] = s_new
        y_ref[:, vg] = _group_norm_gate(o, g_ref[:, vg]).astype(BF16)

    @pl.when(i == pl.num_programs(0) - 1)
    def _():
        sfin_ref[...] = s_ref[...]


def _ret_prompt(q, k, v, g, tables):
    nc = SEQ // RET_CHUNK
    cur = lambda i: (i, 0)
    state_shape = (RET_HEADS, RET_QK_DIM, RET_V_DIM)
    return pl.pallas_call(
        _ret_prompt_kernel,
        grid=(nc,),
        in_specs=[pl.BlockSpec((RET_CHUNK, RET_QK), cur), pl.BlockSpec((RET_CHUNK, RET_QK), cur),
                  pl.BlockSpec((RET_CHUNK, RET_V), cur), pl.BlockSpec((RET_CHUNK, RET_V), cur),
                  _resident((RET_HEADS, RET_CHUNK, RET_CHUNK)), _resident((RET_HEADS, RET_CHUNK, LANES)),
                  _resident((RET_HEADS, RET_CHUNK, LANES)), _SMEM],
        out_specs=[pl.BlockSpec((RET_CHUNK, RET_V), cur), pl.BlockSpec(state_shape, lambda i: (0, 0, 0))],
        out_shape=[jax.ShapeDtypeStruct((N_TOK, RET_V), BF16), jax.ShapeDtypeStruct(state_shape, F32)],
        scratch_shapes=[pltpu.VMEM(state_shape, F32)],
        compiler_params=_params("arbitrary"),
        name="ret_prompt",
    )(q, k, v, g, *tables)


RET_SAMPLE_BATCH = 2


def _ret_sample_kernel(q_ref, k_ref, v_ref, g_ref, s_in_ref, dmat_ref, xi_ref, zeta_ref, decay_ref,
                       y_in_ref, y_ref, s_out_ref, y_scr):
    del y_in_ref
    L = DEC_SEQ
    qa = q_ref[...].astype(F32)
    va = v_ref[...].astype(F32)
    for b in range(RET_SAMPLE_BATCH):
        rows = slice(b * L, (b + 1) * L)
        for h in range(RET_HEADS):
            qk = slice(h * RET_QK_DIM, (h + 1) * RET_QK_DIM)
            vg = slice(h * RET_V_DIM, (h + 1) * RET_V_DIM)
            o, s_new = _retention_head(qa[rows, qk].astype(BF16), k_ref[rows, qk], va[rows, vg].astype(BF16),
                                       s_in_ref[b, h], dmat_ref[h], xi_ref[h], zeta_ref[h], decay_ref[h])
            s_out_ref[b, h] = s_new
            y_scr[rows, vg] = _group_norm_gate(o, g_ref[rows, vg])
    y_ref[...] = y_scr[...].astype(BF16)


def _ret_sample(q, k, v, g, state, tables, y_buf):
    bb = RET_SAMPLE_BATCH
    rows = bb * DEC_SEQ
    off = SEQ // rows
    tok = lambda i: (i + off, 0)
    st = lambda i: (i, 0, 0, 0)
    state_block = (bb, RET_HEADS, RET_QK_DIM, RET_V_DIM)
    return pl.pallas_call(
        _ret_sample_kernel,
        grid=(DEC_BATCH // bb,),
        in_specs=[pl.BlockSpec((rows, RET_QK), tok), pl.BlockSpec((rows, RET_QK), tok),
                  pl.BlockSpec((rows, RET_V), tok), pl.BlockSpec((rows, RET_V), tok),
                  pl.BlockSpec(state_block, st),
                  _resident((RET_HEADS, DEC_SEQ, DEC_SEQ)), _resident((RET_HEADS, DEC_SEQ, LANES)),
                  _resident((RET_HEADS, DEC_SEQ, LANES)), _SMEM, _ALIASED],
        out_specs=[pl.BlockSpec((rows, RET_V), tok), pl.BlockSpec(state_block, st)],
        out_shape=[jax.ShapeDtypeStruct((N_TOK, RET_V), BF16),
                   jax.ShapeDtypeStruct((DEC_BATCH,) + state_block[1:], F32)],
        scratch_shapes=[pltpu.VMEM((rows, RET_V), F32)],
        input_output_aliases={9: 0},
        compiler_params=_params("parallel"),
        name="ret_sample",
    )(q, k, v, g, state, *tables, y_buf)


def kernel(x_prompt, x_sample, cache_k_win, cache_v_win, state_ret, ffn1_w_gate, ffn1_w_up, ffn1_w_down,
           ffn2_w_gate, ffn2_w_up, ffn2_w_down, ln_g, ln_b, attn_w_qkv, attn_w_o, attn_sinks, ret_w_in,
           ret_w_o):
    bf = lambda w: w.astype(BF16)
    ln = lambda i, j: (ln_g[i, j][None, :], ln_b[i, j][None, :])
    ffn1 = lambda i: (bf(ffn1_w_gate[i]), bf(ffn1_w_up[i]), bf(ffn1_w_down[i]))
    ffn2 = lambda i: (bf(ffn2_w_gate[i]), bf(ffn2_w_up[i]), bf(ffn2_w_down[i]))
    n_sample_tiles = N_TILES - N_PROMPT_TILES

    w = ffn1(0)
    x = _half_ffn(x_prompt.reshape(SEQ, D_MODEL), w, ln(0, 0), n_blocks=N_PROMPT_TILES)
    x = _half_ffn(x_sample.reshape(N_SAMPLE, D_MODEL), w, ln(0, 0), n_blocks=n_sample_tiles,
                  out_off=N_PROMPT_TILES, into=x)
    w_qkv = jnp.concatenate([_pair_heads(attn_w_qkv[0][:, :D_MODEL], 1), attn_w_qkv[0][:, D_MODEL:]], axis=1)
    w_o = _pair_heads(attn_w_o[0], 0)
    q, k, v = _qkv_proj(x, bf(w_qkv), _angle_tables(_rope_inv_lane()))
    sinks = attn_sinks[0]
    o = _swa_prompt(q, k, v, sinks)
    o, nk_s, nv_s = _swa_sample(q, k, v, cache_k_win[0].reshape(DEC_BATCH, WINDOW, KV_DIM),
                                cache_v_win[0].reshape(DEC_BATCH, WINDOW, KV_DIM), sinks, o)
    x = _mix_out(o, x, bf(w_o), ln(0, 1))
    x = _half_ffn(x, ffn2(0), ln(0, 2))

    kv_shape = (1, 1, WINDOW, N_KV_HEADS, HEAD_DIM)
    new_k_p = k[SEQ - WINDOW:SEQ].reshape(kv_shape)
    new_v_p = v[SEQ - WINDOW:SEQ].reshape(kv_shape)
    new_k_s = nk_s.reshape(1, DEC_BATCH, WINDOW, N_KV_HEADS, HEAD_DIM)
    new_v_s = nv_s.reshape(1, DEC_BATCH, WINDOW, N_KV_HEADS, HEAD_DIM)

    x = _half_ffn(x, ffn1(1), ln(1, 0))
    rq, rk, rv, rg = _ret_proj(x, bf(ret_w_in[0]), _angle_tables(_xpos_inv_lane()))
    y, s_p = _ret_prompt(rq, rk, rv, rg, _ret_tables(RET_CHUNK))
    y, s_s = _ret_sample(rq, rk, rv, rg, state_ret[0], _ret_tables(DEC_SEQ), y)
    x = _mix_out(y, x, bf(ret_w_o[0]), ln(1, 1))
    w = ffn2(1)
    y_p = _half_ffn(x, w, ln(1, 2), n_blocks=N_PROMPT_TILES, out_rows=SEQ)
    y_s = _half_ffn(x, w, ln(1, 2), in_off=N_PROMPT_TILES, n_blocks=n_sample_tiles, out_rows=N_SAMPLE)

    return (y_p.reshape(1, SEQ, D_MODEL), y_s.reshape(DEC_BATCH, DEC_SEQ, D_MODEL),
            new_k_p, new_v_p, new_k_s, new_v_s, s_p[None, None], s_s[None])
```

```python
import functools
import math

import jax
import jax.numpy as jnp
import numpy as np
from jax import lax
from jax.experimental import pallas as pl
from jax.experimental.pallas import tpu as pltpu

F32 = jnp.float32
BF16 = jnp.bfloat16

D_MODEL = 1024
SEQ = 16384
DEPTH = 2
DEC_BATCH = 128
DEC_SEQ = 8
PAST_LEN = 16384
N_HEADS = 16
N_KV_HEADS = 4
GROUP = N_HEADS // N_KV_HEADS
HEAD_DIM = D_MODEL // N_HEADS
WINDOW = 128
ROPE_THETA = 10000.0
ATTN_SCALE = 1.0 / math.sqrt(HEAD_DIM)
RET_HEADS = 4
RET_QK_DIM = D_MODEL // RET_HEADS
RET_V_DIM = 2 * D_MODEL // RET_HEADS
RET_CHUNK = 256
RET_THETA = 10000.0
FFN_DIM = 2816
LN_EPS = 1e-5
ALPHA = (2.0 * DEPTH) ** 0.25

N_SAMPLE = DEC_BATCH * DEC_SEQ
N_TOK = SEQ + N_SAMPLE
KV_DIM = N_KV_HEADS * HEAD_DIM
RET_QK = RET_HEADS * RET_QK_DIM
RET_V = RET_HEADS * RET_V_DIM

LANES = 128
ROW_TILE = 512
FFN_CHUNK = 256
VMEM_LIMIT = 56 * 1024 * 1024

N_TILES = N_TOK // ROW_TILE
N_PROMPT_TILES = SEQ // ROW_TILE
KV_PAIRS = N_KV_HEADS // 2

NT_DIMS = (((1,), (1,)), ((), ()))
TN_DIMS = (((0,), (0,)), ((), ()))


def _params(*sem):
    return pltpu.CompilerParams(dimension_semantics=sem, vmem_limit_bytes=VMEM_LIMIT)


def _resident(shape):
    return pl.BlockSpec(shape, lambda *_: (0,) * len(shape), pipeline_mode=pl.Buffered(1))


def _rows(width, offset_blocks=0, tile=ROW_TILE):
    return pl.BlockSpec((tile, width), lambda i: (i + offset_blocks, 0))


_ALIASED = pl.BlockSpec(memory_space=pl.ANY)
_SMEM = pl.BlockSpec(memory_space=pltpu.SMEM)


def _layer_norm(y, g, b):
    mu = jnp.mean(y, axis=-1, keepdims=True)
    d = y - mu
    var = jnp.mean(d * d, axis=-1, keepdims=True)
    return d * lax.rsqrt(var + LN_EPS) * g + b


def _silu(x):
    return x / (1.0 + jnp.exp(-x))


def _ffn_kernel(split_in, split_out, *refs):
    n_in = 2 if split_in else 1
    x_refs, (wg_ref, wu_ref, wd_ref, g_ref, b_ref), o_refs = refs[:n_in], refs[n_in:n_in + 5], refs[n_in + 5:]
    i = pl.program_id(0)
    x = x_refs[0][...]
    if split_in:
        x = jnp.where(i < N_PROMPT_TILES, x, x_refs[1][...])
    xb = x.astype(BF16)
    acc = None
    for c in range(FFN_DIM // FFN_CHUNK):
        sl = slice(c * FFN_CHUNK, (c + 1) * FFN_CHUNK)
        gate = jnp.dot(xb, wg_ref[:, sl].astype(BF16), preferred_element_type=F32)
        up = jnp.dot(xb, wu_ref[:, sl].astype(BF16), preferred_element_type=F32)
        h = (_silu(gate) * up).astype(BF16)
        d = jnp.dot(h, wd_ref[sl, :].astype(BF16), preferred_element_type=F32)
        acc = d if acc is None else acc + d
    y = _layer_norm(ALPHA * x + 0.5 * acc, g_ref[...], b_ref[...])
    if split_out:
        @pl.when(i < N_PROMPT_TILES)
        def _():
            o_refs[0][...] = y

        @pl.when(i >= N_PROMPT_TILES)
        def _():
            o_refs[1][...] = y
    else:
        o_refs[0][...] = y


_PROMPT_TILE = lambda i: (jnp.minimum(i, N_PROMPT_TILES - 1), 0)
_SAMPLE_TILE = lambda i: (jnp.maximum(i - N_PROMPT_TILES, 0), 0)


def _layer_weight(shape, layer):
    return pl.BlockSpec((None,) + shape, lambda i: (layer, 0, 0), pipeline_mode=pl.Buffered(1))


def _half_ffn(xs, weights, layer, ln, *, split_out=False):
    split_in = len(xs) == 2
    x_specs = ([pl.BlockSpec((ROW_TILE, D_MODEL), _PROMPT_TILE), pl.BlockSpec((ROW_TILE, D_MODEL), _SAMPLE_TILE)]
               if split_in else [_rows(D_MODEL)])
    if split_out:
        out_specs = [pl.BlockSpec((ROW_TILE, D_MODEL), _PROMPT_TILE), pl.BlockSpec((ROW_TILE, D_MODEL), _SAMPLE_TILE)]
        out_shape = [jax.ShapeDtypeStruct((SEQ, D_MODEL), F32), jax.ShapeDtypeStruct((N_SAMPLE, D_MODEL), F32)]
    else:
        out_specs = _rows(D_MODEL)
        out_shape = jax.ShapeDtypeStruct((N_TOK, D_MODEL), F32)
    return pl.pallas_call(
        functools.partial(_ffn_kernel, split_in, split_out),
        grid=(N_TILES,),
        in_specs=x_specs + [_layer_weight((D_MODEL, FFN_DIM), layer), _layer_weight((D_MODEL, FFN_DIM), layer),
                            _layer_weight((FFN_DIM, D_MODEL), layer),
                            _resident((1, D_MODEL)), _resident((1, D_MODEL))],
        out_specs=out_specs,
        out_shape=out_shape,
        compiler_params=_params("arbitrary" if split_out else "parallel"),
        name="half_ffn",
    )(*xs, *weights, *ln)


def _mix_out_kernel(y_ref, x_ref, w_ref, g_ref, b_ref, o_ref):
    m = jnp.dot(y_ref[...], w_ref[...].astype(BF16), preferred_element_type=F32)
    o_ref[...] = _layer_norm(ALPHA * x_ref[...] + m, g_ref[...], b_ref[...])


def _mix_out(y, x, w, ln):
    n, k = y.shape
    return pl.pallas_call(
        _mix_out_kernel,
        grid=(n // ROW_TILE,),
        in_specs=[_rows(k), _rows(D_MODEL), _layer_weight((k, D_MODEL), 0),
                  _resident((1, D_MODEL)), _resident((1, D_MODEL))],
        out_specs=_rows(D_MODEL),
        out_shape=jax.ShapeDtypeStruct((n, D_MODEL), F32),
        compiler_params=_params("parallel"),
        name="mix_out",
    )(y, x, w, *ln)


def _angle_tables(inv_lane):
    base = jnp.concatenate([jnp.arange(N_PROMPT_TILES) * ROW_TILE,
                            jnp.full((N_TILES - N_PROMPT_TILES,), PAST_LEN)]).astype(F32)
    row = jnp.stack([jnp.arange(ROW_TILE), jnp.arange(ROW_TILE) % DEC_SEQ]).astype(F32)
    ta = base[:, None] * inv_lane[None, :]
    ra = row[:, :, None] * inv_lane[None, None, :]
    return jnp.stack([jnp.cos(ta), jnp.sin(ta)], axis=1), jnp.stack([jnp.cos(ra), jnp.sin(ra)], axis=1)


def _table_specs(width):
    return [pl.BlockSpec((1, 2, width), lambda i: (i, 0, 0)),
            pl.BlockSpec((1, 2, ROW_TILE, width), lambda i: (i // N_PROMPT_TILES, 0, 0, 0))]


def _tile_cos_sin(tt_ref, rt_ref):
    cb, sb = tt_ref[0, 0:1, :], tt_ref[0, 1:2, :]
    cr, sr = rt_ref[0, 0], rt_ref[0, 1]
    return cb * cr - sb * sr, sb * cr + cb * sr


def _rotate_blocks(h, n_blocks, cos, sin, shift, partner_above):
    width = cos.shape[1]
    sign = jnp.where(partner_above, -1.0, 1.0).astype(F32)
    sin = sin * jnp.concatenate([sign] * (width // LANES), axis=1)
    out = []
    for j in range(n_blocks):
        t = (j * LANES) % width
        blk = h[:, j * LANES:(j + 1) * LANES]
        partner = jnp.where(partner_above, pltpu.roll(blk, LANES - shift, 1), pltpu.roll(blk, shift, 1))
        out.append(blk * cos[:, t:t + LANES] + partner * sin[:, t:t + LANES])
    return out


def _qkv_kernel(x_ref, w_ref, tt_ref, rt_ref, q_ref, k_ref, v_ref):
    xb = x_ref[...].astype(BF16)
    h = jnp.dot(xb, w_ref[...].astype(BF16), preferred_element_type=F32)
    cos, sin = _tile_cos_sin(tt_ref, rt_ref)
    lane = lax.broadcasted_iota(jnp.int32, (1, LANES), 1)
    n_rot = (D_MODEL + KV_DIM) // LANES
    rot = _rotate_blocks(h, n_rot, cos, sin, HEAD_DIM // 2, (lane % HEAD_DIM) < HEAD_DIM // 2)
    nq = D_MODEL // LANES
    for j in range(nq):
        q_ref[:, j * LANES:(j + 1) * LANES] = (rot[j] * ATTN_SCALE).astype(BF16)
    for j in range(nq, n_rot):
        k_ref[:, (j - nq) * LANES:(j - nq + 1) * LANES] = rot[j]
    v_ref[...] = h[:, D_MODEL + KV_DIM:]


def _qkv_proj(x, w, tables):
    n = x.shape[0]
    width = D_MODEL + 2 * KV_DIM
    return pl.pallas_call(
        _qkv_kernel,
        grid=(n // ROW_TILE,),
        in_specs=[_rows(D_MODEL), _resident((D_MODEL, width)), *_table_specs(LANES)],
        out_specs=[_rows(D_MODEL), _rows(KV_DIM), _rows(KV_DIM)],
        out_shape=[jax.ShapeDtypeStruct((n, D_MODEL), BF16), jax.ShapeDtypeStruct((n, KV_DIM), F32),
                   jax.ShapeDtypeStruct((n, KV_DIM), F32)],
        compiler_params=_params("parallel"),
        name="attn_qkv",
    )(x, w, *tables)


def _rope_inv_lane():
    inv = ROPE_THETA ** (-jnp.arange(0, HEAD_DIM, 2, dtype=F32) / HEAD_DIM)
    return jnp.tile(inv, LANES // (HEAD_DIM // 2))


def _pair_heads(w, axis):
    shape = w.shape
    w = w.reshape(shape[:axis] + (KV_PAIRS, 2, GROUP, HEAD_DIM) + shape[axis + 1:])
    return jnp.swapaxes(w, axis + 1, axis + 2).reshape(shape)


def _pair_operands(kall, vall):
    s = kall.shape[0]
    low = lax.broadcasted_iota(jnp.int32, (s, LANES), 1) < HEAD_DIM
    kb, vb = kall.astype(BF16), vall.astype(BF16)
    zero = jnp.zeros_like(kb)
    one_lo = jnp.where(low, 1.0, 0.0).astype(BF16)
    one_hi = jnp.where(low, 0.0, 1.0).astype(BF16)
    kcat = jnp.concatenate([jnp.where(low, kb, zero), jnp.where(low, zero, kb)], axis=0)
    vcat = jnp.concatenate([jnp.concatenate([jnp.where(low, vb, zero), one_lo], axis=1),
                            jnp.concatenate([jnp.where(low, zero, vb), one_hi], axis=1)], axis=0)
    return kcat, vcat


def _pair_softmax(s, sink_a, sink_b):
    half = s.shape[1] // 2
    sa, sb = s[:, :half], s[:, half:]
    ma = jnp.maximum(jnp.max(sa, axis=-1, keepdims=True), sink_a)
    mb = jnp.maximum(jnp.max(sb, axis=-1, keepdims=True), sink_b)
    e = jnp.concatenate([jnp.exp(sa - ma), jnp.exp(sb - mb)], axis=1).astype(BF16)
    low = lax.broadcasted_iota(jnp.int32, (s.shape[0], LANES), 1) < HEAD_DIM
    return e, jnp.where(low, jnp.exp(sink_a - ma), jnp.exp(sink_b - mb))


def _swa_prompt_kernel(sinks_ref, bias_ref, q_ref, kp_ref, kc_ref, vp_ref, vc_ref, *rest):
    o_ref, s_ref, e_ref, t_ref = rest[-4:]
    bias = bias_ref[0]
    for p in range(KV_PAIRS):
        lanes = slice(p * LANES, (p + 1) * LANES)
        kcat, vcat = _pair_operands(jnp.concatenate([kp_ref[:, lanes], kc_ref[:, lanes]], axis=0),
                                    jnp.concatenate([vp_ref[:, lanes], vc_ref[:, lanes]], axis=0))
        q = q_ref[:, p * GROUP * LANES:(p + 1) * GROUP * LANES]
        qs = jnp.concatenate([q[:, g * LANES:(g + 1) * LANES] for g in range(GROUP)], axis=0)
        s_ref[...] = lax.dot_general(qs, kcat, NT_DIMS, preferred_element_type=F32)
        for g in range(GROUP):
            rows = slice(g * WINDOW, (g + 1) * WINDOW)
            head_a = (2 * p) * GROUP + g
            e, t = _pair_softmax(s_ref[rows, :] + bias, sinks_ref[head_a], sinks_ref[head_a + GROUP])
            e_ref[rows, :] = e
            t_ref[rows, :] = t
        r = jnp.dot(e_ref[...], vcat, preferred_element_type=F32)
        o = r[:, :LANES] / (r[:, LANES:] + t_ref[...])
        for g in range(GROUP):
            blk = p * GROUP + g
            o_ref[:, blk * LANES:(blk + 1) * LANES] = o[g * WINDOW:(g + 1) * WINDOW].astype(BF16)


def _window_bias(n_rows, row_of, key_dist_and_pos):
    dist, kpos = key_dist_and_pos(row_of(np.arange(n_rows))[:, None])
    valid = (dist >= 0) & (dist < WINDOW) & (kpos >= 0)
    m = np.where(valid, 0.0, -np.inf).astype(np.float32)
    return np.concatenate([m, m], axis=1)


def _swa_prompt(q, k, v, sinks):
    nb = SEQ // WINDOW
    rows = GROUP * WINDOW
    si = np.arange(2 * WINDOW)[None, :]
    bias = jnp.asarray(np.stack([
        _window_bias(WINDOW, lambda r: r, lambda qi: (qi + WINDOW - si, blk * WINDOW - WINDOW + si))
        for blk in (0, 1)]))
    cur = lambda i: (i, 0)
    prev = lambda i: (jnp.maximum(i - 1, 0), 0)
    return pl.pallas_call(
        _swa_prompt_kernel,
        grid=(nb,),
        in_specs=[_SMEM, pl.BlockSpec((1, WINDOW, 4 * WINDOW), lambda i: (jnp.minimum(i, 1), 0, 0)),
                  pl.BlockSpec((WINDOW, D_MODEL), cur),
                  pl.BlockSpec((WINDOW, KV_DIM), prev), pl.BlockSpec((WINDOW, KV_DIM), cur),
                  pl.BlockSpec((WINDOW, KV_DIM), prev), pl.BlockSpec((WINDOW, KV_DIM), cur)],
        out_specs=pl.BlockSpec((WINDOW, D_MODEL), cur),
        out_shape=jax.ShapeDtypeStruct((N_TOK, D_MODEL), BF16),
        scratch_shapes=[pltpu.VMEM((rows, 4 * WINDOW), F32), pltpu.VMEM((rows, 4 * WINDOW), BF16),
                        pltpu.VMEM((rows, LANES), F32)],
        compiler_params=_params("parallel"),
        name="swa_prompt",
    )(sinks, bias, q, k, k, v, v)


SWA_SAMPLE_BATCH = 8
SWA_SAMPLE_KEYS = 2 * WINDOW


def _swa_sample_kernel(sinks_ref, bias_ref, q_ref, kn_ref, vn_ref, ck_ref, cv_ref, o_in_ref,
                       o_ref, nk_ref, nv_ref, o_scr):
    del o_in_ref
    L = DEC_SEQ
    bias = bias_ref[...]
    qall = q_ref[...].astype(F32)
    pad = jnp.zeros((SWA_SAMPLE_KEYS - WINDOW - L, LANES), F32)
    sink_cols = []
    for p in range(KV_PAIRS):
        col = lambda first: jnp.concatenate(
            [jnp.full((L, 1), sinks_ref[first + g], F32) for g in range(GROUP)], axis=0)
        sink_cols.append((col(2 * p * GROUP), col((2 * p + 1) * GROUP)))
    for b in range(SWA_SAMPLE_BATCH):
        r0 = b * L
        kc, vc = ck_ref[b], cv_ref[b]
        kn, vn = kn_ref[r0:r0 + L, :], vn_ref[r0:r0 + L, :]
        nk_ref[b, 0:WINDOW - L, :] = kc[L:, :]
        nk_ref[b, WINDOW - L:WINDOW, :] = kn
        nv_ref[b, 0:WINDOW - L, :] = vc[L:, :]
        nv_ref[b, WINDOW - L:WINDOW, :] = vn
        for p in range(KV_PAIRS):
            lanes = slice(p * LANES, (p + 1) * LANES)
            kcat, vcat = _pair_operands(jnp.concatenate([kc[:, lanes], kn[:, lanes], pad], axis=0),
                                        jnp.concatenate([vc[:, lanes], vn[:, lanes], pad], axis=0))
            qs = jnp.concatenate(
                [qall[r0:r0 + L, (p * GROUP + g) * LANES:(p * GROUP + g + 1) * LANES] for g in range(GROUP)],
                axis=0).astype(BF16)
            s = lax.dot_general(qs, kcat, NT_DIMS, preferred_element_type=F32) + bias
            e, t = _pair_softmax(s, *sink_cols[p])
            r = jnp.dot(e, vcat, preferred_element_type=F32)
            o = r[:, :LANES] / (r[:, LANES:] + t)
            for g in range(GROUP):
                blk = p * GROUP + g
                o_scr[r0:r0 + L, blk * LANES:(blk + 1) * LANES] = o[g * L:(g + 1) * L]
    o_ref[...] = o_scr[...].astype(BF16)


def _swa_sample(q, k, v, cache_k, cache_v, sinks, o_buf):
    bb = SWA_SAMPLE_BATCH
    rows = bb * DEC_SEQ
    off = SEQ // rows
    tok = lambda i: (i + off, 0)
    cache = lambda i: (i, 0, 0)
    cj = np.arange(SWA_SAMPLE_KEYS)[None, :]
    bias = jnp.asarray(_window_bias(
        GROUP * DEC_SEQ, lambda r: r % DEC_SEQ,
        lambda qi: (np.where(cj < WINDOW + DEC_SEQ, PAST_LEN + qi - (PAST_LEN - WINDOW + cj), -1),
                    PAST_LEN - WINDOW + cj)))
    return pl.pallas_call(
        _swa_sample_kernel,
        grid=(DEC_BATCH // bb,),
        in_specs=[_SMEM, _resident((GROUP * DEC_SEQ, 2 * SWA_SAMPLE_KEYS)),
                  pl.BlockSpec((rows, D_MODEL), tok), pl.BlockSpec((rows, KV_DIM), tok),
                  pl.BlockSpec((rows, KV_DIM), tok),
                  pl.BlockSpec((bb, WINDOW, KV_DIM), cache), pl.BlockSpec((bb, WINDOW, KV_DIM), cache),
                  _ALIASED],
        out_specs=[pl.BlockSpec((rows, D_MODEL), tok),
                   pl.BlockSpec((bb, WINDOW, KV_DIM), cache), pl.BlockSpec((bb, WINDOW, KV_DIM), cache)],
        out_shape=[jax.ShapeDtypeStruct((N_TOK, D_MODEL), BF16),
                   jax.ShapeDtypeStruct((DEC_BATCH, WINDOW, KV_DIM), F32),
                   jax.ShapeDtypeStruct((DEC_BATCH, WINDOW, KV_DIM), F32)],
        scratch_shapes=[pltpu.VMEM((rows, D_MODEL), F32)],
        input_output_aliases={7: 0},
        compiler_params=_params("parallel"),
        name="swa_sample",
    )(sinks, bias, q, k, v, cache_k, cache_v, o_buf)


def _ret_proj_kernel(x_ref, w_ref, tt_ref, rt_ref, q_ref, k_ref, v_ref, g_ref):
    xb = x_ref[...].astype(BF16)
    hqk = jnp.dot(xb, w_ref[:, :2 * RET_QK], preferred_element_type=F32)
    cos, sin = _tile_cos_sin(tt_ref, rt_ref)
    lane = lax.broadcasted_iota(jnp.int32, (1, LANES), 1)
    rot = _rotate_blocks(hqk, 2 * RET_QK // LANES, cos, sin, 1, (lane % 2) == 0)
    nq = RET_QK // LANES
    for j in range(nq):
        q_ref[:, j * LANES:(j + 1) * LANES] = rot[j].astype(BF16)
        k_ref[:, j * LANES:(j + 1) * LANES] = rot[nq + j] * (RET_QK_DIM ** -0.5)
    v_ref[...] = jnp.dot(xb, w_ref[:, 2 * RET_QK:2 * RET_QK + RET_V],
                         preferred_element_type=F32).astype(BF16)
    g_ref[...] = _silu(jnp.dot(xb, w_ref[:, 2 * RET_QK + RET_V:], preferred_element_type=F32))


def _ret_proj(x, w, tables):
    n = x.shape[0]
    width = 2 * RET_QK + 2 * RET_V
    return pl.pallas_call(
        _ret_proj_kernel,
        grid=(n // ROW_TILE,),
        in_specs=[_rows(D_MODEL), _resident((D_MODEL, width)), *_table_specs(RET_QK_DIM)],
        out_specs=[_rows(RET_QK), _rows(RET_QK), _rows(RET_V), _rows(RET_V)],
        out_shape=[jax.ShapeDtypeStruct((n, RET_QK), BF16), jax.ShapeDtypeStruct((n, RET_QK), F32),
                   jax.ShapeDtypeStruct((n, RET_V), BF16), jax.ShapeDtypeStruct((n, RET_V), F32)],
        compiler_params=_params("parallel"),
        name="ret_proj",
    )(x, w, *tables)


def _xpos_inv_lane():
    inv = 1.0 / (RET_THETA ** jnp.linspace(0.0, 1.0, RET_QK_DIM // 2, dtype=F32))
    return jnp.repeat(inv, 2)


def _ret_log_decay():
    return jnp.log(1.0 - 2.0 ** (-5.0 - jnp.arange(RET_HEADS, dtype=F32)))


def _ret_tables(L):
    lg = _ret_log_decay()
    idx = jnp.arange(L, dtype=F32)
    diff = idx[:, None] - idx[None, :]
    causal = diff >= 0
    dmat = jnp.where(causal[None], jnp.exp(lg[:, None, None] * jnp.where(causal, diff, 0.0)[None]), 0.0)
    xi = jnp.exp(lg[:, None] * (idx[None, :] + 1.0))
    zeta = jnp.exp(lg[:, None] * (L - 1.0 - idx[None, :]))
    decay = jnp.exp(lg * L)
    rep = lambda a: jnp.broadcast_to(a[:, :, None], (RET_HEADS, L, LANES))
    return dmat, rep(xi), rep(zeta), decay


def _lanes(t, width):
    return jnp.concatenate([t] * (width // LANES), axis=1)


def _group_norm_gate(o, gate):
    mu = jnp.mean(o, axis=-1, keepdims=True)
    d = o - mu
    var = jnp.mean(d * d, axis=-1, keepdims=True)
    return gate * (d * lax.rsqrt(var + LN_EPS))


def _retention_head(qh, kh, vh, s_old, dmat, xi, zeta, decay):
    att = lax.dot_general(qh, kh.astype(BF16), NT_DIMS, preferred_element_type=F32) * dmat
    o = jnp.dot(att.astype(BF16), vh, preferred_element_type=F32)
    o = o + jnp.dot(qh, s_old.astype(BF16), preferred_element_type=F32) * _lanes(xi, RET_V_DIM)
    kz = (kh * _lanes(zeta, RET_QK_DIM)).astype(BF16)
    upd = lax.dot_general(kz, vh, TN_DIMS, preferred_element_type=F32)
    return o, decay * s_old + upd


def _ret_prompt_kernel(q_ref, k_ref, v_ref, g_ref, dmat_ref, xi_ref, zeta_ref, decay_ref,
                       y_ref, sfin_ref, s_ref):
    i = pl.program_id(0)

    @pl.when(i == 0)
    def _():
        s_ref[...] = jnp.zeros_like(s_ref)

    for h in range(RET_HEADS):
        qk = slice(h * RET_QK_DIM, (h + 1) * RET_QK_DIM)
        vg = slice(h * RET_V_DIM, (h + 1) * RET_V_DIM)
        o, s_new = _retention_head(q_ref[:, qk], k_ref[:, qk], v_ref[:, vg], s_ref[h],
                                   dmat_ref[h], xi_ref[h], zeta_ref[h], decay_ref[h])
        s_ref[h] = s_new
        y_ref[:, vg] = _group_norm_gate(o, g_ref[:, vg]).astype(BF16)

    @pl.when(i == pl.num_programs(0) - 1)
    def _():
        sfin_ref[...] = s_ref[...]


def _ret_prompt(q, k, v, g, tables):
    nc = SEQ // RET_CHUNK
    cur = lambda i: (i, 0)
    state_shape = (RET_HEADS, RET_QK_DIM, RET_V_DIM)
    return pl.pallas_call(
        _ret_prompt_kernel,
        grid=(nc,),
        in_specs=[pl.BlockSpec((RET_CHUNK, RET_QK), cur), pl.BlockSpec((RET_CHUNK, RET_QK), cur),
                  pl.BlockSpec((RET_CHUNK, RET_V), cur), pl.BlockSpec((RET_CHUNK, RET_V), cur),
                  _resident((RET_HEADS, RET_CHUNK, RET_CHUNK)), _resident((RET_HEADS, RET_CHUNK, LANES)),
                  _resident((RET_HEADS, RET_CHUNK, LANES)), _SMEM],
        out_specs=[pl.BlockSpec((RET_CHUNK, RET_V), cur), pl.BlockSpec(state_shape, lambda i: (0, 0, 0))],
        out_shape=[jax.ShapeDtypeStruct((N_TOK, RET_V), BF16), jax.ShapeDtypeStruct(state_shape, F32)],
        scratch_shapes=[pltpu.VMEM(state_shape, F32)],
        compiler_params=_params("arbitrary"),
        name="ret_prompt",
    )(q, k, v, g, *tables)


RET_SAMPLE_BATCH = 2


def _ret_sample_kernel(q_ref, k_ref, v_ref, g_ref, s_in_ref, dmat_ref, xi_ref, zeta_ref, decay_ref,
                       y_in_ref, y_ref, s_out_ref, y_scr):
    del y_in_ref
    L = DEC_SEQ
    qa = q_ref[...].astype(F32)
    va = v_ref[...].astype(F32)
    for b in range(RET_SAMPLE_BATCH):
        rows = slice(b * L, (b + 1) * L)
        for h in range(RET_HEADS):
            qk = slice(h * RET_QK_DIM, (h + 1) * RET_QK_DIM)
            vg = slice(h * RET_V_DIM, (h + 1) * RET_V_DIM)
            o, s_new = _retention_head(qa[rows, qk].astype(BF16), k_ref[rows, qk], va[rows, vg].astype(BF16),
                                       s_in_ref[b, h], dmat_ref[h], xi_ref[h], zeta_ref[h], decay_ref[h])
            s_out_ref[b, h] = s_new
            y_scr[rows, vg] = _group_norm_gate(o, g_ref[rows, vg])
    y_ref[...] = y_scr[...].astype(BF16)


def _ret_sample(q, k, v, g, state, tables, y_buf):
    bb = RET_SAMPLE_BATCH
    rows = bb * DEC_SEQ
    off = SEQ // rows
    tok = lambda i: (i + off, 0)
    st = lambda i: (i, 0, 0, 0)
    state_block = (bb, RET_HEADS, RET_QK_DIM, RET_V_DIM)
    return pl.pallas_call(
        _ret_sample_kernel,
        grid=(DEC_BATCH // bb,),
        in_specs=[pl.BlockSpec((rows, RET_QK), tok), pl.BlockSpec((rows, RET_QK), tok),
                  pl.BlockSpec((rows, RET_V), tok), pl.BlockSpec((rows, RET_V), tok),
                  pl.BlockSpec(state_block, st),
                  _resident((RET_HEADS, DEC_SEQ, DEC_SEQ)), _resident((RET_HEADS, DEC_SEQ, LANES)),
                  _resident((RET_HEADS, DEC_SEQ, LANES)), _SMEM, _ALIASED],
        out_specs=[pl.BlockSpec((rows, RET_V), tok), pl.BlockSpec(state_block, st)],
        out_shape=[jax.ShapeDtypeStruct((N_TOK, RET_V), BF16),
                   jax.ShapeDtypeStruct((DEC_BATCH,) + state_block[1:], F32)],
        scratch_shapes=[pltpu.VMEM((rows, RET_V), F32)],
        input_output_aliases={9: 0},
        compiler_params=_params("parallel"),
        name="ret_sample",
    )(q, k, v, g, state, *tables, y_buf)


def kernel(x_prompt, x_sample, cache_k_win, cache_v_win, state_ret, ffn1_w_gate, ffn1_w_up, ffn1_w_down,
           ffn2_w_gate, ffn2_w_up, ffn2_w_down, ln_g, ln_b, attn_w_qkv, attn_w_o, attn_sinks, ret_w_in,
           ret_w_o):
    ln = lambda i, j: (ln_g[i, j][None, :], ln_b[i, j][None, :])
    ffn1 = (ffn1_w_gate, ffn1_w_up, ffn1_w_down)
    ffn2 = (ffn2_w_gate, ffn2_w_up, ffn2_w_down)

    x = _half_ffn((x_prompt.reshape(SEQ, D_MODEL), x_sample.reshape(N_SAMPLE, D_MODEL)), ffn1, 0, ln(0, 0))
    w_qkv = jnp.concatenate([_pair_heads(attn_w_qkv[0][:, :D_MODEL], 1), attn_w_qkv[0][:, D_MODEL:]], axis=1)
    w_o = _pair_heads(attn_w_o, 1)
    q, k, v = _qkv_proj(x, w_qkv, _angle_tables(_rope_inv_lane()))
    sinks = attn_sinks[0]
    o = _swa_prompt(q, k, v, sinks)
    o, nk_s, nv_s = _swa_sample(q, k, v, cache_k_win[0].reshape(DEC_BATCH, WINDOW, KV_DIM),
                                cache_v_win[0].reshape(DEC_BATCH, WINDOW, KV_DIM), sinks, o)
    x = _mix_out(o, x, w_o, ln(0, 1))
    x = _half_ffn((x,), ffn2, 0, ln(0, 2))

    kv_shape = (1, 1, WINDOW, N_KV_HEADS, HEAD_DIM)
    new_k_p = k[SEQ - WINDOW:SEQ].reshape(kv_shape)
    new_v_p = v[SEQ - WINDOW:SEQ].reshape(kv_shape)
    new_k_s = nk_s.reshape(1, DEC_BATCH, WINDOW, N_KV_HEADS, HEAD_DIM)
    new_v_s = nv_s.reshape(1, DEC_BATCH, WINDOW, N_KV_HEADS, HEAD_DIM)

    x = _half_ffn((x,), ffn1, 1, ln(1, 0))
    rq, rk, rv, rg = _ret_proj(x, ret_w_in[0].astype(BF16), _angle_tables(_xpos_inv_lane()))
    y, s_p = _ret_prompt(rq, rk, rv, rg, _ret_tables(RET_CHUNK))
    y, s_s = _ret_sample(rq, rk, rv, rg, state_ret[0], _ret_tables(DEC_SEQ), y)
    x = _mix_out(y, x, ret_w_o, ln(1, 1))
    y_p, y_s = _half_ffn((x,), ffn2, 1, ln(1, 2), split_out=True)

    return (y_p.reshape(1, SEQ, D_MODEL), y_s.reshape(DEC_BATCH, DEC_SEQ, D_MODEL),
            new_k_p, new_v_p, new_k_s, new_v_s, s_p[None, None], s_s[None])
```

```python
import functools
import math

import jax
import jax.numpy as jnp
import numpy as np
from jax import lax
from jax.experimental import pallas as pl
from jax.experimental.pallas import tpu as pltpu

F32 = jnp.float32
BF16 = jnp.bfloat16

D_MODEL = 1024
SEQ = 16384
DEPTH = 2
DEC_BATCH = 128
DEC_SEQ = 8
PAST_LEN = 16384
N_HEADS = 16
N_KV_HEADS = 4
GROUP = N_HEADS // N_KV_HEADS
HEAD_DIM = D_MODEL // N_HEADS
WINDOW = 128
ROPE_THETA = 10000.0
ATTN_SCALE = 1.0 / math.sqrt(HEAD_DIM)
RET_HEADS = 4
RET_QK_DIM = D_MODEL // RET_HEADS
RET_V_DIM = 2 * D_MODEL // RET_HEADS
RET_CHUNK = 256
RET_THETA = 10000.0
FFN_DIM = 2816
LN_EPS = 1e-5
ALPHA = (2.0 * DEPTH) ** 0.25

N_SAMPLE = DEC_BATCH * DEC_SEQ
N_TOK = SEQ + N_SAMPLE
KV_DIM = N_KV_HEADS * HEAD_DIM
RET_QK = RET_HEADS * RET_QK_DIM
RET_V = RET_HEADS * RET_V_DIM

LANES = 128
ROW_TILE = 512
FFN_TILE = 512
FFN_CHUNK = 256
VMEM_LIMIT = 56 * 1024 * 1024

N_TILES = N_TOK // ROW_TILE
N_PROMPT_TILES = SEQ // ROW_TILE
FFN_TILES = N_TOK // FFN_TILE
FFN_PROMPT_TILES = SEQ // FFN_TILE
KV_PAIRS = N_KV_HEADS // 2

NT_DIMS = (((1,), (1,)), ((), ()))
TN_DIMS = (((0,), (0,)), ((), ()))


def _params(*sem):
    return pltpu.CompilerParams(dimension_semantics=sem, vmem_limit_bytes=VMEM_LIMIT)


def _resident(shape):
    return pl.BlockSpec(shape, lambda *_: (0,) * len(shape), pipeline_mode=pl.Buffered(1))


def _rows(width, offset_blocks=0, tile=ROW_TILE):
    return pl.BlockSpec((tile, width), lambda i: (i + offset_blocks, 0))


_ALIASED = pl.BlockSpec(memory_space=pl.ANY)
_SMEM = pl.BlockSpec(memory_space=pltpu.SMEM)


def _layer_norm(y, g, b):
    mu = jnp.mean(y, axis=-1, keepdims=True)
    d = y - mu
    var = jnp.mean(d * d, axis=-1, keepdims=True)
    return d * lax.rsqrt(var + LN_EPS) * g + b


def _silu(x):
    return x / (1.0 + jnp.exp(-x))


def _ffn_kernel(split_in, split_out, *refs):
    n_in = 2 if split_in else 1
    x_refs, (wg_ref, wu_ref, wd_ref, g_ref, b_ref), o_refs = refs[:n_in], refs[n_in:n_in + 5], refs[n_in + 5:]
    i = pl.program_id(0)
    x = x_refs[0][...]
    if split_in:
        x = jnp.where(i < FFN_PROMPT_TILES, x, x_refs[1][...])
    xb = x.astype(BF16)
    acc = None
    for c in range(FFN_DIM // FFN_CHUNK):
        sl = slice(c * FFN_CHUNK, (c + 1) * FFN_CHUNK)
        gate = jnp.dot(xb, wg_ref[:, sl].astype(BF16), preferred_element_type=F32)
        up = jnp.dot(xb, wu_ref[:, sl].astype(BF16), preferred_element_type=F32)
        h = (_silu(gate) * up).astype(BF16)
        d = jnp.dot(h, wd_ref[sl, :].astype(BF16), preferred_element_type=F32)
        acc = d if acc is None else acc + d
    y = _layer_norm(ALPHA * x + 0.5 * acc, g_ref[...], b_ref[...])
    if split_out:
        @pl.when(i < FFN_PROMPT_TILES)
        def _():
            o_refs[0][...] = y

        @pl.when(i >= FFN_PROMPT_TILES)
        def _():
            o_refs[1][...] = y
    else:
        o_refs[0][...] = y


_PROMPT_TILE = lambda i: (jnp.minimum(i, FFN_PROMPT_TILES - 1), 0)
_SAMPLE_TILE = lambda i: (jnp.maximum(i - FFN_PROMPT_TILES, 0), 0)


def _layer_weight(shape, layer):
    return pl.BlockSpec((None,) + shape, lambda i: (layer, 0, 0), pipeline_mode=pl.Buffered(1))


def _half_ffn(xs, weights, layer, ln, *, split_out=False):
    split_in = len(xs) == 2
    tile = (FFN_TILE, D_MODEL)
    x_specs = ([pl.BlockSpec(tile, _PROMPT_TILE), pl.BlockSpec(tile, _SAMPLE_TILE)]
               if split_in else [_rows(D_MODEL, tile=FFN_TILE)])
    if split_out:
        out_specs = [pl.BlockSpec(tile, _PROMPT_TILE), pl.BlockSpec(tile, _SAMPLE_TILE)]
        out_shape = [jax.ShapeDtypeStruct((SEQ, D_MODEL), F32), jax.ShapeDtypeStruct((N_SAMPLE, D_MODEL), F32)]
    else:
        out_specs = _rows(D_MODEL, tile=FFN_TILE)
        out_shape = jax.ShapeDtypeStruct((N_TOK, D_MODEL), F32)
    return pl.pallas_call(
        functools.partial(_ffn_kernel, split_in, split_out),
        grid=(FFN_TILES,),
        in_specs=x_specs + [_layer_weight((D_MODEL, FFN_DIM), layer), _layer_weight((D_MODEL, FFN_DIM), layer),
                            _layer_weight((FFN_DIM, D_MODEL), layer),
                            _resident((1, D_MODEL)), _resident((1, D_MODEL))],
        out_specs=out_specs,
        out_shape=out_shape,
        compiler_params=_params("arbitrary" if split_out else "parallel"),
        name="half_ffn",
    )(*xs, *weights, *ln)


def _mix_out_kernel(yp_ref, ys_ref, x_ref, w_ref, g_ref, b_ref, o_ref):
    y = jnp.where(pl.program_id(0) < N_PROMPT_TILES, yp_ref[...], ys_ref[...].astype(BF16))
    m = jnp.dot(y, w_ref[...].astype(BF16), preferred_element_type=F32)
    o_ref[...] = _layer_norm(ALPHA * x_ref[...] + m, g_ref[...], b_ref[...])


def _mix_out(y_prompt, y_sample, x, w, ln):
    k = y_prompt.shape[1]
    return pl.pallas_call(
        _mix_out_kernel,
        grid=(N_TILES,),
        in_specs=[pl.BlockSpec((ROW_TILE, k), lambda i: (jnp.minimum(i, N_PROMPT_TILES - 1), 0)),
                  pl.BlockSpec((ROW_TILE, k), lambda i: (jnp.maximum(i - N_PROMPT_TILES, 0), 0)),
                  _rows(D_MODEL), _layer_weight((k, D_MODEL), 0),
                  _resident((1, D_MODEL)), _resident((1, D_MODEL))],
        out_specs=_rows(D_MODEL),
        out_shape=jax.ShapeDtypeStruct((N_TOK, D_MODEL), F32),
        compiler_params=_params("parallel"),
        name="mix_out",
    )(y_prompt, y_sample, x, w, *ln)


def _angle_tables(inv_lane):
    base = jnp.concatenate([jnp.arange(N_PROMPT_TILES) * ROW_TILE,
                            jnp.full((N_TILES - N_PROMPT_TILES,), PAST_LEN)]).astype(F32)
    row = jnp.stack([jnp.arange(ROW_TILE), jnp.arange(ROW_TILE) % DEC_SEQ]).astype(F32)
    ta = base[:, None] * inv_lane[None, :]
    ra = row[:, :, None] * inv_lane[None, None, :]
    return jnp.stack([jnp.cos(ta), jnp.sin(ta)], axis=1), jnp.stack([jnp.cos(ra), jnp.sin(ra)], axis=1)


def _table_specs(width):
    return [pl.BlockSpec((1, 2, width), lambda i: (i, 0, 0)),
            pl.BlockSpec((1, 2, ROW_TILE, width), lambda i: (i // N_PROMPT_TILES, 0, 0, 0))]


def _tile_cos_sin(tt_ref, rt_ref):
    cb, sb = tt_ref[0, 0:1, :], tt_ref[0, 1:2, :]
    cr, sr = rt_ref[0, 0], rt_ref[0, 1]
    return cb * cr - sb * sr, sb * cr + cb * sr


def _rotate_blocks(h, n_blocks, cos, sin, shift, partner_above):
    width = cos.shape[1]
    sign = jnp.where(partner_above, -1.0, 1.0).astype(F32)
    sin = sin * jnp.concatenate([sign] * (width // LANES), axis=1)
    out = []
    for j in range(n_blocks):
        t = (j * LANES) % width
        blk = h[:, j * LANES:(j + 1) * LANES]
        partner = jnp.where(partner_above, pltpu.roll(blk, LANES - shift, 1), pltpu.roll(blk, shift, 1))
        out.append(blk * cos[:, t:t + LANES] + partner * sin[:, t:t + LANES])
    return out


def _qkv_kernel(x_ref, w_ref, tt_ref, rt_ref, q_ref, k_ref, v_ref):
    xb = x_ref[...].astype(BF16)
    h = jnp.dot(xb, w_ref[...].astype(BF16), preferred_element_type=F32)
    cos, sin = _tile_cos_sin(tt_ref, rt_ref)
    lane = lax.broadcasted_iota(jnp.int32, (1, LANES), 1)
    n_rot = (D_MODEL + KV_DIM) // LANES
    rot = _rotate_blocks(h, n_rot, cos, sin, HEAD_DIM // 2, (lane % HEAD_DIM) < HEAD_DIM // 2)
    nq = D_MODEL // LANES
    for j in range(nq):
        q_ref[:, j * LANES:(j + 1) * LANES] = (rot[j] * ATTN_SCALE).astype(BF16)
    for j in range(nq, n_rot):
        k_ref[:, (j - nq) * LANES:(j - nq + 1) * LANES] = rot[j]
    v_ref[...] = h[:, D_MODEL + KV_DIM:]


def _qkv_proj(x, w, tables):
    n = x.shape[0]
    width = D_MODEL + 2 * KV_DIM
    return pl.pallas_call(
        _qkv_kernel,
        grid=(n // ROW_TILE,),
        in_specs=[_rows(D_MODEL), _resident((D_MODEL, width)), *_table_specs(LANES)],
        out_specs=[_rows(D_MODEL), _rows(KV_DIM), _rows(KV_DIM)],
        out_shape=[jax.ShapeDtypeStruct((n, D_MODEL), BF16), jax.ShapeDtypeStruct((n, KV_DIM), F32),
                   jax.ShapeDtypeStruct((n, KV_DIM), F32)],
        compiler_params=_params("parallel"),
        name="attn_qkv",
    )(x, w, *tables)


def _rope_inv_lane():
    inv = ROPE_THETA ** (-jnp.arange(0, HEAD_DIM, 2, dtype=F32) / HEAD_DIM)
    return jnp.tile(inv, LANES // (HEAD_DIM // 2))


def _pair_heads(w, axis):
    shape = w.shape
    w = w.reshape(shape[:axis] + (KV_PAIRS, 2, GROUP, HEAD_DIM) + shape[axis + 1:])
    return jnp.swapaxes(w, axis + 1, axis + 2).reshape(shape)


def _pair_operands(kall, vall):
    s = kall.shape[0]
    low = lax.broadcasted_iota(jnp.int32, (s, LANES), 1) < HEAD_DIM
    kb, vb = kall.astype(BF16), vall.astype(BF16)
    zero = jnp.zeros_like(kb)
    one_lo = jnp.where(low, 1.0, 0.0).astype(BF16)
    one_hi = jnp.where(low, 0.0, 1.0).astype(BF16)
    kcat = jnp.concatenate([jnp.where(low, kb, zero), jnp.where(low, zero, kb)], axis=0)
    vcat = jnp.concatenate([jnp.concatenate([jnp.where(low, vb, zero), one_lo], axis=1),
                            jnp.concatenate([jnp.where(low, zero, vb), one_hi], axis=1)], axis=0)
    return kcat, vcat


def _pair_softmax(s, sink_a, sink_b):
    half = s.shape[1] // 2
    sa, sb = s[:, :half], s[:, half:]
    ma = jnp.maximum(jnp.max(sa, axis=-1, keepdims=True), sink_a)
    mb = jnp.maximum(jnp.max(sb, axis=-1, keepdims=True), sink_b)
    e = jnp.concatenate([jnp.exp(sa - ma), jnp.exp(sb - mb)], axis=1).astype(BF16)
    low = lax.broadcasted_iota(jnp.int32, (s.shape[0], LANES), 1) < HEAD_DIM
    return e, jnp.where(low, jnp.exp(sink_a - ma), jnp.exp(sink_b - mb))


SWA_SAMPLE_KEYS = 2 * WINDOW


def _swa_kernel(sinks_ref, bias_ref, sbias_ref, q_ref, kp_ref, kc_ref, vp_ref, vc_ref,
                qs_ref, kn_ref, vn_ref, ck_ref, cv_ref,
                o_ref, os_ref, nk_ref, nv_ref, s_ref, e_ref, t_ref):
    _swa_sample_step(sinks_ref, sbias_ref, qs_ref, kn_ref, vn_ref, ck_ref, cv_ref, os_ref, nk_ref, nv_ref)
    bias = bias_ref[0]
    for p in range(KV_PAIRS):
        lanes = slice(p * LANES, (p + 1) * LANES)
        kcat, vcat = _pair_operands(jnp.concatenate([kp_ref[:, lanes], kc_ref[:, lanes]], axis=0),
                                    jnp.concatenate([vp_ref[:, lanes], vc_ref[:, lanes]], axis=0))
        q = q_ref[:, p * GROUP * LANES:(p + 1) * GROUP * LANES]
        qs = jnp.concatenate([q[:, g * LANES:(g + 1) * LANES] for g in range(GROUP)], axis=0)
        s_ref[...] = lax.dot_general(qs, kcat, NT_DIMS, preferred_element_type=F32)
        for g in range(GROUP):
            rows = slice(g * WINDOW, (g + 1) * WINDOW)
            head_a = (2 * p) * GROUP + g
            e, t = _pair_softmax(s_ref[rows, :] + bias, sinks_ref[head_a], sinks_ref[head_a + GROUP])
            e_ref[rows, :] = e
            t_ref[rows, :] = t
        r = jnp.dot(e_ref[...], vcat, preferred_element_type=F32)
        o = r[:, :LANES] / (r[:, LANES:] + t_ref[...])
        for g in range(GROUP):
            blk = p * GROUP + g
            o_ref[:, blk * LANES:(blk + 1) * LANES] = o[g * WINDOW:(g + 1) * WINDOW].astype(BF16)


def _window_bias(n_rows, row_of, key_dist_and_pos):
    dist, kpos = key_dist_and_pos(row_of(np.arange(n_rows))[:, None])
    valid = (dist >= 0) & (dist < WINDOW) & (kpos >= 0)
    m = np.where(valid, 0.0, -np.inf).astype(np.float32)
    return np.concatenate([m, m], axis=1)


def _swa_sample_step(sinks_ref, bias_ref, q_ref, kn_ref, vn_ref, ck_ref, cv_ref, o_ref, nk_ref, nv_ref):
    L = DEC_SEQ
    qall = q_ref[...].astype(F32)
    qb = jnp.where(pl.program_id(0) % 2 == 1, qall[L:], qall[:L])
    bias = bias_ref[...]
    pad = jnp.zeros((SWA_SAMPLE_KEYS - WINDOW - L, LANES), F32)
    kc, vc = ck_ref[0], cv_ref[0]
    kn, vn = kn_ref[...], vn_ref[...]
    nk_ref[0, 0:WINDOW - L, :] = kc[L:, :]
    nk_ref[0, WINDOW - L:WINDOW, :] = kn
    nv_ref[0, 0:WINDOW - L, :] = vc[L:, :]
    nv_ref[0, WINDOW - L:WINDOW, :] = vn
    for p in range(KV_PAIRS):
        lanes = slice(p * LANES, (p + 1) * LANES)
        kcat, vcat = _pair_operands(jnp.concatenate([kc[:, lanes], kn[:, lanes], pad], axis=0),
                                    jnp.concatenate([vc[:, lanes], vn[:, lanes], pad], axis=0))
        qs = jnp.concatenate(
            [qb[:, (p * GROUP + g) * LANES:(p * GROUP + g + 1) * LANES] for g in range(GROUP)],
            axis=0).astype(BF16)
        sink_col = lambda first: jnp.concatenate(
            [jnp.full((L, 1), sinks_ref[first + g], F32) for g in range(GROUP)], axis=0)
        s = lax.dot_general(qs, kcat, NT_DIMS, preferred_element_type=F32) + bias
        e, t = _pair_softmax(s, sink_col(2 * p * GROUP), sink_col((2 * p + 1) * GROUP))
        r = jnp.dot(e, vcat, preferred_element_type=F32)
        o = r[:, :LANES] / (r[:, LANES:] + t)
        for g in range(GROUP):
            blk = p * GROUP + g
            o_ref[:, blk * LANES:(blk + 1) * LANES] = o[g * L:(g + 1) * L]


def _swa(q, k, v, cache_k, cache_v, sinks):
    assert SEQ // WINDOW == DEC_BATCH
    rows = GROUP * WINDOW
    si = np.arange(2 * WINDOW)[None, :]
    bias = jnp.asarray(np.stack([
        _window_bias(WINDOW, lambda r: r, lambda qi: (qi + WINDOW - si, blk * WINDOW - WINDOW + si))
        for blk in (0, 1)]))
    cj = np.arange(SWA_SAMPLE_KEYS)[None, :]
    sbias = jnp.asarray(_window_bias(
        GROUP * DEC_SEQ, lambda r: r % DEC_SEQ,
        lambda qi: (np.where(cj < WINDOW + DEC_SEQ, PAST_LEN + qi - (PAST_LEN - WINDOW + cj), -1),
                    PAST_LEN - WINDOW + cj)))
    cur = lambda i: (i, 0)
    prev = lambda i: (jnp.maximum(i - 1, 0), 0)
    seq = lambda i: (i + SEQ // DEC_SEQ, 0)
    cache = lambda i: (i, 0, 0)
    return pl.pallas_call(
        _swa_kernel,
        grid=(DEC_BATCH,),
        in_specs=[_SMEM, pl.BlockSpec((1, WINDOW, 4 * WINDOW), lambda i: (jnp.minimum(i, 1), 0, 0)),
                  _resident((GROUP * DEC_SEQ, 2 * SWA_SAMPLE_KEYS)),
                  pl.BlockSpec((WINDOW, D_MODEL), cur),
                  pl.BlockSpec((WINDOW, KV_DIM), prev), pl.BlockSpec((WINDOW, KV_DIM), cur),
                  pl.BlockSpec((WINDOW, KV_DIM), prev), pl.BlockSpec((WINDOW, KV_DIM), cur),
                  pl.BlockSpec((2 * DEC_SEQ, D_MODEL), lambda i: (i // 2 + SEQ // (2 * DEC_SEQ), 0)),
                  pl.BlockSpec((DEC_SEQ, KV_DIM), seq), pl.BlockSpec((DEC_SEQ, KV_DIM), seq),
                  pl.BlockSpec((1, WINDOW, KV_DIM), cache), pl.BlockSpec((1, WINDOW, KV_DIM), cache)],
        out_specs=[pl.BlockSpec((WINDOW, D_MODEL), cur), pl.BlockSpec((DEC_SEQ, D_MODEL), cur),
                   pl.BlockSpec((1, WINDOW, KV_DIM), cache), pl.BlockSpec((1, WINDOW, KV_DIM), cache)],
        out_shape=[jax.ShapeDtypeStruct((SEQ, D_MODEL), BF16), jax.ShapeDtypeStruct((N_SAMPLE, D_MODEL), F32),
                   jax.ShapeDtypeStruct((DEC_BATCH, WINDOW, KV_DIM), F32),
                   jax.ShapeDtypeStruct((DEC_BATCH, WINDOW, KV_DIM), F32)],
        scratch_shapes=[pltpu.VMEM((rows, 4 * WINDOW), F32), pltpu.VMEM((rows, 4 * WINDOW), BF16),
                        pltpu.VMEM((rows, LANES), F32)],
        compiler_params=_params("parallel"),
        name="swa",
    )(sinks, bias, sbias, q, k, k, v, v, q, k, v, cache_k, cache_v)


def _ret_proj_kernel(x_ref, w_ref, tt_ref, rt_ref, q_ref, k_ref, v_ref, g_ref):
    xb = x_ref[...].astype(BF16)
    hqk = jnp.dot(xb, w_ref[:, :2 * RET_QK], preferred_element_type=F32)
    cos, sin = _tile_cos_sin(tt_ref, rt_ref)
    lane = lax.broadcasted_iota(jnp.int32, (1, LANES), 1)
    rot = _rotate_blocks(hqk, 2 * RET_QK // LANES, cos, sin, 1, (lane % 2) == 0)
    nq = RET_QK // LANES
    for j in range(nq):
        q_ref[:, j * LANES:(j + 1) * LANES] = rot[j].astype(BF16)
        k_ref[:, j * LANES:(j + 1) * LANES] = rot[nq + j] * (RET_QK_DIM ** -0.5)
    v_ref[...] = jnp.dot(xb, w_ref[:, 2 * RET_QK:2 * RET_QK + RET_V],
                         preferred_element_type=F32).astype(BF16)
    g_ref[...] = _silu(jnp.dot(xb, w_ref[:, 2 * RET_QK + RET_V:], preferred_element_type=F32))


def _ret_proj(x, w, tables):
    n = x.shape[0]
    width = 2 * RET_QK + 2 * RET_V
    return pl.pallas_call(
        _ret_proj_kernel,
        grid=(n // ROW_TILE,),
        in_specs=[_rows(D_MODEL), _resident((D_MODEL, width)), *_table_specs(RET_QK_DIM)],
        out_specs=[_rows(RET_QK), _rows(RET_QK), _rows(RET_V), _rows(RET_V)],
        out_shape=[jax.ShapeDtypeStruct((n, RET_QK), BF16), jax.ShapeDtypeStruct((n, RET_QK), F32),
                   jax.ShapeDtypeStruct((n, RET_V), BF16), jax.ShapeDtypeStruct((n, RET_V), F32)],
        compiler_params=_params("parallel"),
        name="ret_proj",
    )(x, w, *tables)


def _xpos_inv_lane():
    inv = 1.0 / (RET_THETA ** jnp.linspace(0.0, 1.0, RET_QK_DIM // 2, dtype=F32))
    return jnp.repeat(inv, 2)


def _ret_log_decay():
    return jnp.log(1.0 - 2.0 ** (-5.0 - jnp.arange(RET_HEADS, dtype=F32)))


def _ret_tables(L):
    lg = _ret_log_decay()
    idx = jnp.arange(L, dtype=F32)
    diff = idx[:, None] - idx[None, :]
    causal = diff >= 0
    dmat = jnp.where(causal[None], jnp.exp(lg[:, None, None] * jnp.where(causal, diff, 0.0)[None]), 0.0)
    xi = jnp.exp(lg[:, None] * (idx[None, :] + 1.0))
    zeta = jnp.exp(lg[:, None] * (L - 1.0 - idx[None, :]))
    decay = jnp.exp(lg * L)
    rep = lambda a: jnp.broadcast_to(a[:, :, None], (RET_HEADS, L, LANES))
    return dmat, rep(xi), rep(zeta), decay


def _lanes(t, width):
    return jnp.concatenate([t] * (width // LANES), axis=1)


def _group_norm_gate(o, gate):
    mu = jnp.mean(o, axis=-1, keepdims=True)
    d = o - mu
    var = jnp.mean(d * d, axis=-1, keepdims=True)
    return gate * (d * lax.rsqrt(var + LN_EPS))


def _retention_head(qh, kh, vh, s_old, dmat, xi, zeta, decay):
    att = lax.dot_general(qh, kh.astype(BF16), NT_DIMS, preferred_element_type=F32) * dmat
    o = jnp.dot(att.astype(BF16), vh, preferred_element_type=F32)
    o = o + jnp.dot(qh, s_old.astype(BF16), preferred_element_type=F32) * _lanes(xi, RET_V_DIM)
    kz = (kh * _lanes(zeta, RET_QK_DIM)).astype(BF16)
    upd = lax.dot_general(kz, vh, TN_DIMS, preferred_element_type=F32)
    return o, decay * s_old + upd


RET_SAMPLE_BATCH = DEC_BATCH // (SEQ // RET_CHUNK)


def _ret_kernel(q_ref, k_ref, v_ref, g_ref, dmat_ref, xi_ref, zeta_ref, decay_ref,
                qs_ref, ks_ref, vs_ref, gs_ref, s_in_ref, sdmat_ref, sxi_ref, szeta_ref, sdecay_ref,
                y_ref, sfin_ref, ys_ref, s_out_ref, s_ref, ys_scr):
    i = pl.program_id(0)

    @pl.when(i == 0)
    def _():
        s_ref[...] = jnp.zeros_like(s_ref)

    L = DEC_SEQ
    qa = qs_ref[...].astype(F32)
    va = vs_ref[...].astype(F32)
    for h in range(RET_HEADS):
        qk = slice(h * RET_QK_DIM, (h + 1) * RET_QK_DIM)
        vg = slice(h * RET_V_DIM, (h + 1) * RET_V_DIM)
        o, s_new = _retention_head(q_ref[:, qk], k_ref[:, qk], v_ref[:, vg], s_ref[h],
                                   dmat_ref[h], xi_ref[h], zeta_ref[h], decay_ref[h])
        s_ref[h] = s_new
        y_ref[:, vg] = _group_norm_gate(o, g_ref[:, vg]).astype(BF16)
        for b in range(RET_SAMPLE_BATCH):
            rows = slice(b * L, (b + 1) * L)
            o, s_new = _retention_head(qa[rows, qk].astype(BF16), ks_ref[rows, qk], va[rows, vg].astype(BF16),
                                       s_in_ref[b, h], sdmat_ref[h], sxi_ref[h], szeta_ref[h], sdecay_ref[h])
            s_out_ref[b, h] = s_new
            ys_scr[rows, vg] = _group_norm_gate(o, gs_ref[rows, vg])
    ys_ref[...] = ys_scr[...].astype(BF16)

    @pl.when(i == pl.num_programs(0) - 1)
    def _():
        sfin_ref[...] = s_ref[...]


def _ret(q, k, v, g, state, tables, sample_tables):
    nc = SEQ // RET_CHUNK
    bb = RET_SAMPLE_BATCH
    rows = bb * DEC_SEQ
    cur = lambda i: (i, 0)
    tok = lambda i: (i + SEQ // rows, 0)
    st = lambda i: (i, 0, 0, 0)
    state_shape = (RET_HEADS, RET_QK_DIM, RET_V_DIM)
    state_block = (bb,) + state_shape
    table_specs = lambda n: [_resident((RET_HEADS, n, n)), _resident((RET_HEADS, n, LANES)),
                             _resident((RET_HEADS, n, LANES)), _SMEM]
    return pl.pallas_call(
        _ret_kernel,
        grid=(nc,),
        in_specs=[pl.BlockSpec((RET_CHUNK, RET_QK), cur), pl.BlockSpec((RET_CHUNK, RET_QK), cur),
                  pl.BlockSpec((RET_CHUNK, RET_V), cur), pl.BlockSpec((RET_CHUNK, RET_V), cur),
                  *table_specs(RET_CHUNK),
                  pl.BlockSpec((rows, RET_QK), tok), pl.BlockSpec((rows, RET_QK), tok),
                  pl.BlockSpec((rows, RET_V), tok), pl.BlockSpec((rows, RET_V), tok),
                  pl.BlockSpec(state_block, st), *table_specs(DEC_SEQ)],
        out_specs=[pl.BlockSpec((RET_CHUNK, RET_V), cur), pl.BlockSpec(state_shape, lambda i: (0, 0, 0)),
                   pl.BlockSpec((rows, RET_V), cur), pl.BlockSpec(state_block, st)],
        out_shape=[jax.ShapeDtypeStruct((SEQ, RET_V), BF16), jax.ShapeDtypeStruct(state_shape, F32),
                   jax.ShapeDtypeStruct((N_SAMPLE, RET_V), BF16),
                   jax.ShapeDtypeStruct((DEC_BATCH,) + state_shape, F32)],
        scratch_shapes=[pltpu.VMEM(state_shape, F32), pltpu.VMEM((rows, RET_V), F32)],
        compiler_params=_params("arbitrary"),
        name="ret",
    )(q, k, v, g, *tables, q, k, v, g, state, *sample_tables)


def kernel(x_prompt, x_sample, cache_k_win, cache_v_win, state_ret, ffn1_w_gate, ffn1_w_up, ffn1_w_down,
           ffn2_w_gate, ffn2_w_up, ffn2_w_down, ln_g, ln_b, attn_w_qkv, attn_w_o, attn_sinks, ret_w_in,
           ret_w_o):
    ln = lambda i, j: (ln_g[i, j][None, :], ln_b[i, j][None, :])
    ffn1 = (ffn1_w_gate, ffn1_w_up, ffn1_w_down)
    ffn2 = (ffn2_w_gate, ffn2_w_up, ffn2_w_down)

    x = _half_ffn((x_prompt.reshape(SEQ, D_MODEL), x_sample.reshape(N_SAMPLE, D_MODEL)), ffn1, 0, ln(0, 0))
    w_qkv = jnp.concatenate([_pair_heads(attn_w_qkv[0][:, :D_MODEL], 1), attn_w_qkv[0][:, D_MODEL:]], axis=1)
    w_o = _pair_heads(attn_w_o, 1)
    q, k, v = _qkv_proj(x, w_qkv, _angle_tables(_rope_inv_lane()))
    sinks = attn_sinks[0]
    o_p, o_s, nk_s, nv_s = _swa(q, k, v, cache_k_win[0].reshape(DEC_BATCH, WINDOW, KV_DIM),
                                cache_v_win[0].reshape(DEC_BATCH, WINDOW, KV_DIM), sinks)
    x = _mix_out(o_p, o_s, x, w_o, ln(0, 1))
    x = _half_ffn((x,), ffn2, 0, ln(0, 2))

    kv_shape = (1, 1, WINDOW, N_KV_HEADS, HEAD_DIM)
    new_k_p = k[SEQ - WINDOW:SEQ].reshape(kv_shape)
    new_v_p = v[SEQ - WINDOW:SEQ].reshape(kv_shape)
    new_k_s = nk_s.reshape(1, DEC_BATCH, WINDOW, N_KV_HEADS, HEAD_DIM)
    new_v_s = nv_s.reshape(1, DEC_BATCH, WINDOW, N_KV_HEADS, HEAD_DIM)

    x = _half_ffn((x,), ffn1, 1, ln(1, 0))
    rq, rk, rv, rg = _ret_proj(x, ret_w_in[0].astype(BF16), _angle_tables(_xpos_inv_lane()))
    y_p, s_p, y_s, s_s = _ret(rq, rk, rv, rg, state_ret[0], _ret_tables(RET_CHUNK), _ret_tables(DEC_SEQ))
    x = _mix_out(y_p, y_s, x, ret_w_o, ln(1, 1))
    y_p, y_s = _half_ffn((x,), ffn2, 1, ln(1, 2), split_out=True)

    return (y_p.reshape(1, SEQ, D_MODEL), y_s.reshape(DEC_BATCH, DEC_SEQ, D_MODEL),
            new_k_p, new_v_p, new_k_s, new_v_s, s_p[None, None], s_s[None])
```

```python
import functools
import math

import jax
import jax.numpy as jnp
import numpy as np
from jax import lax
from jax.experimental import pallas as pl
from jax.experimental.pallas import tpu as pltpu

F32 = jnp.float32
BF16 = jnp.bfloat16

D_MODEL = 1024
SEQ = 16384
DEPTH = 2
DEC_BATCH = 128
DEC_SEQ = 8
PAST_LEN = 16384
N_HEADS = 16
N_KV_HEADS = 4
GROUP = N_HEADS // N_KV_HEADS
HEAD_DIM = D_MODEL // N_HEADS
WINDOW = 128
ROPE_THETA = 10000.0
ATTN_SCALE = 1.0 / math.sqrt(HEAD_DIM)
RET_HEADS = 4
RET_QK_DIM = D_MODEL // RET_HEADS
RET_V_DIM = 2 * D_MODEL // RET_HEADS
RET_CHUNK = 256
RET_THETA = 10000.0
FFN_DIM = 2816
LN_EPS = 1e-5
ALPHA = (2.0 * DEPTH) ** 0.25

N_SAMPLE = DEC_BATCH * DEC_SEQ
N_TOK = SEQ + N_SAMPLE
KV_DIM = N_KV_HEADS * HEAD_DIM
RET_QK = RET_HEADS * RET_QK_DIM
RET_V = RET_HEADS * RET_V_DIM

LANES = 128
ROW_TILE = 512
FFN_TILE = 512
FFN_CHUNK = 256
VMEM_LIMIT = 56 * 1024 * 1024

N_TILES = N_TOK // ROW_TILE
N_PROMPT_TILES = SEQ // ROW_TILE
FFN_TILES = N_TOK // FFN_TILE
FFN_PROMPT_TILES = SEQ // FFN_TILE
KV_PAIRS = N_KV_HEADS // 2

NT_DIMS = (((1,), (1,)), ((), ()))
TN_DIMS = (((0,), (0,)), ((), ()))


def _params(*sem):
    return pltpu.CompilerParams(dimension_semantics=sem, vmem_limit_bytes=VMEM_LIMIT)


def _resident(shape):
    return pl.BlockSpec(shape, lambda *_: (0,) * len(shape), pipeline_mode=pl.Buffered(1))


def _rows(width, offset_blocks=0, tile=ROW_TILE):
    return pl.BlockSpec((tile, width), lambda i: (i + offset_blocks, 0))


_ALIASED = pl.BlockSpec(memory_space=pl.ANY)
_SMEM = pl.BlockSpec(memory_space=pltpu.SMEM)


def _layer_norm(y, g, b):
    mu = jnp.mean(y, axis=-1, keepdims=True)
    d = y - mu
    var = jnp.mean(d * d, axis=-1, keepdims=True)
    return d * lax.rsqrt(var + LN_EPS) * g + b


def _silu(x):
    return x / (1.0 + jnp.exp(-x))


def _ffn_kernel(split_in, split_out, *refs):
    n_in = 2 if split_in else 1
    x_refs, (wg_ref, wu_ref, wd_ref, g_ref, b_ref), o_refs = refs[:n_in], refs[n_in:n_in + 5], refs[n_in + 5:]
    i = pl.program_id(0)
    x = x_refs[0][...]
    if split_in:
        x = jnp.where(i < FFN_PROMPT_TILES, x, x_refs[1][...])
    xb = x.astype(BF16)
    acc = None
    for c in range(FFN_DIM // FFN_CHUNK):
        sl = slice(c * FFN_CHUNK, (c + 1) * FFN_CHUNK)
        gate = jnp.dot(xb, wg_ref[:, sl].astype(BF16), preferred_element_type=F32)
        up = jnp.dot(xb, wu_ref[:, sl].astype(BF16), preferred_element_type=F32)
        h = (_silu(gate) * up).astype(BF16)
        d = jnp.dot(h, wd_ref[sl, :].astype(BF16), preferred_element_type=F32)
        acc = d if acc is None else acc + d
    y = _layer_norm(ALPHA * x + 0.5 * acc, g_ref[...], b_ref[...])
    if split_out:
        @pl.when(i < FFN_PROMPT_TILES)
        def _():
            o_refs[0][...] = y

        @pl.when(i >= FFN_PROMPT_TILES)
        def _():
            o_refs[1][...] = y
    else:
        o_refs[0][...] = y


_PROMPT_TILE = lambda i: (jnp.minimum(i, FFN_PROMPT_TILES - 1), 0)
_SAMPLE_TILE = lambda i: (jnp.maximum(i - FFN_PROMPT_TILES, 0), 0)


def _layer_weight(shape, layer):
    return pl.BlockSpec((None,) + shape, lambda i: (layer, 0, 0), pipeline_mode=pl.Buffered(1))


def _half_ffn(xs, weights, layer, ln, *, split_out=False):
    split_in = len(xs) == 2
    tile = (FFN_TILE, D_MODEL)
    x_specs = ([pl.BlockSpec(tile, _PROMPT_TILE), pl.BlockSpec(tile, _SAMPLE_TILE)]
               if split_in else [_rows(D_MODEL, tile=FFN_TILE)])
    if split_out:
        out_specs = [pl.BlockSpec(tile, _PROMPT_TILE), pl.BlockSpec(tile, _SAMPLE_TILE)]
        out_shape = [jax.ShapeDtypeStruct((SEQ, D_MODEL), F32), jax.ShapeDtypeStruct((N_SAMPLE, D_MODEL), F32)]
    else:
        out_specs = _rows(D_MODEL, tile=FFN_TILE)
        out_shape = jax.ShapeDtypeStruct((N_TOK, D_MODEL), F32)
    return pl.pallas_call(
        functools.partial(_ffn_kernel, split_in, split_out),
        grid=(FFN_TILES,),
        in_specs=x_specs + [_layer_weight((D_MODEL, FFN_DIM), layer), _layer_weight((D_MODEL, FFN_DIM), layer),
                            _layer_weight((FFN_DIM, D_MODEL), layer),
                            _resident((1, D_MODEL)), _resident((1, D_MODEL))],
        out_specs=out_specs,
        out_shape=out_shape,
        compiler_params=_params("arbitrary" if split_out else "parallel"),
        name="half_ffn",
    )(*xs, *weights, *ln)


def _mix_out_kernel(split, *refs):
    n_y = 2 if split else 1
    y_refs, (x_ref, w_ref, g_ref, b_ref, o_ref) = refs[:n_y], refs[n_y:]
    y = y_refs[-1][...].astype(BF16)
    if split:
        y = jnp.where(pl.program_id(0) < N_PROMPT_TILES, y_refs[0][...], y)
    m = jnp.dot(y, w_ref[...].astype(BF16), preferred_element_type=F32)
    o_ref[...] = _layer_norm(ALPHA * x_ref[...] + m, g_ref[...], b_ref[...])


def _mix_out(ys, x, w, ln):
    split = len(ys) == 2
    k = ys[0].shape[1]
    if split:
        y_specs = [pl.BlockSpec((ROW_TILE, k), lambda i: (jnp.minimum(i, N_PROMPT_TILES - 1), 0)),
                   pl.BlockSpec((ROW_TILE, k), lambda i: (jnp.maximum(i - N_PROMPT_TILES, 0), 0))]
        n_tiles, x_off = N_TILES, 0
    else:
        y_specs = [_rows(k)]
        n_tiles, x_off = N_TILES - N_PROMPT_TILES, N_PROMPT_TILES
    return pl.pallas_call(
        functools.partial(_mix_out_kernel, split),
        grid=(n_tiles,),
        in_specs=y_specs + [_rows(D_MODEL, x_off), _layer_weight((k, D_MODEL), 0),
                            _resident((1, D_MODEL)), _resident((1, D_MODEL))],
        out_specs=_rows(D_MODEL),
        out_shape=jax.ShapeDtypeStruct((n_tiles * ROW_TILE, D_MODEL), F32),
        compiler_params=_params("parallel"),
        name="mix_out",
    )(*ys, x, w, *ln)


def _angle_tables(inv_lane):
    base = jnp.concatenate([jnp.arange(N_PROMPT_TILES) * ROW_TILE,
                            jnp.full((N_TILES - N_PROMPT_TILES,), PAST_LEN)]).astype(F32)
    row = jnp.stack([jnp.arange(ROW_TILE), jnp.arange(ROW_TILE) % DEC_SEQ]).astype(F32)
    ta = base[:, None] * inv_lane[None, :]
    ra = row[:, :, None] * inv_lane[None, None, :]
    return jnp.stack([jnp.cos(ta), jnp.sin(ta)], axis=1), jnp.stack([jnp.cos(ra), jnp.sin(ra)], axis=1)


def _table_specs(width):
    return [pl.BlockSpec((1, 2, width), lambda i: (i, 0, 0)),
            pl.BlockSpec((1, 2, ROW_TILE, width), lambda i: (i // N_PROMPT_TILES, 0, 0, 0))]


def _tile_cos_sin(tt_ref, rt_ref):
    cb, sb = tt_ref[0, 0:1, :], tt_ref[0, 1:2, :]
    cr, sr = rt_ref[0, 0], rt_ref[0, 1]
    return cb * cr - sb * sr, sb * cr + cb * sr


def _rotate_blocks(h, n_blocks, cos, sin, shift, partner_above):
    width = cos.shape[1]
    sign = jnp.where(partner_above, -1.0, 1.0).astype(F32)
    sin = sin * jnp.concatenate([sign] * (width // LANES), axis=1)
    out = []
    for j in range(n_blocks):
        t = (j * LANES) % width
        blk = h[:, j * LANES:(j + 1) * LANES]
        partner = jnp.where(partner_above, pltpu.roll(blk, LANES - shift, 1), pltpu.roll(blk, shift, 1))
        out.append(blk * cos[:, t:t + LANES] + partner * sin[:, t:t + LANES])
    return out


def _qkv_kernel(x_ref, w_ref, tt_ref, rt_ref, q_ref, k_ref, v_ref):
    xb = x_ref[...].astype(BF16)
    h = jnp.dot(xb, w_ref[...].astype(BF16), preferred_element_type=F32)
    cos, sin = _tile_cos_sin(tt_ref, rt_ref)
    lane = lax.broadcasted_iota(jnp.int32, (1, LANES), 1)
    n_rot = (D_MODEL + KV_DIM) // LANES
    rot = _rotate_blocks(h, n_rot, cos, sin, HEAD_DIM // 2, (lane % HEAD_DIM) < HEAD_DIM // 2)
    nq = D_MODEL // LANES
    for j in range(nq):
        q_ref[:, j * LANES:(j + 1) * LANES] = (rot[j] * ATTN_SCALE).astype(BF16)
    for j in range(nq, n_rot):
        k_ref[:, (j - nq) * LANES:(j - nq + 1) * LANES] = rot[j]
    v_ref[...] = h[:, D_MODEL + KV_DIM:]


def _qkv_proj(x, w, tables):
    n = x.shape[0]
    width = D_MODEL + 2 * KV_DIM
    return pl.pallas_call(
        _qkv_kernel,
        grid=(n // ROW_TILE,),
        in_specs=[_rows(D_MODEL), _resident((D_MODEL, width)), *_table_specs(LANES)],
        out_specs=[_rows(D_MODEL), _rows(KV_DIM), _rows(KV_DIM)],
        out_shape=[jax.ShapeDtypeStruct((n, D_MODEL), BF16), jax.ShapeDtypeStruct((n, KV_DIM), F32),
                   jax.ShapeDtypeStruct((n, KV_DIM), F32)],
        compiler_params=_params("parallel"),
        name="attn_qkv",
    )(x, w, *tables)


def _rope_inv_lane():
    inv = ROPE_THETA ** (-jnp.arange(0, HEAD_DIM, 2, dtype=F32) / HEAD_DIM)
    return jnp.tile(inv, LANES // (HEAD_DIM // 2))


def _pair_heads(w, axis):
    shape = w.shape
    w = w.reshape(shape[:axis] + (KV_PAIRS, 2, GROUP, HEAD_DIM) + shape[axis + 1:])
    return jnp.swapaxes(w, axis + 1, axis + 2).reshape(shape)


def _pair_operands(kall, vall):
    s = kall.shape[0]
    low = lax.broadcasted_iota(jnp.int32, (s, LANES), 1) < HEAD_DIM
    kb, vb = kall.astype(BF16), vall.astype(BF16)
    zero = jnp.zeros_like(kb)
    one_lo = jnp.where(low, 1.0, 0.0).astype(BF16)
    one_hi = jnp.where(low, 0.0, 1.0).astype(BF16)
    kcat = jnp.concatenate([jnp.where(low, kb, zero), jnp.where(low, zero, kb)], axis=0)
    vcat = jnp.concatenate([jnp.concatenate([jnp.where(low, vb, zero), one_lo], axis=1),
                            jnp.concatenate([jnp.where(low, zero, vb), one_hi], axis=1)], axis=0)
    return kcat, vcat


def _pair_softmax(s, sink_a, sink_b):
    half = s.shape[1] // 2
    sa, sb = s[:, :half], s[:, half:]
    ma = jnp.maximum(jnp.max(sa, axis=-1, keepdims=True), sink_a)
    mb = jnp.maximum(jnp.max(sb, axis=-1, keepdims=True), sink_b)
    e = jnp.concatenate([jnp.exp(sa - ma), jnp.exp(sb - mb)], axis=1).astype(BF16)
    low = lax.broadcasted_iota(jnp.int32, (s.shape[0], LANES), 1) < HEAD_DIM
    return e, jnp.where(low, jnp.exp(sink_a - ma), jnp.exp(sink_b - mb))


SWA_SAMPLE_KEYS = 2 * WINDOW


def _swa_kernel(sinks_ref, bias_ref, sbias_ref, q_ref, kp_ref, kc_ref, vp_ref, vc_ref,
                qs_ref, kn_ref, vn_ref, ck_ref, cv_ref,
                o_ref, os_ref, nk_ref, nv_ref, s_ref, e_ref, t_ref):
    _swa_sample_step(sinks_ref, sbias_ref, qs_ref, kn_ref, vn_ref, ck_ref, cv_ref, os_ref, nk_ref, nv_ref)
    bias = bias_ref[0]
    for p in range(KV_PAIRS):
        lanes = slice(p * LANES, (p + 1) * LANES)
        kcat, vcat = _pair_operands(jnp.concatenate([kp_ref[:, lanes], kc_ref[:, lanes]], axis=0),
                                    jnp.concatenate([vp_ref[:, lanes], vc_ref[:, lanes]], axis=0))
        q = q_ref[:, p * GROUP * LANES:(p + 1) * GROUP * LANES]
        qs = jnp.concatenate([q[:, g * LANES:(g + 1) * LANES] for g in range(GROUP)], axis=0)
        s_ref[...] = lax.dot_general(qs, kcat, NT_DIMS, preferred_element_type=F32)
        for g in range(GROUP):
            rows = slice(g * WINDOW, (g + 1) * WINDOW)
            head_a = (2 * p) * GROUP + g
            e, t = _pair_softmax(s_ref[rows, :] + bias, sinks_ref[head_a], sinks_ref[head_a + GROUP])
            e_ref[rows, :] = e
            t_ref[rows, :] = t
        r = jnp.dot(e_ref[...], vcat, preferred_element_type=F32)
        o = r[:, :LANES] / (r[:, LANES:] + t_ref[...])
        for g in range(GROUP):
            blk = p * GROUP + g
            o_ref[:, blk * LANES:(blk + 1) * LANES] = o[g * WINDOW:(g + 1) * WINDOW].astype(BF16)


def _window_bias(n_rows, row_of, key_dist_and_pos):
    dist, kpos = key_dist_and_pos(row_of(np.arange(n_rows))[:, None])
    valid = (dist >= 0) & (dist < WINDOW) & (kpos >= 0)
    m = np.where(valid, 0.0, -np.inf).astype(np.float32)
    return np.concatenate([m, m], axis=1)


def _swa_sample_step(sinks_ref, bias_ref, q_ref, kn_ref, vn_ref, ck_ref, cv_ref, o_ref, nk_ref, nv_ref):
    L = DEC_SEQ
    qall = q_ref[...].astype(F32)
    qb = jnp.where(pl.program_id(0) % 2 == 1, qall[L:], qall[:L])
    bias = bias_ref[...]
    pad = jnp.zeros((SWA_SAMPLE_KEYS - WINDOW - L, LANES), F32)
    kc, vc = ck_ref[0], cv_ref[0]
    kn, vn = kn_ref[...], vn_ref[...]
    nk_ref[0, 0:WINDOW - L, :] = kc[L:, :]
    nk_ref[0, WINDOW - L:WINDOW, :] = kn
    nv_ref[0, 0:WINDOW - L, :] = vc[L:, :]
    nv_ref[0, WINDOW - L:WINDOW, :] = vn
    for p in range(KV_PAIRS):
        lanes = slice(p * LANES, (p + 1) * LANES)
        kcat, vcat = _pair_operands(jnp.concatenate([kc[:, lanes], kn[:, lanes], pad], axis=0),
                                    jnp.concatenate([vc[:, lanes], vn[:, lanes], pad], axis=0))
        qs = jnp.concatenate(
            [qb[:, (p * GROUP + g) * LANES:(p * GROUP + g + 1) * LANES] for g in range(GROUP)],
            axis=0).astype(BF16)
        sink_col = lambda first: jnp.concatenate(
            [jnp.full((L, 1), sinks_ref[first + g], F32) for g in range(GROUP)], axis=0)
        s = lax.dot_general(qs, kcat, NT_DIMS, preferred_element_type=F32) + bias
        e, t = _pair_softmax(s, sink_col(2 * p * GROUP), sink_col((2 * p + 1) * GROUP))
        r = jnp.dot(e, vcat, preferred_element_type=F32)
        o = r[:, :LANES] / (r[:, LANES:] + t)
        for g in range(GROUP):
            blk = p * GROUP + g
            o_ref[:, blk * LANES:(blk + 1) * LANES] = o[g * L:(g + 1) * L]


def _swa(q, k, v, cache_k, cache_v, sinks):
    assert SEQ // WINDOW == DEC_BATCH
    rows = GROUP * WINDOW
    si = np.arange(2 * WINDOW)[None, :]
    bias = jnp.asarray(np.stack([
        _window_bias(WINDOW, lambda r: r, lambda qi: (qi + WINDOW - si, blk * WINDOW - WINDOW + si))
        for blk in (0, 1)]))
    cj = np.arange(SWA_SAMPLE_KEYS)[None, :]
    sbias = jnp.asarray(_window_bias(
        GROUP * DEC_SEQ, lambda r: r % DEC_SEQ,
        lambda qi: (np.where(cj < WINDOW + DEC_SEQ, PAST_LEN + qi - (PAST_LEN - WINDOW + cj), -1),
                    PAST_LEN - WINDOW + cj)))
    cur = lambda i: (i, 0)
    prev = lambda i: (jnp.maximum(i - 1, 0), 0)
    seq = lambda i: (i + SEQ // DEC_SEQ, 0)
    cache = lambda i: (i, 0, 0)
    return pl.pallas_call(
        _swa_kernel,
        grid=(DEC_BATCH,),
        in_specs=[_SMEM, pl.BlockSpec((1, WINDOW, 4 * WINDOW), lambda i: (jnp.minimum(i, 1), 0, 0)),
                  _resident((GROUP * DEC_SEQ, 2 * SWA_SAMPLE_KEYS)),
                  pl.BlockSpec((WINDOW, D_MODEL), cur),
                  pl.BlockSpec((WINDOW, KV_DIM), prev), pl.BlockSpec((WINDOW, KV_DIM), cur),
                  pl.BlockSpec((WINDOW, KV_DIM), prev), pl.BlockSpec((WINDOW, KV_DIM), cur),
                  pl.BlockSpec((2 * DEC_SEQ, D_MODEL), lambda i: (i // 2 + SEQ // (2 * DEC_SEQ), 0)),
                  pl.BlockSpec((DEC_SEQ, KV_DIM), seq), pl.BlockSpec((DEC_SEQ, KV_DIM), seq),
                  pl.BlockSpec((1, WINDOW, KV_DIM), cache), pl.BlockSpec((1, WINDOW, KV_DIM), cache)],
        out_specs=[pl.BlockSpec((WINDOW, D_MODEL), cur), pl.BlockSpec((DEC_SEQ, D_MODEL), cur),
                   pl.BlockSpec((1, WINDOW, KV_DIM), cache), pl.BlockSpec((1, WINDOW, KV_DIM), cache)],
        out_shape=[jax.ShapeDtypeStruct((SEQ, D_MODEL), BF16), jax.ShapeDtypeStruct((N_SAMPLE, D_MODEL), F32),
                   jax.ShapeDtypeStruct((DEC_BATCH, WINDOW, KV_DIM), F32),
                   jax.ShapeDtypeStruct((DEC_BATCH, WINDOW, KV_DIM), F32)],
        scratch_shapes=[pltpu.VMEM((rows, 4 * WINDOW), F32), pltpu.VMEM((rows, 4 * WINDOW), BF16),
                        pltpu.VMEM((rows, LANES), F32)],
        compiler_params=_params("parallel"),
        name="swa",
    )(sinks, bias, sbias, q, k, k, v, v, q, k, v, cache_k, cache_v)


def _ret_proj_kernel(x_ref, w_ref, tt_ref, rt_ref, q_ref, k_ref, v_ref, g_ref):
    xb = x_ref[...].astype(BF16)
    hqk = jnp.dot(xb, w_ref[:, :2 * RET_QK], preferred_element_type=F32)
    cos, sin = _tile_cos_sin(tt_ref, rt_ref)
    lane = lax.broadcasted_iota(jnp.int32, (1, LANES), 1)
    rot = _rotate_blocks(hqk, 2 * RET_QK // LANES, cos, sin, 1, (lane % 2) == 0)
    nq = RET_QK // LANES
    for j in range(nq):
        q_ref[:, j * LANES:(j + 1) * LANES] = rot[j].astype(BF16)
        k_ref[:, j * LANES:(j + 1) * LANES] = rot[nq + j] * (RET_QK_DIM ** -0.5)
    v_ref[...] = jnp.dot(xb, w_ref[:, 2 * RET_QK:2 * RET_QK + RET_V],
                         preferred_element_type=F32).astype(BF16)
    g_ref[...] = _silu(jnp.dot(xb, w_ref[:, 2 * RET_QK + RET_V:], preferred_element_type=F32))


def _ret_proj(x, w, tables):
    n = x.shape[0]
    width = 2 * RET_QK + 2 * RET_V
    return pl.pallas_call(
        _ret_proj_kernel,
        grid=(n // ROW_TILE,),
        in_specs=[_rows(D_MODEL), _resident((D_MODEL, width)), *_table_specs(RET_QK_DIM)],
        out_specs=[_rows(RET_QK), _rows(RET_QK), _rows(RET_V), _rows(RET_V)],
        out_shape=[jax.ShapeDtypeStruct((n, RET_QK), BF16), jax.ShapeDtypeStruct((n, RET_QK), F32),
                   jax.ShapeDtypeStruct((n, RET_V), BF16), jax.ShapeDtypeStruct((n, RET_V), F32)],
        compiler_params=_params("parallel"),
        name="ret_proj",
    )(x, w, *tables)


def _xpos_inv_lane():
    inv = 1.0 / (RET_THETA ** jnp.linspace(0.0, 1.0, RET_QK_DIM // 2, dtype=F32))
    return jnp.repeat(inv, 2)


def _ret_log_decay():
    return jnp.log(1.0 - 2.0 ** (-5.0 - jnp.arange(RET_HEADS, dtype=F32)))


def _ret_tables(L):
    lg = _ret_log_decay()
    idx = jnp.arange(L, dtype=F32)
    diff = idx[:, None] - idx[None, :]
    causal = diff >= 0
    dmat = jnp.where(causal[None], jnp.exp(lg[:, None, None] * jnp.where(causal, diff, 0.0)[None]), 0.0)
    xi = jnp.exp(lg[:, None] * (idx[None, :] + 1.0))
    zeta = jnp.exp(lg[:, None] * (L - 1.0 - idx[None, :]))
    decay = jnp.exp(lg * L)
    rep = lambda a: jnp.broadcast_to(a[:, :, None], (RET_HEADS, L, LANES))
    return dmat, rep(xi), rep(zeta), decay


def _lanes(t, width):
    return jnp.concatenate([t] * (width // LANES), axis=1)


def _group_norm_gate(o, gate):
    mu = jnp.mean(o, axis=-1, keepdims=True)
    d = o - mu
    var = jnp.mean(d * d, axis=-1, keepdims=True)
    return gate * (d * lax.rsqrt(var + LN_EPS))


def _retention_head(qh, kh, vh, s_old, dmat, xi, zeta, decay):
    att = lax.dot_general(qh, kh.astype(BF16), NT_DIMS, preferred_element_type=F32) * dmat
    o = jnp.dot(att.astype(BF16), vh, preferred_element_type=F32)
    o = o + jnp.dot(qh, s_old.astype(BF16), preferred_element_type=F32) * _lanes(xi, RET_V_DIM)
    kz = (kh * _lanes(zeta, RET_QK_DIM)).astype(BF16)
    upd = lax.dot_general(kz, vh, TN_DIMS, preferred_element_type=F32)
    return o, decay * s_old + upd


RET_SAMPLE_BATCH = DEC_BATCH // (SEQ // RET_CHUNK)


def _ret_kernel(q_ref, k_ref, v_ref, g_ref, dmat_ref, xi_ref, zeta_ref, decay_ref,
                x_ref, wo_ref, lg_ref, lb_ref,
                qs_ref, ks_ref, vs_ref, gs_ref, s_in_ref, sdmat_ref, sxi_ref, szeta_ref, sdecay_ref,
                xo_ref, sfin_ref, ys_ref, s_out_ref, s_ref, ys_scr):
    i = pl.program_id(0)

    @pl.when(i == 0)
    def _():
        s_ref[...] = jnp.zeros_like(s_ref)

    L = DEC_SEQ
    qa = qs_ref[...].astype(F32)
    va = vs_ref[...].astype(F32)
    mixed = None
    for h in range(RET_HEADS):
        qk = slice(h * RET_QK_DIM, (h + 1) * RET_QK_DIM)
        vg = slice(h * RET_V_DIM, (h + 1) * RET_V_DIM)
        o, s_new = _retention_head(q_ref[:, qk], k_ref[:, qk], v_ref[:, vg], s_ref[h],
                                   dmat_ref[h], xi_ref[h], zeta_ref[h], decay_ref[h])
        s_ref[h] = s_new
        yh = _group_norm_gate(o, g_ref[:, vg]).astype(BF16)
        d = jnp.dot(yh, wo_ref[vg, :].astype(BF16), preferred_element_type=F32)
        mixed = d if mixed is None else mixed + d
        for b in range(RET_SAMPLE_BATCH):
            rows = slice(b * L, (b + 1) * L)
            o, s_new = _retention_head(qa[rows, qk].astype(BF16), ks_ref[rows, qk], va[rows, vg].astype(BF16),
                                       s_in_ref[b, h], sdmat_ref[h], sxi_ref[h], szeta_ref[h], sdecay_ref[h])
            s_out_ref[b, h] = s_new
            ys_scr[rows, vg] = _group_norm_gate(o, gs_ref[rows, vg])
    ys_ref[...] = ys_scr[...].astype(BF16)
    xo_ref[...] = _layer_norm(ALPHA * x_ref[...] + mixed, lg_ref[...], lb_ref[...])

    @pl.when(i == pl.num_programs(0) - 1)
    def _():
        sfin_ref[...] = s_ref[...]


def _ret(q, k, v, g, state, tables, sample_tables, x, w_o, ln):
    nc = SEQ // RET_CHUNK
    bb = RET_SAMPLE_BATCH
    rows = bb * DEC_SEQ
    cur = lambda i: (i, 0)
    tok = lambda i: (i + SEQ // rows, 0)
    st = lambda i: (i, 0, 0, 0)
    state_shape = (RET_HEADS, RET_QK_DIM, RET_V_DIM)
    state_block = (bb,) + state_shape
    table_specs = lambda n: [_resident((RET_HEADS, n, n)), _resident((RET_HEADS, n, LANES)),
                             _resident((RET_HEADS, n, LANES)), _SMEM]
    return pl.pallas_call(
        _ret_kernel,
        grid=(nc,),
        in_specs=[pl.BlockSpec((RET_CHUNK, RET_QK), cur), pl.BlockSpec((RET_CHUNK, RET_QK), cur),
                  pl.BlockSpec((RET_CHUNK, RET_V), cur), pl.BlockSpec((RET_CHUNK, RET_V), cur),
                  *table_specs(RET_CHUNK),
                  pl.BlockSpec((RET_CHUNK, D_MODEL), cur), _layer_weight((RET_V, D_MODEL), 0),
                  _resident((1, D_MODEL)), _resident((1, D_MODEL)),
                  pl.BlockSpec((rows, RET_QK), tok), pl.BlockSpec((rows, RET_QK), tok),
                  pl.BlockSpec((rows, RET_V), tok), pl.BlockSpec((rows, RET_V), tok),
                  pl.BlockSpec(state_block, st), *table_specs(DEC_SEQ)],
        out_specs=[pl.BlockSpec((RET_CHUNK, D_MODEL), cur), pl.BlockSpec(state_shape, lambda i: (0, 0, 0)),
                   pl.BlockSpec((rows, RET_V), cur), pl.BlockSpec(state_block, st)],
        out_shape=[jax.ShapeDtypeStruct((SEQ, D_MODEL), F32), jax.ShapeDtypeStruct(state_shape, F32),
                   jax.ShapeDtypeStruct((N_SAMPLE, RET_V), BF16),
                   jax.ShapeDtypeStruct((DEC_BATCH,) + state_shape, F32)],
        scratch_shapes=[pltpu.VMEM(state_shape, F32), pltpu.VMEM((rows, RET_V), F32)],
        compiler_params=_params("arbitrary"),
        name="ret",
    )(q, k, v, g, *tables, x, w_o, *ln, q, k, v, g, state, *sample_tables)


def kernel(x_prompt, x_sample, cache_k_win, cache_v_win, state_ret, ffn1_w_gate, ffn1_w_up, ffn1_w_down,
           ffn2_w_gate, ffn2_w_up, ffn2_w_down, ln_g, ln_b, attn_w_qkv, attn_w_o, attn_sinks, ret_w_in,
           ret_w_o):
    ln = lambda i, j: (ln_g[i, j][None, :], ln_b[i, j][None, :])
    ffn1 = (ffn1_w_gate, ffn1_w_up, ffn1_w_down)
    ffn2 = (ffn2_w_gate, ffn2_w_up, ffn2_w_down)

    x = _half_ffn((x_prompt.reshape(SEQ, D_MODEL), x_sample.reshape(N_SAMPLE, D_MODEL)), ffn1, 0, ln(0, 0))
    w_qkv = jnp.concatenate([_pair_heads(attn_w_qkv[0][:, :D_MODEL], 1), attn_w_qkv[0][:, D_MODEL:]], axis=1)
    w_o = _pair_heads(attn_w_o, 1)
    q, k, v = _qkv_proj(x, w_qkv, _angle_tables(_rope_inv_lane()))
    sinks = attn_sinks[0]
    o_p, o_s, nk_s, nv_s = _swa(q, k, v, cache_k_win[0].reshape(DEC_BATCH, WINDOW, KV_DIM),
                                cache_v_win[0].reshape(DEC_BATCH, WINDOW, KV_DIM), sinks)
    x = _mix_out((o_p, o_s), x, w_o, ln(0, 1))
    x = _half_ffn((x,), ffn2, 0, ln(0, 2))

    kv_shape = (1, 1, WINDOW, N_KV_HEADS, HEAD_DIM)
    new_k_p = k[SEQ - WINDOW:SEQ].reshape(kv_shape)
    new_v_p = v[SEQ - WINDOW:SEQ].reshape(kv_shape)
    new_k_s = nk_s.reshape(1, DEC_BATCH, WINDOW, N_KV_HEADS, HEAD_DIM)
    new_v_s = nv_s.reshape(1, DEC_BATCH, WINDOW, N_KV_HEADS, HEAD_DIM)

    x = _half_ffn((x,), ffn1, 1, ln(1, 0))
    rq, rk, rv, rg = _ret_proj(x, ret_w_in[0].astype(BF16), _angle_tables(_xpos_inv_lane()))
    x_p, s_p, y_s, s_s = _ret(rq, rk, rv, rg, state_ret[0], _ret_tables(RET_CHUNK), _ret_tables(DEC_SEQ),
                              x, ret_w_o, ln(1, 1))
    x_s = _mix_out((y_s,), x, ret_w_o, ln(1, 1))
    y_p, y_s = _half_ffn((x_p, x_s), ffn2, 1, ln(1, 2), split_out=True)

    return (y_p.reshape(1, SEQ, D_MODEL), y_s.reshape(DEC_BATCH, DEC_SEQ, D_MODEL),
            new_k_p, new_v_p, new_k_s, new_v_s, s_p[None, None], s_s[None])
```

```python
import functools
import math

import jax
import jax.numpy as jnp
import numpy as np
from jax import lax
from jax.experimental import pallas as pl
from jax.experimental.pallas import tpu as pltpu

F32 = jnp.float32
BF16 = jnp.bfloat16

D_MODEL = 1024
SEQ = 16384
DEPTH = 2
DEC_BATCH = 128
DEC_SEQ = 8
PAST_LEN = 16384
N_HEADS = 16
N_KV_HEADS = 4
GROUP = N_HEADS // N_KV_HEADS
HEAD_DIM = D_MODEL // N_HEADS
WINDOW = 128
ROPE_THETA = 10000.0
ATTN_SCALE = 1.0 / math.sqrt(HEAD_DIM)
RET_HEADS = 4
RET_QK_DIM = D_MODEL // RET_HEADS
RET_V_DIM = 2 * D_MODEL // RET_HEADS
RET_CHUNK = 256
RET_THETA = 10000.0
FFN_DIM = 2816
LN_EPS = 1e-5
ALPHA = (2.0 * DEPTH) ** 0.25

N_SAMPLE = DEC_BATCH * DEC_SEQ
N_TOK = SEQ + N_SAMPLE
KV_DIM = N_KV_HEADS * HEAD_DIM
RET_QK = RET_HEADS * RET_QK_DIM
RET_V = RET_HEADS * RET_V_DIM

LANES = 128
ROW_TILE = 512
FFN_TILE = 512
FFN_CHUNK = 256
VMEM_LIMIT = 56 * 1024 * 1024

N_TILES = N_TOK // ROW_TILE
N_PROMPT_TILES = SEQ // ROW_TILE
FFN_TILES = N_TOK // FFN_TILE
FFN_PROMPT_TILES = SEQ // FFN_TILE
KV_PAIRS = N_KV_HEADS // 2

NT_DIMS = (((1,), (1,)), ((), ()))
TN_DIMS = (((0,), (0,)), ((), ()))


def _params(*sem):
    return pltpu.CompilerParams(dimension_semantics=sem, vmem_limit_bytes=VMEM_LIMIT)


def _resident(shape):
    return pl.BlockSpec(shape, lambda *_: (0,) * len(shape), pipeline_mode=pl.Buffered(1))


def _rows(width, offset_blocks=0, tile=ROW_TILE):
    return pl.BlockSpec((tile, width), lambda i: (i + offset_blocks, 0))


_ALIASED = pl.BlockSpec(memory_space=pl.ANY)
_SMEM = pl.BlockSpec(memory_space=pltpu.SMEM)


def _layer_norm(y, g, b):
    mu = jnp.mean(y, axis=-1, keepdims=True)
    d = y - mu
    var = jnp.mean(d * d, axis=-1, keepdims=True)
    return d * lax.rsqrt(var + LN_EPS) * g + b


def _silu(x):
    return x / (1.0 + jnp.exp(-x))


def _ffn_kernel(split_in, split_out, *refs):
    n_in = 2 if split_in else 1
    x_refs, (wg_ref, wu_ref, wd_ref, g_ref, b_ref), o_refs = refs[:n_in], refs[n_in:n_in + 5], refs[n_in + 5:]
    i = pl.program_id(0)
    x = x_refs[0][...]
    if split_in:
        x = jnp.where(i < FFN_PROMPT_TILES, x, x_refs[1][...])
    xb = x.astype(BF16)
    acc = None
    for c in range(FFN_DIM // FFN_CHUNK):
        sl = slice(c * FFN_CHUNK, (c + 1) * FFN_CHUNK)
        gate = jnp.dot(xb, wg_ref[:, sl].astype(BF16), preferred_element_type=F32)
        up = jnp.dot(xb, wu_ref[:, sl].astype(BF16), preferred_element_type=F32)
        h = (_silu(gate) * up).astype(BF16)
        d = jnp.dot(h, wd_ref[sl, :].astype(BF16), preferred_element_type=F32)
        acc = d if acc is None else acc + d
    y = _layer_norm(ALPHA * x + 0.5 * acc, g_ref[...], b_ref[...])
    if split_out:
        @pl.when(i < FFN_PROMPT_TILES)
        def _():
            o_refs[0][...] = y

        @pl.when(i >= FFN_PROMPT_TILES)
        def _():
            o_refs[1][...] = y
    else:
        o_refs[0][...] = y


_PROMPT_TILE = lambda i: (jnp.minimum(i, FFN_PROMPT_TILES - 1), 0)
_SAMPLE_TILE = lambda i: (jnp.maximum(i - FFN_PROMPT_TILES, 0), 0)


def _layer_weight(shape, layer):
    return pl.BlockSpec((None,) + shape, lambda i: (layer, 0, 0), pipeline_mode=pl.Buffered(1))


def _half_ffn(xs, weights, layer, ln, *, split_out=False):
    split_in = len(xs) == 2
    tile = (FFN_TILE, D_MODEL)
    x_specs = ([pl.BlockSpec(tile, _PROMPT_TILE), pl.BlockSpec(tile, _SAMPLE_TILE)]
               if split_in else [_rows(D_MODEL, tile=FFN_TILE)])
    if split_out:
        out_specs = [pl.BlockSpec(tile, _PROMPT_TILE), pl.BlockSpec(tile, _SAMPLE_TILE)]
        out_shape = [jax.ShapeDtypeStruct((SEQ, D_MODEL), F32), jax.ShapeDtypeStruct((N_SAMPLE, D_MODEL), F32)]
    else:
        out_specs = _rows(D_MODEL, tile=FFN_TILE)
        out_shape = jax.ShapeDtypeStruct((N_TOK, D_MODEL), F32)
    return pl.pallas_call(
        functools.partial(_ffn_kernel, split_in, split_out),
        grid=(FFN_TILES,),
        in_specs=x_specs + [_layer_weight((D_MODEL, FFN_DIM), layer), _layer_weight((D_MODEL, FFN_DIM), layer),
                            _layer_weight((FFN_DIM, D_MODEL), layer),
                            _resident((1, D_MODEL)), _resident((1, D_MODEL))],
        out_specs=out_specs,
        out_shape=out_shape,
        compiler_params=_params("arbitrary" if split_out else "parallel"),
        name="half_ffn",
    )(*xs, *weights, *ln)


def _mix_out_kernel(split, *refs):
    n_y = 2 if split else 1
    y_refs, (x_ref, w_ref, g_ref, b_ref, o_ref) = refs[:n_y], refs[n_y:]
    y = y_refs[-1][...].astype(BF16)
    if split:
        y = jnp.where(pl.program_id(0) < N_PROMPT_TILES, y_refs[0][...], y)
    m = jnp.dot(y, w_ref[...].astype(BF16), preferred_element_type=F32)
    o_ref[...] = _layer_norm(ALPHA * x_ref[...] + m, g_ref[...], b_ref[...])


def _mix_out(ys, x, w, ln):
    split = len(ys) == 2
    k = ys[0].shape[1]
    if split:
        y_specs = [pl.BlockSpec((ROW_TILE, k), lambda i: (jnp.minimum(i, N_PROMPT_TILES - 1), 0)),
                   pl.BlockSpec((ROW_TILE, k), lambda i: (jnp.maximum(i - N_PROMPT_TILES, 0), 0))]
        n_tiles, x_off = N_TILES, 0
    else:
        y_specs = [_rows(k)]
        n_tiles, x_off = N_TILES - N_PROMPT_TILES, N_PROMPT_TILES
    return pl.pallas_call(
        functools.partial(_mix_out_kernel, split),
        grid=(n_tiles,),
        in_specs=y_specs + [_rows(D_MODEL, x_off), _layer_weight((k, D_MODEL), 0),
                            _resident((1, D_MODEL)), _resident((1, D_MODEL))],
        out_specs=_rows(D_MODEL),
        out_shape=jax.ShapeDtypeStruct((n_tiles * ROW_TILE, D_MODEL), F32),
        compiler_params=_params("parallel"),
        name="mix_out",
    )(*ys, x, w, *ln)


def _angle_tables(inv_lane):
    base = jnp.concatenate([jnp.arange(N_PROMPT_TILES) * ROW_TILE,
                            jnp.full((N_TILES - N_PROMPT_TILES,), PAST_LEN)]).astype(F32)
    row = jnp.stack([jnp.arange(ROW_TILE), jnp.arange(ROW_TILE) % DEC_SEQ]).astype(F32)
    ta = base[:, None] * inv_lane[None, :]
    ra = row[:, :, None] * inv_lane[None, None, :]
    return jnp.stack([jnp.cos(ta), jnp.sin(ta)], axis=1), jnp.stack([jnp.cos(ra), jnp.sin(ra)], axis=1)


def _table_specs(width):
    return [pl.BlockSpec((1, 2, width), lambda i: (i, 0, 0)),
            pl.BlockSpec((1, 2, ROW_TILE, width), lambda i: (i // N_PROMPT_TILES, 0, 0, 0))]


def _tile_cos_sin(tt_ref, rt_ref):
    cb, sb = tt_ref[0, 0:1, :], tt_ref[0, 1:2, :]
    cr, sr = rt_ref[0, 0], rt_ref[0, 1]
    return cb * cr - sb * sr, sb * cr + cb * sr


def _rotate_blocks(h, n_blocks, cos, sin, shift, partner_above):
    width = cos.shape[1]
    sign = jnp.where(partner_above, -1.0, 1.0).astype(F32)
    sin = sin * jnp.concatenate([sign] * (width // LANES), axis=1)
    out = []
    for j in range(n_blocks):
        t = (j * LANES) % width
        blk = h[:, j * LANES:(j + 1) * LANES]
        partner = jnp.where(partner_above, pltpu.roll(blk, LANES - shift, 1), pltpu.roll(blk, shift, 1))
        out.append(blk * cos[:, t:t + LANES] + partner * sin[:, t:t + LANES])
    return out


def _qkv_kernel(x_ref, w_ref, tt_ref, rt_ref, q_ref, k_ref, v_ref):
    xb = x_ref[...].astype(BF16)
    h = jnp.dot(xb, w_ref[...].astype(BF16), preferred_element_type=F32)
    cos, sin = _tile_cos_sin(tt_ref, rt_ref)
    lane = lax.broadcasted_iota(jnp.int32, (1, LANES), 1)
    n_rot = (D_MODEL + KV_DIM) // LANES
    rot = _rotate_blocks(h, n_rot, cos, sin, HEAD_DIM // 2, (lane % HEAD_DIM) < HEAD_DIM // 2)
    nq = D_MODEL // LANES
    for j in range(nq):
        q_ref[:, j * LANES:(j + 1) * LANES] = (rot[j] * ATTN_SCALE).astype(BF16)
    for j in range(nq, n_rot):
        k_ref[:, (j - nq) * LANES:(j - nq + 1) * LANES] = rot[j]
    v_ref[...] = h[:, D_MODEL + KV_DIM:]


def _qkv_proj(x, w, tables):
    n = x.shape[0]
    width = D_MODEL + 2 * KV_DIM
    return pl.pallas_call(
        _qkv_kernel,
        grid=(n // ROW_TILE,),
        in_specs=[_rows(D_MODEL), _resident((D_MODEL, width)), *_table_specs(LANES)],
        out_specs=[_rows(D_MODEL), _rows(KV_DIM), _rows(KV_DIM)],
        out_shape=[jax.ShapeDtypeStruct((n, D_MODEL), BF16), jax.ShapeDtypeStruct((n, KV_DIM), F32),
                   jax.ShapeDtypeStruct((n, KV_DIM), F32)],
        compiler_params=_params("parallel"),
        name="attn_qkv",
    )(x, w, *tables)


def _rope_inv_lane():
    inv = ROPE_THETA ** (-jnp.arange(0, HEAD_DIM, 2, dtype=F32) / HEAD_DIM)
    return jnp.tile(inv, LANES // (HEAD_DIM // 2))


def _pair_heads(w, axis):
    shape = w.shape
    w = w.reshape(shape[:axis] + (KV_PAIRS, 2, GROUP, HEAD_DIM) + shape[axis + 1:])
    return jnp.swapaxes(w, axis + 1, axis + 2).reshape(shape)


def _pair_operands(kall, vall):
    s = kall.shape[0]
    low = lax.broadcasted_iota(jnp.int32, (s, LANES), 1) < HEAD_DIM
    kb, vb = kall.astype(BF16), vall.astype(BF16)
    zero = jnp.zeros_like(kb)
    one_lo = jnp.where(low, 1.0, 0.0).astype(BF16)
    one_hi = jnp.where(low, 0.0, 1.0).astype(BF16)
    kcat = jnp.concatenate([jnp.where(low, kb, zero), jnp.where(low, zero, kb)], axis=0)
    vcat = jnp.concatenate([jnp.concatenate([jnp.where(low, vb, zero), one_lo], axis=1),
                            jnp.concatenate([jnp.where(low, zero, vb), one_hi], axis=1)], axis=0)
    return kcat, vcat


def _pair_softmax(s, sink_a, sink_b):
    half = s.shape[1] // 2
    sa, sb = s[:, :half], s[:, half:]
    ma = jnp.maximum(jnp.max(sa, axis=-1, keepdims=True), sink_a)
    mb = jnp.maximum(jnp.max(sb, axis=-1, keepdims=True), sink_b)
    e = jnp.concatenate([jnp.exp(sa - ma), jnp.exp(sb - mb)], axis=1).astype(BF16)
    low = lax.broadcasted_iota(jnp.int32, (s.shape[0], LANES), 1) < HEAD_DIM
    return e, jnp.where(low, jnp.exp(sink_a - ma), jnp.exp(sink_b - mb))


SWA_SAMPLE_KEYS = 2 * WINDOW


SWA_STEP = 2


def _swa_kernel(sinks_ref, bias_ref, sbias_ref, q_ref, kp_ref, kc_ref, vp_ref, vc_ref,
                qs_ref, kn_ref, vn_ref, ck_ref, cv_ref,
                o_ref, os_ref, nk_ref, nv_ref, s_ref, e_ref, t_ref, os_scr):
    first = jnp.minimum(pl.program_id(0), 1)
    qall = qs_ref[...].astype(F32)
    for j in range(SWA_STEP):
        seq_j = slice(j * DEC_SEQ, (j + 1) * DEC_SEQ)
        _swa_sample_seq(sinks_ref, sbias_ref[...], qall[seq_j], kn_ref[seq_j, :], vn_ref[seq_j, :],
                        ck_ref[j], cv_ref[j], os_scr.at[seq_j], nk_ref.at[j], nv_ref.at[j])
        rows_j = slice(j * WINDOW, (j + 1) * WINDOW)
        bias = bias_ref[first] if j == 0 else bias_ref[1]
        for p in range(KV_PAIRS):
            lanes = slice(p * LANES, (p + 1) * LANES)
            k_prev = kp_ref[:, lanes] if j == 0 else kc_ref[(j - 1) * WINDOW:j * WINDOW, lanes]
            v_prev = vp_ref[:, lanes] if j == 0 else vc_ref[(j - 1) * WINDOW:j * WINDOW, lanes]
            kcat, vcat = _pair_operands(jnp.concatenate([k_prev, kc_ref[rows_j, lanes]], axis=0),
                                        jnp.concatenate([v_prev, vc_ref[rows_j, lanes]], axis=0))
            q = q_ref[rows_j, p * GROUP * LANES:(p + 1) * GROUP * LANES]
            qs = jnp.concatenate([q[:, g * LANES:(g + 1) * LANES] for g in range(GROUP)], axis=0)
            c = j * KV_PAIRS + p
            s_ref[c] = lax.dot_general(qs, kcat, NT_DIMS, preferred_element_type=F32)
            for g in range(GROUP):
                rows = slice(g * WINDOW, (g + 1) * WINDOW)
                head_a = (2 * p) * GROUP + g
                e, t = _pair_softmax(s_ref[c, rows, :] + bias, sinks_ref[head_a], sinks_ref[head_a + GROUP])
                e_ref[c, rows, :] = e
                t_ref[c, rows, :] = t
            r = jnp.dot(e_ref[c], vcat, preferred_element_type=F32)
            o = r[:, :LANES] / (r[:, LANES:] + t_ref[c])
            for g in range(GROUP):
                blk = p * GROUP + g
                o_ref[rows_j, blk * LANES:(blk + 1) * LANES] = o[g * WINDOW:(g + 1) * WINDOW].astype(BF16)
    os_ref[...] = os_scr[...].astype(BF16)


def _window_bias(n_rows, row_of, key_dist_and_pos):
    dist, kpos = key_dist_and_pos(row_of(np.arange(n_rows))[:, None])
    valid = (dist >= 0) & (dist < WINDOW) & (kpos >= 0)
    m = np.where(valid, 0.0, -np.inf).astype(np.float32)
    return np.concatenate([m, m], axis=1)


def _swa_sample_seq(sinks_ref, bias, qb, kn, vn, kc, vc, o_ref, nk_ref, nv_ref):
    L = DEC_SEQ
    pad = jnp.zeros((SWA_SAMPLE_KEYS - WINDOW - L, LANES), F32)
    nk_ref[0:WINDOW - L, :] = kc[L:, :]
    nk_ref[WINDOW - L:WINDOW, :] = kn
    nv_ref[0:WINDOW - L, :] = vc[L:, :]
    nv_ref[WINDOW - L:WINDOW, :] = vn
    for p in range(KV_PAIRS):
        lanes = slice(p * LANES, (p + 1) * LANES)
        kcat, vcat = _pair_operands(jnp.concatenate([kc[:, lanes], kn[:, lanes], pad], axis=0),
                                    jnp.concatenate([vc[:, lanes], vn[:, lanes], pad], axis=0))
        qs = jnp.concatenate(
            [qb[:, (p * GROUP + g) * LANES:(p * GROUP + g + 1) * LANES] for g in range(GROUP)],
            axis=0).astype(BF16)
        sink_col = lambda first: jnp.concatenate(
            [jnp.full((L, 1), sinks_ref[first + g], F32) for g in range(GROUP)], axis=0)
        s = lax.dot_general(qs, kcat, NT_DIMS, preferred_element_type=F32) + bias
        e, t = _pair_softmax(s, sink_col(2 * p * GROUP), sink_col((2 * p + 1) * GROUP))
        r = jnp.dot(e, vcat, preferred_element_type=F32)
        o = r[:, :LANES] / (r[:, LANES:] + t)
        for g in range(GROUP):
            blk = p * GROUP + g
            o_ref[:, blk * LANES:(blk + 1) * LANES] = o[g * L:(g + 1) * L]


def _swa(q, k, v, cache_k, cache_v, sinks):
    assert SEQ // WINDOW == DEC_BATCH
    rows = GROUP * WINDOW
    blk_rows, seq_rows = SWA_STEP * WINDOW, SWA_STEP * DEC_SEQ
    si = np.arange(2 * WINDOW)[None, :]
    bias = jnp.asarray(np.stack([
        _window_bias(WINDOW, lambda r: r, lambda qi: (qi + WINDOW - si, blk * WINDOW - WINDOW + si))
        for blk in (0, 1)]))
    cj = np.arange(SWA_SAMPLE_KEYS)[None, :]
    sbias = jnp.asarray(_window_bias(
        GROUP * DEC_SEQ, lambda r: r % DEC_SEQ,
        lambda qi: (np.where(cj < WINDOW + DEC_SEQ, PAST_LEN + qi - (PAST_LEN - WINDOW + cj), -1),
                    PAST_LEN - WINDOW + cj)))
    cur = lambda i: (i, 0)
    prev = lambda i: (jnp.maximum(SWA_STEP * i - 1, 0), 0)
    seq = lambda i: (i + SEQ // seq_rows, 0)
    cache = lambda i: (i, 0, 0)
    n_sets = SWA_STEP * KV_PAIRS
    return pl.pallas_call(
        _swa_kernel,
        grid=(DEC_BATCH // SWA_STEP,),
        in_specs=[_SMEM, _resident((2, WINDOW, 4 * WINDOW)),
                  _resident((GROUP * DEC_SEQ, 2 * SWA_SAMPLE_KEYS)),
                  pl.BlockSpec((blk_rows, D_MODEL), cur),
                  pl.BlockSpec((WINDOW, KV_DIM), prev), pl.BlockSpec((blk_rows, KV_DIM), cur),
                  pl.BlockSpec((WINDOW, KV_DIM), prev), pl.BlockSpec((blk_rows, KV_DIM), cur),
                  pl.BlockSpec((seq_rows, D_MODEL), seq),
                  pl.BlockSpec((seq_rows, KV_DIM), seq), pl.BlockSpec((seq_rows, KV_DIM), seq),
                  pl.BlockSpec((SWA_STEP, WINDOW, KV_DIM), cache), pl.BlockSpec((SWA_STEP, WINDOW, KV_DIM), cache)],
        out_specs=[pl.BlockSpec((blk_rows, D_MODEL), cur), pl.BlockSpec((seq_rows, D_MODEL), cur),
                   pl.BlockSpec((SWA_STEP, WINDOW, KV_DIM), cache), pl.BlockSpec((SWA_STEP, WINDOW, KV_DIM), cache)],
        out_shape=[jax.ShapeDtypeStruct((SEQ, D_MODEL), BF16), jax.ShapeDtypeStruct((N_SAMPLE, D_MODEL), BF16),
                   jax.ShapeDtypeStruct((DEC_BATCH, WINDOW, KV_DIM), F32),
                   jax.ShapeDtypeStruct((DEC_BATCH, WINDOW, KV_DIM), F32)],
        scratch_shapes=[pltpu.VMEM((n_sets, rows, 4 * WINDOW), F32),
                        pltpu.VMEM((n_sets, rows, 4 * WINDOW), BF16),
                        pltpu.VMEM((n_sets, rows, LANES), F32),
                        pltpu.VMEM((seq_rows, D_MODEL), F32)],
        compiler_params=_params("parallel"),
        name="swa",
    )(sinks, bias, sbias, q, k, k, v, v, q, k, v, cache_k, cache_v)


def _ret_proj_kernel(x_ref, w_ref, tt_ref, rt_ref, q_ref, k_ref, v_ref, g_ref):
    xb = x_ref[...].astype(BF16)
    hqk = jnp.dot(xb, w_ref[:, :2 * RET_QK], preferred_element_type=F32)
    cos, sin = _tile_cos_sin(tt_ref, rt_ref)
    lane = lax.broadcasted_iota(jnp.int32, (1, LANES), 1)
    rot = _rotate_blocks(hqk, 2 * RET_QK // LANES, cos, sin, 1, (lane % 2) == 0)
    nq = RET_QK // LANES
    for j in range(nq):
        q_ref[:, j * LANES:(j + 1) * LANES] = rot[j].astype(BF16)
        k_ref[:, j * LANES:(j + 1) * LANES] = rot[nq + j] * (RET_QK_DIM ** -0.5)
    v_ref[...] = jnp.dot(xb, w_ref[:, 2 * RET_QK:2 * RET_QK + RET_V],
                         preferred_element_type=F32).astype(BF16)
    g_ref[...] = _silu(jnp.dot(xb, w_ref[:, 2 * RET_QK + RET_V:], preferred_element_type=F32))


def _ret_proj(x, w, tables):
    n = x.shape[0]
    width = 2 * RET_QK + 2 * RET_V
    return pl.pallas_call(
        _ret_proj_kernel,
        grid=(n // ROW_TILE,),
        in_specs=[_rows(D_MODEL), _resident((D_MODEL, width)), *_table_specs(RET_QK_DIM)],
        out_specs=[_rows(RET_QK), _rows(RET_QK), _rows(RET_V), _rows(RET_V)],
        out_shape=[jax.ShapeDtypeStruct((n, RET_QK), BF16), jax.ShapeDtypeStruct((n, RET_QK), F32),
                   jax.ShapeDtypeStruct((n, RET_V), BF16), jax.ShapeDtypeStruct((n, RET_V), F32)],
        compiler_params=_params("parallel"),
        name="ret_proj",
    )(x, w, *tables)


def _xpos_inv_lane():
    inv = 1.0 / (RET_THETA ** jnp.linspace(0.0, 1.0, RET_QK_DIM // 2, dtype=F32))
    return jnp.repeat(inv, 2)


def _ret_log_decay():
    return jnp.log(1.0 - 2.0 ** (-5.0 - jnp.arange(RET_HEADS, dtype=F32)))


def _ret_tables(L):
    lg = _ret_log_decay()
    idx = jnp.arange(L, dtype=F32)
    diff = idx[:, None] - idx[None, :]
    causal = diff >= 0
    dmat = jnp.where(causal[None], jnp.exp(lg[:, None, None] * jnp.where(causal, diff, 0.0)[None]), 0.0)
    xi = jnp.exp(lg[:, None] * (idx[None, :] + 1.0))
    zeta = jnp.exp(lg[:, None] * (L - 1.0 - idx[None, :]))
    decay = jnp.exp(lg * L)
    rep = lambda a: jnp.broadcast_to(a[:, :, None], (RET_HEADS, L, LANES))
    return dmat, rep(xi), rep(zeta), decay


def _lanes(t, width):
    return jnp.concatenate([t] * (width // LANES), axis=1)


def _group_norm_gate(o, gate):
    mu = jnp.mean(o, axis=-1, keepdims=True)
    d = o - mu
    var = jnp.mean(d * d, axis=-1, keepdims=True)
    return gate * (d * lax.rsqrt(var + LN_EPS))


def _retention_head(qh, kh, vh, s_old, dmat, xi, zeta, decay):
    att = lax.dot_general(qh, kh.astype(BF16), NT_DIMS, preferred_element_type=F32) * dmat
    o = jnp.dot(att.astype(BF16), vh, preferred_element_type=F32)
    o = o + jnp.dot(qh, s_old.astype(BF16), preferred_element_type=F32) * _lanes(xi, RET_V_DIM)
    kz = (kh * _lanes(zeta, RET_QK_DIM)).astype(BF16)
    upd = lax.dot_general(kz, vh, TN_DIMS, preferred_element_type=F32)
    return o, decay * s_old + upd


RET_SAMPLE_BATCH = DEC_BATCH // (SEQ // RET_CHUNK)


def _ret_kernel(q_ref, k_ref, v_ref, g_ref, dmat_ref, xi_ref, zeta_ref, decay_ref,
                x_ref, wo_ref, lg_ref, lb_ref,
                qs_ref, ks_ref, vs_ref, gs_ref, s_in_ref, sdmat_ref, sxi_ref, szeta_ref, sdecay_ref,
                xo_ref, sfin_ref, ys_ref, s_out_ref, s_ref, ys_scr):
    i = pl.program_id(0)

    @pl.when(i == 0)
    def _():
        s_ref[...] = jnp.zeros_like(s_ref)

    L = DEC_SEQ
    qa = qs_ref[...].astype(F32)
    va = vs_ref[...].astype(F32)
    mixed = None
    for h in range(RET_HEADS):
        qk = slice(h * RET_QK_DIM, (h + 1) * RET_QK_DIM)
        vg = slice(h * RET_V_DIM, (h + 1) * RET_V_DIM)
        o, s_new = _retention_head(q_ref[:, qk], k_ref[:, qk], v_ref[:, vg], s_ref[h],
                                   dmat_ref[h], xi_ref[h], zeta_ref[h], decay_ref[h])
        s_ref[h] = s_new
        yh = _group_norm_gate(o, g_ref[:, vg]).astype(BF16)
        d = jnp.dot(yh, wo_ref[vg, :].astype(BF16), preferred_element_type=F32)
        mixed = d if mixed is None else mixed + d
        for b in range(RET_SAMPLE_BATCH):
            rows = slice(b * L, (b + 1) * L)
            o, s_new = _retention_head(qa[rows, qk].astype(BF16), ks_ref[rows, qk], va[rows, vg].astype(BF16),
                                       s_in_ref[b, h], sdmat_ref[h], sxi_ref[h], szeta_ref[h], sdecay_ref[h])
            s_out_ref[b, h] = s_new
            ys_scr[rows, vg] = _group_norm_gate(o, gs_ref[rows, vg])
    ys_ref[...] = ys_scr[...].astype(BF16)
    xo_ref[...] = _layer_norm(ALPHA * x_ref[...] + mixed, lg_ref[...], lb_ref[...])

    @pl.when(i == pl.num_programs(0) - 1)
    def _():
        sfin_ref[...] = s_ref[...]


def _ret(q, k, v, g, state, tables, sample_tables, x, w_o, ln):
    nc = SEQ // RET_CHUNK
    bb = RET_SAMPLE_BATCH
    rows = bb * DEC_SEQ
    cur = lambda i: (i, 0)
    tok = lambda i: (i + SEQ // rows, 0)
    st = lambda i: (i, 0, 0, 0)
    state_shape = (RET_HEADS, RET_QK_DIM, RET_V_DIM)
    state_block = (bb,) + state_shape
    table_specs = lambda n: [_resident((RET_HEADS, n, n)), _resident((RET_HEADS, n, LANES)),
                             _resident((RET_HEADS, n, LANES)), _SMEM]
    return pl.pallas_call(
        _ret_kernel,
        grid=(nc,),
        in_specs=[pl.BlockSpec((RET_CHUNK, RET_QK), cur), pl.BlockSpec((RET_CHUNK, RET_QK), cur),
                  pl.BlockSpec((RET_CHUNK, RET_V), cur), pl.BlockSpec((RET_CHUNK, RET_V), cur),
                  *table_specs(RET_CHUNK),
                  pl.BlockSpec((RET_CHUNK, D_MODEL), cur), _layer_weight((RET_V, D_MODEL), 0),
                  _resident((1, D_MODEL)), _resident((1, D_MODEL)),
                  pl.BlockSpec((rows, RET_QK), tok), pl.BlockSpec((rows, RET_QK), tok),
                  pl.BlockSpec((rows, RET_V), tok), pl.BlockSpec((rows, RET_V), tok),
                  pl.BlockSpec(state_block, st), *table_specs(DEC_SEQ)],
        out_specs=[pl.BlockSpec((RET_CHUNK, D_MODEL), cur), pl.BlockSpec(state_shape, lambda i: (0, 0, 0)),
                   pl.BlockSpec((rows, RET_V), cur), pl.BlockSpec(state_block, st)],
        out_shape=[jax.ShapeDtypeStruct((SEQ, D_MODEL), F32), jax.ShapeDtypeStruct(state_shape, F32),
                   jax.ShapeDtypeStruct((N_SAMPLE, RET_V), BF16),
                   jax.ShapeDtypeStruct((DEC_BATCH,) + state_shape, F32)],
        scratch_shapes=[pltpu.VMEM(state_shape, F32), pltpu.VMEM((rows, RET_V), F32)],
        compiler_params=_params("arbitrary"),
        name="ret",
    )(q, k, v, g, *tables, x, w_o, *ln, q, k, v, g, state, *sample_tables)


def kernel(x_prompt, x_sample, cache_k_win, cache_v_win, state_ret, ffn1_w_gate, ffn1_w_up, ffn1_w_down,
           ffn2_w_gate, ffn2_w_up, ffn2_w_down, ln_g, ln_b, attn_w_qkv, attn_w_o, attn_sinks, ret_w_in,
           ret_w_o):
    ln = lambda i, j: (ln_g[i, j][None, :], ln_b[i, j][None, :])
    ffn1 = (ffn1_w_gate, ffn1_w_up, ffn1_w_down)
    ffn2 = (ffn2_w_gate, ffn2_w_up, ffn2_w_down)

    x = _half_ffn((x_prompt.reshape(SEQ, D_MODEL), x_sample.reshape(N_SAMPLE, D_MODEL)), ffn1, 0, ln(0, 0))
    w_qkv = jnp.concatenate([_pair_heads(attn_w_qkv[0][:, :D_MODEL], 1), attn_w_qkv[0][:, D_MODEL:]], axis=1)
    w_o = _pair_heads(attn_w_o, 1)
    q, k, v = _qkv_proj(x, w_qkv, _angle_tables(_rope_inv_lane()))
    sinks = attn_sinks[0]
    o_p, o_s, nk_s, nv_s = _swa(q, k, v, cache_k_win[0].reshape(DEC_BATCH, WINDOW, KV_DIM),
                                cache_v_win[0].reshape(DEC_BATCH, WINDOW, KV_DIM), sinks)
    x = _mix_out((o_p, o_s), x, w_o, ln(0, 1))
    x = _half_ffn((x,), ffn2, 0, ln(0, 2))

    kv_shape = (1, 1, WINDOW, N_KV_HEADS, HEAD_DIM)
    new_k_p = k[SEQ - WINDOW:SEQ].reshape(kv_shape)
    new_v_p = v[SEQ - WINDOW:SEQ].reshape(kv_shape)
    new_k_s = nk_s.reshape(1, DEC_BATCH, WINDOW, N_KV_HEADS, HEAD_DIM)
    new_v_s = nv_s.reshape(1, DEC_BATCH, WINDOW, N_KV_HEADS, HEAD_DIM)

    x = _half_ffn((x,), ffn1, 1, ln(1, 0))
    rq, rk, rv, rg = _ret_proj(x, ret_w_in[0].astype(BF16), _angle_tables(_xpos_inv_lane()))
    x_p, s_p, y_s, s_s = _ret(rq, rk, rv, rg, state_ret[0], _ret_tables(RET_CHUNK), _ret_tables(DEC_SEQ),
                              x, ret_w_o, ln(1, 1))
    x_s = _mix_out((y_s,), x, ret_w_o, ln(1, 1))
    y_p, y_s = _half_ffn((x_p, x_s), ffn2, 1, ln(1, 2), split_out=True)

    return (y_p.reshape(1, SEQ, D_MODEL), y_s.reshape(DEC_BATCH, DEC_SEQ, D_MODEL),
            new_k_p, new_v_p, new_k_s, new_v_s, s_p[None, None], s_s[None])
```

```python
import functools
import math

import jax
import jax.numpy as jnp
import numpy as np
from jax import lax
from jax.experimental import pallas as pl
from jax.experimental.pallas import tpu as pltpu

F32 = jnp.float32
BF16 = jnp.bfloat16

D_MODEL = 1024
SEQ = 16384
DEPTH = 2
DEC_BATCH = 128
DEC_SEQ = 8
PAST_LEN = 16384
N_HEADS = 16
N_KV_HEADS = 4
GROUP = N_HEADS // N_KV_HEADS
HEAD_DIM = D_MODEL // N_HEADS
WINDOW = 128
ROPE_THETA = 10000.0
ATTN_SCALE = 1.0 / math.sqrt(HEAD_DIM)
RET_HEADS = 4
RET_QK_DIM = D_MODEL // RET_HEADS
RET_V_DIM = 2 * D_MODEL // RET_HEADS
RET_CHUNK = 256
RET_THETA = 10000.0
FFN_DIM = 2816
LN_EPS = 1e-5
ALPHA = (2.0 * DEPTH) ** 0.25

N_SAMPLE = DEC_BATCH * DEC_SEQ
N_TOK = SEQ + N_SAMPLE
KV_DIM = N_KV_HEADS * HEAD_DIM
RET_QK = RET_HEADS * RET_QK_DIM
RET_V = RET_HEADS * RET_V_DIM

LANES = 128
ROW_TILE = 512
FFN_TILE = 512
FFN_CHUNK = 256
VMEM_LIMIT = 56 * 1024 * 1024

N_TILES = N_TOK // ROW_TILE
N_PROMPT_TILES = SEQ // ROW_TILE
FFN_TILES = N_TOK // FFN_TILE
FFN_PROMPT_TILES = SEQ // FFN_TILE
KV_PAIRS = N_KV_HEADS // 2

NT_DIMS = (((1,), (1,)), ((), ()))
TN_DIMS = (((0,), (0,)), ((), ()))


def _params(*sem):
    return pltpu.CompilerParams(dimension_semantics=sem, vmem_limit_bytes=VMEM_LIMIT)


def _resident(shape):
    return pl.BlockSpec(shape, lambda *_: (0,) * len(shape), pipeline_mode=pl.Buffered(1))


def _rows(width, offset_blocks=0, tile=ROW_TILE):
    return pl.BlockSpec((tile, width), lambda i: (i + offset_blocks, 0))


_ALIASED = pl.BlockSpec(memory_space=pl.ANY)
_SMEM = pl.BlockSpec(memory_space=pltpu.SMEM)


def _layer_norm(y, g, b):
    mu = jnp.mean(y, axis=-1, keepdims=True)
    d = y - mu
    var = jnp.mean(d * d, axis=-1, keepdims=True)
    return d * lax.rsqrt(var + LN_EPS) * g + b


def _silu(x):
    return x / (1.0 + jnp.exp(-x))


def _ffn_kernel(split_in, split_out, mix, *refs):
    n_in = 2 if split_in else 1
    x_refs, refs = refs[:n_in], refs[n_in:]
    if mix:
        (yp_ref, ys_ref, wo_ref, mg_ref, mb_ref), refs = refs[:5], refs[5:]
    (wg_ref, wu_ref, wd_ref, g_ref, b_ref), o_refs = refs[:5], refs[5:]
    i = pl.program_id(0)
    x = x_refs[0][...]
    if split_in:
        x = jnp.where(i < FFN_PROMPT_TILES, x, x_refs[1][...])
    if mix:
        y = jnp.where(i < FFN_PROMPT_TILES, yp_ref[...], ys_ref[...].astype(BF16))
        m = jnp.dot(y, wo_ref[...].astype(BF16), preferred_element_type=F32)
        x = _layer_norm(ALPHA * x + m, mg_ref[...], mb_ref[...])
    xb = x.astype(BF16)
    acc = None
    for c in range(FFN_DIM // FFN_CHUNK):
        sl = slice(c * FFN_CHUNK, (c + 1) * FFN_CHUNK)
        gate = jnp.dot(xb, wg_ref[:, sl].astype(BF16), preferred_element_type=F32)
        up = jnp.dot(xb, wu_ref[:, sl].astype(BF16), preferred_element_type=F32)
        h = (_silu(gate) * up).astype(BF16)
        d = jnp.dot(h, wd_ref[sl, :].astype(BF16), preferred_element_type=F32)
        acc = d if acc is None else acc + d
    y = _layer_norm(ALPHA * x + 0.5 * acc, g_ref[...], b_ref[...])
    if split_out:
        @pl.when(i < FFN_PROMPT_TILES)
        def _():
            o_refs[0][...] = y

        @pl.when(i >= FFN_PROMPT_TILES)
        def _():
            o_refs[1][...] = y
    else:
        o_refs[0][...] = y


_PROMPT_TILE = lambda i: (jnp.minimum(i, FFN_PROMPT_TILES - 1), 0)
_SAMPLE_TILE = lambda i: (jnp.maximum(i - FFN_PROMPT_TILES, 0), 0)


def _layer_weight(shape, layer):
    return pl.BlockSpec((None,) + shape, lambda i: (layer, 0, 0), pipeline_mode=pl.Buffered(1))


def _half_ffn(xs, weights, layer, ln, *, split_out=False, mix=None):
    split_in = len(xs) == 2
    tile = (FFN_TILE, D_MODEL)
    x_specs = ([pl.BlockSpec(tile, _PROMPT_TILE), pl.BlockSpec(tile, _SAMPLE_TILE)]
               if split_in else [_rows(D_MODEL, tile=FFN_TILE)])
    mix_args = ()
    if mix is not None:
        ys, w_o, mix_ln = mix
        k = w_o.shape[1]
        x_specs += [pl.BlockSpec((FFN_TILE, k), _PROMPT_TILE), pl.BlockSpec((FFN_TILE, k), _SAMPLE_TILE),
                    _layer_weight((k, D_MODEL), 0), _resident((1, D_MODEL)), _resident((1, D_MODEL))]
        mix_args = (*ys, w_o, *mix_ln)
    if split_out:
        out_specs = [pl.BlockSpec(tile, _PROMPT_TILE), pl.BlockSpec(tile, _SAMPLE_TILE)]
        out_shape = [jax.ShapeDtypeStruct((SEQ, D_MODEL), F32), jax.ShapeDtypeStruct((N_SAMPLE, D_MODEL), F32)]
    else:
        out_specs = _rows(D_MODEL, tile=FFN_TILE)
        out_shape = jax.ShapeDtypeStruct((N_TOK, D_MODEL), F32)
    return pl.pallas_call(
        functools.partial(_ffn_kernel, split_in, split_out, mix is not None),
        grid=(FFN_TILES,),
        in_specs=x_specs + [_layer_weight((D_MODEL, FFN_DIM), layer), _layer_weight((D_MODEL, FFN_DIM), layer),
                            _layer_weight((FFN_DIM, D_MODEL), layer),
                            _resident((1, D_MODEL)), _resident((1, D_MODEL))],
        out_specs=out_specs,
        out_shape=out_shape,
        compiler_params=_params("arbitrary" if split_out else "parallel"),
        name="half_ffn",
    )(*xs, *mix_args, *weights, *ln)


def _mix_out_kernel(split, *refs):
    n_y = 2 if split else 1
    y_refs, (x_ref, w_ref, g_ref, b_ref, o_ref) = refs[:n_y], refs[n_y:]
    y = y_refs[-1][...].astype(BF16)
    if split:
        y = jnp.where(pl.program_id(0) < N_PROMPT_TILES, y_refs[0][...], y)
    m = jnp.dot(y, w_ref[...].astype(BF16), preferred_element_type=F32)
    o_ref[...] = _layer_norm(ALPHA * x_ref[...] + m, g_ref[...], b_ref[...])


def _mix_out(ys, x, w, ln):
    split = len(ys) == 2
    k = ys[0].shape[1]
    if split:
        y_specs = [pl.BlockSpec((ROW_TILE, k), lambda i: (jnp.minimum(i, N_PROMPT_TILES - 1), 0)),
                   pl.BlockSpec((ROW_TILE, k), lambda i: (jnp.maximum(i - N_PROMPT_TILES, 0), 0))]
        n_tiles, x_off = N_TILES, 0
    else:
        y_specs = [_rows(k)]
        n_tiles, x_off = N_TILES - N_PROMPT_TILES, N_PROMPT_TILES
    return pl.pallas_call(
        functools.partial(_mix_out_kernel, split),
        grid=(n_tiles,),
        in_specs=y_specs + [_rows(D_MODEL, x_off), _layer_weight((k, D_MODEL), 0),
                            _resident((1, D_MODEL)), _resident((1, D_MODEL))],
        out_specs=_rows(D_MODEL),
        out_shape=jax.ShapeDtypeStruct((n_tiles * ROW_TILE, D_MODEL), F32),
        compiler_params=_params("parallel"),
        name="mix_out",
    )(*ys, x, w, *ln)


def _angle_tables(inv_lane):
    base = jnp.concatenate([jnp.arange(N_PROMPT_TILES) * ROW_TILE,
                            jnp.full((N_TILES - N_PROMPT_TILES,), PAST_LEN)]).astype(F32)
    row = jnp.stack([jnp.arange(ROW_TILE), jnp.arange(ROW_TILE) % DEC_SEQ]).astype(F32)
    ta = base[:, None] * inv_lane[None, :]
    ra = row[:, :, None] * inv_lane[None, None, :]
    return jnp.stack([jnp.cos(ta), jnp.sin(ta)], axis=1), jnp.stack([jnp.cos(ra), jnp.sin(ra)], axis=1)


def _table_specs(width):
    return [pl.BlockSpec((1, 2, width), lambda i: (i, 0, 0)),
            pl.BlockSpec((1, 2, ROW_TILE, width), lambda i: (i // N_PROMPT_TILES, 0, 0, 0))]


def _tile_cos_sin(tt_ref, rt_ref):
    cb, sb = tt_ref[0, 0:1, :], tt_ref[0, 1:2, :]
    cr, sr = rt_ref[0, 0], rt_ref[0, 1]
    return cb * cr - sb * sr, sb * cr + cb * sr


def _rotate_blocks(h, n_blocks, cos, sin, shift, partner_above):
    width = cos.shape[1]
    sign = jnp.where(partner_above, -1.0, 1.0).astype(F32)
    sin = sin * jnp.concatenate([sign] * (width // LANES), axis=1)
    out = []
    for j in range(n_blocks):
        t = (j * LANES) % width
        blk = h[:, j * LANES:(j + 1) * LANES]
        partner = jnp.where(partner_above, pltpu.roll(blk, LANES - shift, 1), pltpu.roll(blk, shift, 1))
        out.append(blk * cos[:, t:t + LANES] + partner * sin[:, t:t + LANES])
    return out


def _qkv_kernel(x_ref, w_ref, tt_ref, rt_ref, q_ref, k_ref, v_ref):
    xb = x_ref[...].astype(BF16)
    h = jnp.dot(xb, w_ref[...].astype(BF16), preferred_element_type=F32)
    cos, sin = _tile_cos_sin(tt_ref, rt_ref)
    lane = lax.broadcasted_iota(jnp.int32, (1, LANES), 1)
    n_rot = (D_MODEL + KV_DIM) // LANES
    rot = _rotate_blocks(h, n_rot, cos, sin, HEAD_DIM // 2, (lane % HEAD_DIM) < HEAD_DIM // 2)
    nq = D_MODEL // LANES
    for j in range(nq):
        q_ref[:, j * LANES:(j + 1) * LANES] = (rot[j] * ATTN_SCALE).astype(BF16)
    for j in range(nq, n_rot):
        k_ref[:, (j - nq) * LANES:(j - nq + 1) * LANES] = rot[j]
    v_ref[...] = h[:, D_MODEL + KV_DIM:]


def _qkv_proj(x, w, tables):
    n = x.shape[0]
    width = D_MODEL + 2 * KV_DIM
    return pl.pallas_call(
        _qkv_kernel,
        grid=(n // ROW_TILE,),
        in_specs=[_rows(D_MODEL), _resident((D_MODEL, width)), *_table_specs(LANES)],
        out_specs=[_rows(D_MODEL), _rows(KV_DIM), _rows(KV_DIM)],
        out_shape=[jax.ShapeDtypeStruct((n, D_MODEL), BF16), jax.ShapeDtypeStruct((n, KV_DIM), F32),
                   jax.ShapeDtypeStruct((n, KV_DIM), F32)],
        compiler_params=_params("parallel"),
        name="attn_qkv",
    )(x, w, *tables)


def _rope_inv_lane():
    inv = ROPE_THETA ** (-jnp.arange(0, HEAD_DIM, 2, dtype=F32) / HEAD_DIM)
    return jnp.tile(inv, LANES // (HEAD_DIM // 2))


def _pair_heads(w, axis):
    shape = w.shape
    w = w.reshape(shape[:axis] + (KV_PAIRS, 2, GROUP, HEAD_DIM) + shape[axis + 1:])
    return jnp.swapaxes(w, axis + 1, axis + 2).reshape(shape)


def _pair_operands(kall, vall):
    s = kall.shape[0]
    low = lax.broadcasted_iota(jnp.int32, (s, LANES), 1) < HEAD_DIM
    kb, vb = kall.astype(BF16), vall.astype(BF16)
    zero = jnp.zeros_like(kb)
    one_lo = jnp.where(low, 1.0, 0.0).astype(BF16)
    one_hi = jnp.where(low, 0.0, 1.0).astype(BF16)
    kcat = jnp.concatenate([jnp.where(low, kb, zero), jnp.where(low, zero, kb)], axis=0)
    vcat = jnp.concatenate([jnp.concatenate([jnp.where(low, vb, zero), one_lo], axis=1),
                            jnp.concatenate([jnp.where(low, zero, vb), one_hi], axis=1)], axis=0)
    return kcat, vcat


def _pair_softmax(s, sink_a, sink_b):
    half = s.shape[1] // 2
    sa, sb = s[:, :half], s[:, half:]
    ma = jnp.maximum(jnp.max(sa, axis=-1, keepdims=True), sink_a)
    mb = jnp.maximum(jnp.max(sb, axis=-1, keepdims=True), sink_b)
    e = jnp.concatenate([jnp.exp(sa - ma), jnp.exp(sb - mb)], axis=1).astype(BF16)
    low = lax.broadcasted_iota(jnp.int32, (s.shape[0], LANES), 1) < HEAD_DIM
    return e, jnp.where(low, jnp.exp(sink_a - ma), jnp.exp(sink_b - mb))


SWA_SAMPLE_KEYS = 2 * WINDOW


SWA_STEP = 2


def _swa_kernel(sinks_ref, bias_ref, sbias_ref, q_ref, kp_ref, kc_ref, vp_ref, vc_ref,
                qs_ref, kn_ref, vn_ref, ck_ref, cv_ref,
                o_ref, os_ref, nk_ref, nv_ref, s_ref, e_ref, t_ref, os_scr):
    first = jnp.minimum(pl.program_id(0), 1)
    qall = qs_ref[...].astype(F32)
    for j in range(SWA_STEP):
        seq_j = slice(j * DEC_SEQ, (j + 1) * DEC_SEQ)
        _swa_sample_seq(sinks_ref, sbias_ref[...], qall[seq_j], kn_ref[seq_j, :], vn_ref[seq_j, :],
                        ck_ref[j], cv_ref[j], os_scr.at[seq_j], nk_ref.at[j], nv_ref.at[j])
        rows_j = slice(j * WINDOW, (j + 1) * WINDOW)
        bias = bias_ref[first] if j == 0 else bias_ref[1]
        for p in range(KV_PAIRS):
            lanes = slice(p * LANES, (p + 1) * LANES)
            k_prev = kp_ref[:, lanes] if j == 0 else kc_ref[(j - 1) * WINDOW:j * WINDOW, lanes]
            v_prev = vp_ref[:, lanes] if j == 0 else vc_ref[(j - 1) * WINDOW:j * WINDOW, lanes]
            kcat, vcat = _pair_operands(jnp.concatenate([k_prev, kc_ref[rows_j, lanes]], axis=0),
                                        jnp.concatenate([v_prev, vc_ref[rows_j, lanes]], axis=0))
            q = q_ref[rows_j, p * GROUP * LANES:(p + 1) * GROUP * LANES]
            qs = jnp.concatenate([q[:, g * LANES:(g + 1) * LANES] for g in range(GROUP)], axis=0)
            c = j * KV_PAIRS + p
            s_ref[c] = lax.dot_general(qs, kcat, NT_DIMS, preferred_element_type=F32)
            for g in range(GROUP):
                rows = slice(g * WINDOW, (g + 1) * WINDOW)
                head_a = (2 * p) * GROUP + g
                e, t = _pair_softmax(s_ref[c, rows, :] + bias, sinks_ref[head_a], sinks_ref[head_a + GROUP])
                e_ref[c, rows, :] = e
                t_ref[c, rows, :] = t
            r = jnp.dot(e_ref[c], vcat, preferred_element_type=F32)
            o = r[:, :LANES] / (r[:, LANES:] + t_ref[c])
            for g in range(GROUP):
                blk = p * GROUP + g
                o_ref[rows_j, blk * LANES:(blk + 1) * LANES] = o[g * WINDOW:(g + 1) * WINDOW].astype(BF16)
    os_ref[...] = os_scr[...].astype(BF16)


def _window_bias(n_rows, row_of, key_dist_and_pos):
    dist, kpos = key_dist_and_pos(row_of(np.arange(n_rows))[:, None])
    valid = (dist >= 0) & (dist < WINDOW) & (kpos >= 0)
    m = np.where(valid, 0.0, -np.inf).astype(np.float32)
    return np.concatenate([m, m], axis=1)


def _swa_sample_seq(sinks_ref, bias, qb, kn, vn, kc, vc, o_ref, nk_ref, nv_ref):
    L = DEC_SEQ
    pad = jnp.zeros((SWA_SAMPLE_KEYS - WINDOW - L, LANES), F32)
    nk_ref[0:WINDOW - L, :] = kc[L:, :]
    nk_ref[WINDOW - L:WINDOW, :] = kn
    nv_ref[0:WINDOW - L, :] = vc[L:, :]
    nv_ref[WINDOW - L:WINDOW, :] = vn
    for p in range(KV_PAIRS):
        lanes = slice(p * LANES, (p + 1) * LANES)
        kcat, vcat = _pair_operands(jnp.concatenate([kc[:, lanes], kn[:, lanes], pad], axis=0),
                                    jnp.concatenate([vc[:, lanes], vn[:, lanes], pad], axis=0))
        qs = jnp.concatenate(
            [qb[:, (p * GROUP + g) * LANES:(p * GROUP + g + 1) * LANES] for g in range(GROUP)],
            axis=0).astype(BF16)
        sink_col = lambda first: jnp.concatenate(
            [jnp.full((L, 1), sinks_ref[first + g], F32) for g in range(GROUP)], axis=0)
        s = lax.dot_general(qs, kcat, NT_DIMS, preferred_element_type=F32) + bias
        e, t = _pair_softmax(s, sink_col(2 * p * GROUP), sink_col((2 * p + 1) * GROUP))
        r = jnp.dot(e, vcat, preferred_element_type=F32)
        o = r[:, :LANES] / (r[:, LANES:] + t)
        for g in range(GROUP):
            blk = p * GROUP + g
            o_ref[:, blk * LANES:(blk + 1) * LANES] = o[g * L:(g + 1) * L]


def _swa(q, k, v, cache_k, cache_v, sinks):
    assert SEQ // WINDOW == DEC_BATCH
    rows = GROUP * WINDOW
    blk_rows, seq_rows = SWA_STEP * WINDOW, SWA_STEP * DEC_SEQ
    si = np.arange(2 * WINDOW)[None, :]
    bias = jnp.asarray(np.stack([
        _window_bias(WINDOW, lambda r: r, lambda qi: (qi + WINDOW - si, blk * WINDOW - WINDOW + si))
        for blk in (0, 1)]))
    cj = np.arange(SWA_SAMPLE_KEYS)[None, :]
    sbias = jnp.asarray(_window_bias(
        GROUP * DEC_SEQ, lambda r: r % DEC_SEQ,
        lambda qi: (np.where(cj < WINDOW + DEC_SEQ, PAST_LEN + qi - (PAST_LEN - WINDOW + cj), -1),
                    PAST_LEN - WINDOW + cj)))
    cur = lambda i: (i, 0)
    prev = lambda i: (jnp.maximum(SWA_STEP * i - 1, 0), 0)
    seq = lambda i: (i + SEQ // seq_rows, 0)
    cache = lambda i: (i, 0, 0)
    n_sets = SWA_STEP * KV_PAIRS
    return pl.pallas_call(
        _swa_kernel,
        grid=(DEC_BATCH // SWA_STEP,),
        in_specs=[_SMEM, _resident((2, WINDOW, 4 * WINDOW)),
                  _resident((GROUP * DEC_SEQ, 2 * SWA_SAMPLE_KEYS)),
                  pl.BlockSpec((blk_rows, D_MODEL), cur),
                  pl.BlockSpec((WINDOW, KV_DIM), prev), pl.BlockSpec((blk_rows, KV_DIM), cur),
                  pl.BlockSpec((WINDOW, KV_DIM), prev), pl.BlockSpec((blk_rows, KV_DIM), cur),
                  pl.BlockSpec((seq_rows, D_MODEL), seq),
                  pl.BlockSpec((seq_rows, KV_DIM), seq), pl.BlockSpec((seq_rows, KV_DIM), seq),
                  pl.BlockSpec((SWA_STEP, WINDOW, KV_DIM), cache), pl.BlockSpec((SWA_STEP, WINDOW, KV_DIM), cache)],
        out_specs=[pl.BlockSpec((blk_rows, D_MODEL), cur), pl.BlockSpec((seq_rows, D_MODEL), cur),
                   pl.BlockSpec((SWA_STEP, WINDOW, KV_DIM), cache), pl.BlockSpec((SWA_STEP, WINDOW, KV_DIM), cache)],
        out_shape=[jax.ShapeDtypeStruct((SEQ, D_MODEL), BF16), jax.ShapeDtypeStruct((N_SAMPLE, D_MODEL), BF16),
                   jax.ShapeDtypeStruct((DEC_BATCH, WINDOW, KV_DIM), F32),
                   jax.ShapeDtypeStruct((DEC_BATCH, WINDOW, KV_DIM), F32)],
        scratch_shapes=[pltpu.VMEM((n_sets, rows, 4 * WINDOW), F32),
                        pltpu.VMEM((n_sets, rows, 4 * WINDOW), BF16),
                        pltpu.VMEM((n_sets, rows, LANES), F32),
                        pltpu.VMEM((seq_rows, D_MODEL), F32)],
        compiler_params=_params("parallel"),
        name="swa",
    )(sinks, bias, sbias, q, k, k, v, v, q, k, v, cache_k, cache_v)


def _ret_proj_kernel(x_ref, w_ref, tt_ref, rt_ref, q_ref, k_ref, v_ref, g_ref):
    xb = x_ref[...].astype(BF16)
    hqk = jnp.dot(xb, w_ref[:, :2 * RET_QK], preferred_element_type=F32)
    cos, sin = _tile_cos_sin(tt_ref, rt_ref)
    lane = lax.broadcasted_iota(jnp.int32, (1, LANES), 1)
    rot = _rotate_blocks(hqk, 2 * RET_QK // LANES, cos, sin, 1, (lane % 2) == 0)
    nq = RET_QK // LANES
    for j in range(nq):
        q_ref[:, j * LANES:(j + 1) * LANES] = rot[j].astype(BF16)
        k_ref[:, j * LANES:(j + 1) * LANES] = rot[nq + j] * (RET_QK_DIM ** -0.5)
    v_ref[...] = jnp.dot(xb, w_ref[:, 2 * RET_QK:2 * RET_QK + RET_V],
                         preferred_element_type=F32).astype(BF16)
    g_ref[...] = _silu(jnp.dot(xb, w_ref[:, 2 * RET_QK + RET_V:], preferred_element_type=F32))


def _ret_proj(x, w, tables):
    n = x.shape[0]
    width = 2 * RET_QK + 2 * RET_V
    return pl.pallas_call(
        _ret_proj_kernel,
        grid=(n // ROW_TILE,),
        in_specs=[_rows(D_MODEL), _resident((D_MODEL, width)), *_table_specs(RET_QK_DIM)],
        out_specs=[_rows(RET_QK), _rows(RET_QK), _rows(RET_V), _rows(RET_V)],
        out_shape=[jax.ShapeDtypeStruct((n, RET_QK), BF16), jax.ShapeDtypeStruct((n, RET_QK), F32),
                   jax.ShapeDtypeStruct((n, RET_V), BF16), jax.ShapeDtypeStruct((n, RET_V), F32)],
        compiler_params=_params("parallel"),
        name="ret_proj",
    )(x, w, *tables)


def _xpos_inv_lane():
    inv = 1.0 / (RET_THETA ** jnp.linspace(0.0, 1.0, RET_QK_DIM // 2, dtype=F32))
    return jnp.repeat(inv, 2)


def _ret_log_decay():
    return jnp.log(1.0 - 2.0 ** (-5.0 - jnp.arange(RET_HEADS, dtype=F32)))


def _ret_tables(L):
    lg = _ret_log_decay()
    idx = jnp.arange(L, dtype=F32)
    diff = idx[:, None] - idx[None, :]
    causal = diff >= 0
    dmat = jnp.where(causal[None], jnp.exp(lg[:, None, None] * jnp.where(causal, diff, 0.0)[None]), 0.0)
    xi = jnp.exp(lg[:, None] * (idx[None, :] + 1.0))
    zeta = jnp.exp(lg[:, None] * (L - 1.0 - idx[None, :]))
    decay = jnp.exp(lg * L)
    rep = lambda a: jnp.broadcast_to(a[:, :, None], (RET_HEADS, L, LANES))
    return dmat, rep(xi), rep(zeta), decay


def _lanes(t, width):
    return jnp.concatenate([t] * (width // LANES), axis=1)


def _group_norm_gate(o, gate):
    mu = jnp.mean(o, axis=-1, keepdims=True)
    d = o - mu
    var = jnp.mean(d * d, axis=-1, keepdims=True)
    return gate * (d * lax.rsqrt(var + LN_EPS))


def _retention_head(qh, kh, vh, s_old, dmat, xi, zeta, decay):
    att = lax.dot_general(qh, kh.astype(BF16), NT_DIMS, preferred_element_type=F32) * dmat
    o = jnp.dot(att.astype(BF16), vh, preferred_element_type=F32)
    o = o + jnp.dot(qh, s_old.astype(BF16), preferred_element_type=F32) * _lanes(xi, RET_V_DIM)
    kz = (kh * _lanes(zeta, RET_QK_DIM)).astype(BF16)
    upd = lax.dot_general(kz, vh, TN_DIMS, preferred_element_type=F32)
    return o, decay * s_old + upd


RET_SAMPLE_BATCH = DEC_BATCH // (SEQ // RET_CHUNK)


def _ret_kernel(q_ref, k_ref, v_ref, g_ref, dmat_ref, xi_ref, zeta_ref, decay_ref,
                x_ref, wo_ref, lg_ref, lb_ref,
                qs_ref, ks_ref, vs_ref, gs_ref, s_in_ref, sdmat_ref, sxi_ref, szeta_ref, sdecay_ref,
                xo_ref, sfin_ref, ys_ref, s_out_ref, s_ref, ys_scr):
    i = pl.program_id(0)

    @pl.when(i == 0)
    def _():
        s_ref[...] = jnp.zeros_like(s_ref)

    L = DEC_SEQ
    qa = qs_ref[...].astype(F32)
    va = vs_ref[...].astype(F32)
    mixed = None
    for h in range(RET_HEADS):
        qk = slice(h * RET_QK_DIM, (h + 1) * RET_QK_DIM)
        vg = slice(h * RET_V_DIM, (h + 1) * RET_V_DIM)
        o, s_new = _retention_head(q_ref[:, qk], k_ref[:, qk], v_ref[:, vg], s_ref[h],
                                   dmat_ref[h], xi_ref[h], zeta_ref[h], decay_ref[h])
        s_ref[h] = s_new
        yh = _group_norm_gate(o, g_ref[:, vg]).astype(BF16)
        d = jnp.dot(yh, wo_ref[vg, :].astype(BF16), preferred_element_type=F32)
        mixed = d if mixed is None else mixed + d
        for b in range(RET_SAMPLE_BATCH):
            rows = slice(b * L, (b + 1) * L)
            o, s_new = _retention_head(qa[rows, qk].astype(BF16), ks_ref[rows, qk], va[rows, vg].astype(BF16),
                                       s_in_ref[b, h], sdmat_ref[h], sxi_ref[h], szeta_ref[h], sdecay_ref[h])
            s_out_ref[b, h] = s_new
            ys_scr[rows, vg] = _group_norm_gate(o, gs_ref[rows, vg])
    ys_ref[...] = ys_scr[...].astype(BF16)
    xo_ref[...] = _layer_norm(ALPHA * x_ref[...] + mixed, lg_ref[...], lb_ref[...])

    @pl.when(i == pl.num_programs(0) - 1)
    def _():
        sfin_ref[...] = s_ref[...]


def _ret(q, k, v, g, state, tables, sample_tables, x, w_o, ln):
    nc = SEQ // RET_CHUNK
    bb = RET_SAMPLE_BATCH
    rows = bb * DEC_SEQ
    cur = lambda i: (i, 0)
    tok = lambda i: (i + SEQ // rows, 0)
    st = lambda i: (i, 0, 0, 0)
    state_shape = (RET_HEADS, RET_QK_DIM, RET_V_DIM)
    state_block = (bb,) + state_shape
    table_specs = lambda n: [_resident((RET_HEADS, n, n)), _resident((RET_HEADS, n, LANES)),
                             _resident((RET_HEADS, n, LANES)), _SMEM]
    return pl.pallas_call(
        _ret_kernel,
        grid=(nc,),
        in_specs=[pl.BlockSpec((RET_CHUNK, RET_QK), cur), pl.BlockSpec((RET_CHUNK, RET_QK), cur),
                  pl.BlockSpec((RET_CHUNK, RET_V), cur), pl.BlockSpec((RET_CHUNK, RET_V), cur),
                  *table_specs(RET_CHUNK),
                  pl.BlockSpec((RET_CHUNK, D_MODEL), cur), _layer_weight((RET_V, D_MODEL), 0),
                  _resident((1, D_MODEL)), _resident((1, D_MODEL)),
                  pl.BlockSpec((rows, RET_QK), tok), pl.BlockSpec((rows, RET_QK), tok),
                  pl.BlockSpec((rows, RET_V), tok), pl.BlockSpec((rows, RET_V), tok),
                  pl.BlockSpec(state_block, st), *table_specs(DEC_SEQ)],
        out_specs=[pl.BlockSpec((RET_CHUNK, D_MODEL), cur), pl.BlockSpec(state_shape, lambda i: (0, 0, 0)),
                   pl.BlockSpec((rows, RET_V), cur), pl.BlockSpec(state_block, st)],
        out_shape=[jax.ShapeDtypeStruct((SEQ, D_MODEL), F32), jax.ShapeDtypeStruct(state_shape, F32),
                   jax.ShapeDtypeStruct((N_SAMPLE, RET_V), BF16),
                   jax.ShapeDtypeStruct((DEC_BATCH,) + state_shape, F32)],
        scratch_shapes=[pltpu.VMEM(state_shape, F32), pltpu.VMEM((rows, RET_V), F32)],
        compiler_params=_params("arbitrary"),
        name="ret",
    )(q, k, v, g, *tables, x, w_o, *ln, q, k, v, g, state, *sample_tables)


def kernel(x_prompt, x_sample, cache_k_win, cache_v_win, state_ret, ffn1_w_gate, ffn1_w_up, ffn1_w_down,
           ffn2_w_gate, ffn2_w_up, ffn2_w_down, ln_g, ln_b, attn_w_qkv, attn_w_o, attn_sinks, ret_w_in,
           ret_w_o):
    ln = lambda i, j: (ln_g[i, j][None, :], ln_b[i, j][None, :])
    ffn1 = (ffn1_w_gate, ffn1_w_up, ffn1_w_down)
    ffn2 = (ffn2_w_gate, ffn2_w_up, ffn2_w_down)

    x = _half_ffn((x_prompt.reshape(SEQ, D_MODEL), x_sample.reshape(N_SAMPLE, D_MODEL)), ffn1, 0, ln(0, 0))
    w_qkv = jnp.concatenate([_pair_heads(attn_w_qkv[0][:, :D_MODEL], 1), attn_w_qkv[0][:, D_MODEL:]],
                            axis=1).astype(BF16)
    w_o = _pair_heads(attn_w_o, 1).astype(BF16)
    q, k, v = _qkv_proj(x, w_qkv, _angle_tables(_rope_inv_lane()))
    sinks = attn_sinks[0]
    o_p, o_s, nk_s, nv_s = _swa(q, k, v, cache_k_win[0].reshape(DEC_BATCH, WINDOW, KV_DIM),
                                cache_v_win[0].reshape(DEC_BATCH, WINDOW, KV_DIM), sinks)
    x = _half_ffn((x,), ffn2, 0, ln(0, 2), mix=((o_p, o_s), w_o, ln(0, 1)))

    kv_shape = (1, 1, WINDOW, N_KV_HEADS, HEAD_DIM)
    new_k_p = k[SEQ - WINDOW:SEQ].reshape(kv_shape)
    new_v_p = v[SEQ - WINDOW:SEQ].reshape(kv_shape)
    new_k_s = nk_s.reshape(1, DEC_BATCH, WINDOW, N_KV_HEADS, HEAD_DIM)
    new_v_s = nv_s.reshape(1, DEC_BATCH, WINDOW, N_KV_HEADS, HEAD_DIM)

    x = _half_ffn((x,), ffn1, 1, ln(1, 0))
    rq, rk, rv, rg = _ret_proj(x, ret_w_in[0].astype(BF16), _angle_tables(_xpos_inv_lane()))
    x_p, s_p, y_s, s_s = _ret(rq, rk, rv, rg, state_ret[0], _ret_tables(RET_CHUNK), _ret_tables(DEC_SEQ),
                              x, ret_w_o, ln(1, 1))
    x_s = _mix_out((y_s,), x, ret_w_o, ln(1, 1))
    y_p, y_s = _half_ffn((x_p, x_s), ffn2, 1, ln(1, 2), split_out=True)

    return (y_p.reshape(1, SEQ, D_MODEL), y_s.reshape(DEC_BATCH, DEC_SEQ, D_MODEL),
            new_k_p, new_v_p, new_k_s, new_v_s, s_p[None, None], s_s[None])
```

```python
import functools
import math

import jax
import jax.numpy as jnp
import numpy as np
from jax import lax
from jax.experimental import pallas as pl
from jax.experimental.pallas import tpu as pltpu

F32 = jnp.float32
BF16 = jnp.bfloat16

D_MODEL = 1024
SEQ = 16384
DEPTH = 2
DEC_BATCH = 128
DEC_SEQ = 8
PAST_LEN = 16384
N_HEADS = 16
N_KV_HEADS = 4
GROUP = N_HEADS // N_KV_HEADS
HEAD_DIM = D_MODEL // N_HEADS
WINDOW = 128
ROPE_THETA = 10000.0
ATTN_SCALE = 1.0 / math.sqrt(HEAD_DIM)
RET_HEADS = 4
RET_QK_DIM = D_MODEL // RET_HEADS
RET_V_DIM = 2 * D_MODEL // RET_HEADS
RET_CHUNK = 256
RET_THETA = 10000.0
FFN_DIM = 2816
LN_EPS = 1e-5
ALPHA = (2.0 * DEPTH) ** 0.25

N_SAMPLE = DEC_BATCH * DEC_SEQ
N_TOK = SEQ + N_SAMPLE
KV_DIM = N_KV_HEADS * HEAD_DIM
RET_QK = RET_HEADS * RET_QK_DIM
RET_V = RET_HEADS * RET_V_DIM

LANES = 128
ROW_TILE = 512
FFN_TILE = 512
FFN_CHUNK = 256
VMEM_LIMIT = 56 * 1024 * 1024

N_TILES = N_TOK // ROW_TILE
N_PROMPT_TILES = SEQ // ROW_TILE
FFN_TILES = N_TOK // FFN_TILE
FFN_PROMPT_TILES = SEQ // FFN_TILE
KV_PAIRS = N_KV_HEADS // 2

NT_DIMS = (((1,), (1,)), ((), ()))
TN_DIMS = (((0,), (0,)), ((), ()))


def _params(*sem):
    return pltpu.CompilerParams(dimension_semantics=sem, vmem_limit_bytes=VMEM_LIMIT)


def _resident(shape):
    return pl.BlockSpec(shape, lambda *_: (0,) * len(shape), pipeline_mode=pl.Buffered(1))


def _rows(width, offset_blocks=0, tile=ROW_TILE):
    return pl.BlockSpec((tile, width), lambda i: (i + offset_blocks, 0))


_ALIASED = pl.BlockSpec(memory_space=pl.ANY)
_SMEM = pl.BlockSpec(memory_space=pltpu.SMEM)


def _layer_norm(y, g, b):
    mu = jnp.mean(y, axis=-1, keepdims=True)
    d = y - mu
    var = jnp.mean(d * d, axis=-1, keepdims=True)
    return d * lax.rsqrt(var + LN_EPS) * g + b


def _silu(x):
    return x / (1.0 + jnp.exp(-x))


def _ffn_kernel(split_in, split_out, mix, *refs):
    n_in = 2 if split_in else 1
    x_refs, refs = refs[:n_in], refs[n_in:]
    if mix:
        (yp_ref, ys_ref, wo_ref, mg_ref, mb_ref), refs = refs[:5], refs[5:]
    (wg_ref, wu_ref, wd_ref, g_ref, b_ref), o_refs = refs[:5], refs[5:]
    i = pl.program_id(0)
    x = x_refs[0][...]
    if split_in:
        x = jnp.where(i < FFN_PROMPT_TILES, x, x_refs[1][...])
    if mix:
        y = jnp.where(i < FFN_PROMPT_TILES, yp_ref[...], ys_ref[...].astype(BF16))
        m = jnp.dot(y, wo_ref[...].astype(BF16), preferred_element_type=F32)
        x = _layer_norm(ALPHA * x + m, mg_ref[...], mb_ref[...])
    xb = x.astype(BF16)
    acc = None
    for c in range(FFN_DIM // FFN_CHUNK):
        sl = slice(c * FFN_CHUNK, (c + 1) * FFN_CHUNK)
        gate = jnp.dot(xb, wg_ref[:, sl].astype(BF16), preferred_element_type=F32)
        up = jnp.dot(xb, wu_ref[:, sl].astype(BF16), preferred_element_type=F32)
        h = (_silu(gate) * up).astype(BF16)
        d = jnp.dot(h, wd_ref[sl, :].astype(BF16), preferred_element_type=F32)
        acc = d if acc is None else acc + d
    y = _layer_norm(ALPHA * x + 0.5 * acc, g_ref[...], b_ref[...])
    if split_out:
        @pl.when(i < FFN_PROMPT_TILES)
        def _():
            o_refs[0][...] = y

        @pl.when(i >= FFN_PROMPT_TILES)
        def _():
            o_refs[1][...] = y
    else:
        o_refs[0][...] = y


_PROMPT_TILE = lambda i: (jnp.minimum(i, FFN_PROMPT_TILES - 1), 0)
_SAMPLE_TILE = lambda i: (jnp.maximum(i - FFN_PROMPT_TILES, 0), 0)


def _layer_weight(shape, layer):
    return pl.BlockSpec((None,) + shape, lambda i: (layer, 0, 0), pipeline_mode=pl.Buffered(1))


def _half_ffn(xs, weights, layer, ln, *, split_out=False, mix=None):
    split_in = len(xs) == 2
    tile = (FFN_TILE, D_MODEL)
    x_specs = ([pl.BlockSpec(tile, _PROMPT_TILE), pl.BlockSpec(tile, _SAMPLE_TILE)]
               if split_in else [_rows(D_MODEL, tile=FFN_TILE)])
    mix_args = ()
    if mix is not None:
        ys, w_o, mix_ln = mix
        k = w_o.shape[1]
        x_specs += [pl.BlockSpec((FFN_TILE, k), _PROMPT_TILE), pl.BlockSpec((FFN_TILE, k), _SAMPLE_TILE),
                    _layer_weight((k, D_MODEL), 0), _resident((1, D_MODEL)), _resident((1, D_MODEL))]
        mix_args = (*ys, w_o, *mix_ln)
    if split_out:
        out_specs = [pl.BlockSpec(tile, _PROMPT_TILE), pl.BlockSpec(tile, _SAMPLE_TILE)]
        out_shape = [jax.ShapeDtypeStruct((SEQ, D_MODEL), F32), jax.ShapeDtypeStruct((N_SAMPLE, D_MODEL), F32)]
    else:
        out_specs = _rows(D_MODEL, tile=FFN_TILE)
        out_shape = jax.ShapeDtypeStruct((N_TOK, D_MODEL), F32)
    return pl.pallas_call(
        functools.partial(_ffn_kernel, split_in, split_out, mix is not None),
        grid=(FFN_TILES,),
        in_specs=x_specs + [_layer_weight((D_MODEL, FFN_DIM), layer), _layer_weight((D_MODEL, FFN_DIM), layer),
                            _layer_weight((FFN_DIM, D_MODEL), layer),
                            _resident((1, D_MODEL)), _resident((1, D_MODEL))],
        out_specs=out_specs,
        out_shape=out_shape,
        compiler_params=_params("arbitrary" if split_out else "parallel"),
        name="half_ffn",
    )(*xs, *mix_args, *weights, *ln)


def _mix_out_kernel(split, *refs):
    n_y = 2 if split else 1
    y_refs, (x_ref, w_ref, g_ref, b_ref, o_ref) = refs[:n_y], refs[n_y:]
    y = y_refs[-1][...].astype(BF16)
    if split:
        y = jnp.where(pl.program_id(0) < N_PROMPT_TILES, y_refs[0][...], y)
    m = jnp.dot(y, w_ref[...].astype(BF16), preferred_element_type=F32)
    o_ref[...] = _layer_norm(ALPHA * x_ref[...] + m, g_ref[...], b_ref[...])


def _mix_out(ys, x, w, ln):
    split = len(ys) == 2
    k = ys[0].shape[1]
    if split:
        y_specs = [pl.BlockSpec((ROW_TILE, k), lambda i: (jnp.minimum(i, N_PROMPT_TILES - 1), 0)),
                   pl.BlockSpec((ROW_TILE, k), lambda i: (jnp.maximum(i - N_PROMPT_TILES, 0), 0))]
        n_tiles, x_off = N_TILES, 0
    else:
        y_specs = [_rows(k)]
        n_tiles, x_off = N_TILES - N_PROMPT_TILES, N_PROMPT_TILES
    return pl.pallas_call(
        functools.partial(_mix_out_kernel, split),
        grid=(n_tiles,),
        in_specs=y_specs + [_rows(D_MODEL, x_off), _layer_weight((k, D_MODEL), 0),
                            _resident((1, D_MODEL)), _resident((1, D_MODEL))],
        out_specs=_rows(D_MODEL),
        out_shape=jax.ShapeDtypeStruct((n_tiles * ROW_TILE, D_MODEL), F32),
        compiler_params=_params("parallel"),
        name="mix_out",
    )(*ys, x, w, *ln)


def _angle_tables(inv_lane):
    base = jnp.concatenate([jnp.arange(N_PROMPT_TILES) * ROW_TILE,
                            jnp.full((N_TILES - N_PROMPT_TILES,), PAST_LEN)]).astype(F32)
    row = jnp.stack([jnp.arange(ROW_TILE), jnp.arange(ROW_TILE) % DEC_SEQ]).astype(F32)
    ta = base[:, None] * inv_lane[None, :]
    ra = row[:, :, None] * inv_lane[None, None, :]
    return jnp.stack([jnp.cos(ta), jnp.sin(ta)], axis=1), jnp.stack([jnp.cos(ra), jnp.sin(ra)], axis=1)


def _table_specs(width):
    return [pl.BlockSpec((1, 2, width), lambda i: (i, 0, 0)),
            pl.BlockSpec((1, 2, ROW_TILE, width), lambda i: (i // N_PROMPT_TILES, 0, 0, 0))]


def _tile_cos_sin(tt_ref, rt_ref):
    cb, sb = tt_ref[0, 0:1, :], tt_ref[0, 1:2, :]
    cr, sr = rt_ref[0, 0], rt_ref[0, 1]
    return cb * cr - sb * sr, sb * cr + cb * sr


def _rotate_blocks(h, n_blocks, cos, sin, shift, partner_above):
    width = cos.shape[1]
    sign = jnp.where(partner_above, -1.0, 1.0).astype(F32)
    sin = sin * jnp.concatenate([sign] * (width // LANES), axis=1)
    out = []
    for j in range(n_blocks):
        t = (j * LANES) % width
        blk = h[:, j * LANES:(j + 1) * LANES]
        partner = jnp.where(partner_above, pltpu.roll(blk, LANES - shift, 1), pltpu.roll(blk, shift, 1))
        out.append(blk * cos[:, t:t + LANES] + partner * sin[:, t:t + LANES])
    return out


def _qkv_kernel(x_ref, w_ref, tt_ref, rt_ref, q_ref, k_ref, v_ref):
    xb = x_ref[...].astype(BF16)
    h = jnp.dot(xb, w_ref[...].astype(BF16), preferred_element_type=F32)
    cos, sin = _tile_cos_sin(tt_ref, rt_ref)
    lane = lax.broadcasted_iota(jnp.int32, (1, LANES), 1)
    n_rot = (D_MODEL + KV_DIM) // LANES
    rot = _rotate_blocks(h, n_rot, cos, sin, HEAD_DIM // 2, (lane % HEAD_DIM) < HEAD_DIM // 2)
    nq = D_MODEL // LANES
    for j in range(nq):
        q_ref[:, j * LANES:(j + 1) * LANES] = (rot[j] * ATTN_SCALE).astype(BF16)
    for j in range(nq, n_rot):
        k_ref[:, (j - nq) * LANES:(j - nq + 1) * LANES] = rot[j]
    v_ref[...] = h[:, D_MODEL + KV_DIM:]


def _qkv_proj(x, w, tables):
    n = x.shape[0]
    width = D_MODEL + 2 * KV_DIM
    return pl.pallas_call(
        _qkv_kernel,
        grid=(n // ROW_TILE,),
        in_specs=[_rows(D_MODEL), _resident((D_MODEL, width)), *_table_specs(LANES)],
        out_specs=[_rows(D_MODEL), _rows(KV_DIM), _rows(KV_DIM)],
        out_shape=[jax.ShapeDtypeStruct((n, D_MODEL), BF16), jax.ShapeDtypeStruct((n, KV_DIM), F32),
                   jax.ShapeDtypeStruct((n, KV_DIM), F32)],
        compiler_params=_params("parallel"),
        name="attn_qkv",
    )(x, w, *tables)


def _rope_inv_lane():
    inv = ROPE_THETA ** (-jnp.arange(0, HEAD_DIM, 2, dtype=F32) / HEAD_DIM)
    return jnp.tile(inv, LANES // (HEAD_DIM // 2))


def _pair_heads(w, axis):
    shape = w.shape
    w = w.reshape(shape[:axis] + (KV_PAIRS, 2, GROUP, HEAD_DIM) + shape[axis + 1:])
    return jnp.swapaxes(w, axis + 1, axis + 2).reshape(shape)


def _pair_operands(kall, vall):
    s = kall.shape[0]
    low = lax.broadcasted_iota(jnp.int32, (s, LANES), 1) < HEAD_DIM
    kb, vb = kall.astype(BF16), vall.astype(BF16)
    zero = jnp.zeros_like(kb)
    one_lo = jnp.where(low, 1.0, 0.0).astype(BF16)
    one_hi = jnp.where(low, 0.0, 1.0).astype(BF16)
    kcat = jnp.concatenate([jnp.where(low, kb, zero), jnp.where(low, zero, kb)], axis=0)
    vcat = jnp.concatenate([jnp.concatenate([jnp.where(low, vb, zero), one_lo], axis=1),
                            jnp.concatenate([jnp.where(low, zero, vb), one_hi], axis=1)], axis=0)
    return kcat, vcat


def _pair_softmax(s, sink_a, sink_b):
    half = s.shape[1] // 2
    sa, sb = s[:, :half], s[:, half:]
    ma = jnp.maximum(jnp.max(sa, axis=-1, keepdims=True), sink_a)
    mb = jnp.maximum(jnp.max(sb, axis=-1, keepdims=True), sink_b)
    e = jnp.concatenate([jnp.exp(sa - ma), jnp.exp(sb - mb)], axis=1).astype(BF16)
    low = lax.broadcasted_iota(jnp.int32, (s.shape[0], LANES), 1) < HEAD_DIM
    return e, jnp.where(low, jnp.exp(sink_a - ma), jnp.exp(sink_b - mb))


SWA_SAMPLE_KEYS = 2 * WINDOW


SWA_STEP = 2


def _swa_kernel(sinks_ref, bias_ref, sbias_ref, q_ref, kp_ref, kc_ref, vp_ref, vc_ref,
                qs_ref, kn_ref, vn_ref, ck_ref, cv_ref,
                o_ref, os_ref, nk_ref, nv_ref, s_ref, e_ref, t_ref, os_scr):
    first = jnp.minimum(pl.program_id(0), 1)
    qall = qs_ref[...].astype(F32)
    for j in range(SWA_STEP):
        seq_j = slice(j * DEC_SEQ, (j + 1) * DEC_SEQ)
        _swa_sample_seq(sinks_ref, sbias_ref[...], qall[seq_j], kn_ref[seq_j, :], vn_ref[seq_j, :],
                        ck_ref[j].T, cv_ref[j].T, os_scr.at[seq_j], nk_ref.at[j], nv_ref.at[j])
        rows_j = slice(j * WINDOW, (j + 1) * WINDOW)
        bias = bias_ref[first] if j == 0 else bias_ref[1]
        for p in range(KV_PAIRS):
            lanes = slice(p * LANES, (p + 1) * LANES)
            k_prev = kp_ref[:, lanes] if j == 0 else kc_ref[(j - 1) * WINDOW:j * WINDOW, lanes]
            v_prev = vp_ref[:, lanes] if j == 0 else vc_ref[(j - 1) * WINDOW:j * WINDOW, lanes]
            kcat, vcat = _pair_operands(jnp.concatenate([k_prev, kc_ref[rows_j, lanes]], axis=0),
                                        jnp.concatenate([v_prev, vc_ref[rows_j, lanes]], axis=0))
            q = q_ref[rows_j, p * GROUP * LANES:(p + 1) * GROUP * LANES]
            qs = jnp.concatenate([q[:, g * LANES:(g + 1) * LANES] for g in range(GROUP)], axis=0)
            c = j * KV_PAIRS + p
            s_ref[c] = lax.dot_general(qs, kcat, NT_DIMS, preferred_element_type=F32)
            for g in range(GROUP):
                rows = slice(g * WINDOW, (g + 1) * WINDOW)
                head_a = (2 * p) * GROUP + g
                e, t = _pair_softmax(s_ref[c, rows, :] + bias, sinks_ref[head_a], sinks_ref[head_a + GROUP])
                e_ref[c, rows, :] = e
                t_ref[c, rows, :] = t
            r = jnp.dot(e_ref[c], vcat, preferred_element_type=F32)
            o = r[:, :LANES] / (r[:, LANES:] + t_ref[c])
            for g in range(GROUP):
                blk = p * GROUP + g
                o_ref[rows_j, blk * LANES:(blk + 1) * LANES] = o[g * WINDOW:(g + 1) * WINDOW].astype(BF16)
    os_ref[...] = os_scr[...].astype(BF16)


def _window_bias(n_rows, row_of, key_dist_and_pos):
    dist, kpos = key_dist_and_pos(row_of(np.arange(n_rows))[:, None])
    valid = (dist >= 0) & (dist < WINDOW) & (kpos >= 0)
    m = np.where(valid, 0.0, -np.inf).astype(np.float32)
    return np.concatenate([m, m], axis=1)


def _swa_sample_seq(sinks_ref, bias, qb, kn, vn, kc, vc, o_ref, nk_ref, nv_ref):
    L = DEC_SEQ
    pad = jnp.zeros((SWA_SAMPLE_KEYS - WINDOW - L, LANES), F32)
    nk_ref[0:WINDOW - L, :] = kc[L:, :]
    nk_ref[WINDOW - L:WINDOW, :] = kn
    nv_ref[0:WINDOW - L, :] = vc[L:, :]
    nv_ref[WINDOW - L:WINDOW, :] = vn
    for p in range(KV_PAIRS):
        lanes = slice(p * LANES, (p + 1) * LANES)
        kcat, vcat = _pair_operands(jnp.concatenate([kc[:, lanes], kn[:, lanes], pad], axis=0),
                                    jnp.concatenate([vc[:, lanes], vn[:, lanes], pad], axis=0))
        qs = jnp.concatenate(
            [qb[:, (p * GROUP + g) * LANES:(p * GROUP + g + 1) * LANES] for g in range(GROUP)],
            axis=0).astype(BF16)
        sink_col = lambda first: jnp.concatenate(
            [jnp.full((L, 1), sinks_ref[first + g], F32) for g in range(GROUP)], axis=0)
        s = lax.dot_general(qs, kcat, NT_DIMS, preferred_element_type=F32) + bias
        e, t = _pair_softmax(s, sink_col(2 * p * GROUP), sink_col((2 * p + 1) * GROUP))
        r = jnp.dot(e, vcat, preferred_element_type=F32)
        o = r[:, :LANES] / (r[:, LANES:] + t)
        for g in range(GROUP):
            blk = p * GROUP + g
            o_ref[:, blk * LANES:(blk + 1) * LANES] = o[g * L:(g + 1) * L]


def _swa(q, k, v, cache_k, cache_v, sinks):
    assert SEQ // WINDOW == DEC_BATCH
    rows = GROUP * WINDOW
    blk_rows, seq_rows = SWA_STEP * WINDOW, SWA_STEP * DEC_SEQ
    si = np.arange(2 * WINDOW)[None, :]
    bias = jnp.asarray(np.stack([
        _window_bias(WINDOW, lambda r: r, lambda qi: (qi + WINDOW - si, blk * WINDOW - WINDOW + si))
        for blk in (0, 1)]))
    cj = np.arange(SWA_SAMPLE_KEYS)[None, :]
    sbias = jnp.asarray(_window_bias(
        GROUP * DEC_SEQ, lambda r: r % DEC_SEQ,
        lambda qi: (np.where(cj < WINDOW + DEC_SEQ, PAST_LEN + qi - (PAST_LEN - WINDOW + cj), -1),
                    PAST_LEN - WINDOW + cj)))
    cur = lambda i: (i, 0)
    prev = lambda i: (jnp.maximum(SWA_STEP * i - 1, 0), 0)
    seq = lambda i: (i + SEQ // seq_rows, 0)
    cache = lambda i: (i, 0, 0)
    n_sets = SWA_STEP * KV_PAIRS
    return pl.pallas_call(
        _swa_kernel,
        grid=(DEC_BATCH // SWA_STEP,),
        in_specs=[_SMEM, _resident((2, WINDOW, 4 * WINDOW)),
                  _resident((GROUP * DEC_SEQ, 2 * SWA_SAMPLE_KEYS)),
                  pl.BlockSpec((blk_rows, D_MODEL), cur),
                  pl.BlockSpec((WINDOW, KV_DIM), prev), pl.BlockSpec((blk_rows, KV_DIM), cur),
                  pl.BlockSpec((WINDOW, KV_DIM), prev), pl.BlockSpec((blk_rows, KV_DIM), cur),
                  pl.BlockSpec((seq_rows, D_MODEL), seq),
                  pl.BlockSpec((seq_rows, KV_DIM), seq), pl.BlockSpec((seq_rows, KV_DIM), seq),
                  pl.BlockSpec((SWA_STEP, KV_DIM, WINDOW), cache), pl.BlockSpec((SWA_STEP, KV_DIM, WINDOW), cache)],
        out_specs=[pl.BlockSpec((blk_rows, D_MODEL), cur), pl.BlockSpec((seq_rows, D_MODEL), cur),
                   pl.BlockSpec((SWA_STEP, WINDOW, KV_DIM), cache), pl.BlockSpec((SWA_STEP, WINDOW, KV_DIM), cache)],
        out_shape=[jax.ShapeDtypeStruct((SEQ, D_MODEL), BF16), jax.ShapeDtypeStruct((N_SAMPLE, D_MODEL), BF16),
                   jax.ShapeDtypeStruct((DEC_BATCH, WINDOW, KV_DIM), F32),
                   jax.ShapeDtypeStruct((DEC_BATCH, WINDOW, KV_DIM), F32)],
        scratch_shapes=[pltpu.VMEM((n_sets, rows, 4 * WINDOW), F32),
                        pltpu.VMEM((n_sets, rows, 4 * WINDOW), BF16),
                        pltpu.VMEM((n_sets, rows, LANES), F32),
                        pltpu.VMEM((seq_rows, D_MODEL), F32)],
        compiler_params=_params("parallel"),
        name="swa",
    )(sinks, bias, sbias, q, k, k, v, v, q, k, v, cache_k, cache_v)


def _ret_proj_kernel(x_ref, w_ref, tt_ref, rt_ref, q_ref, k_ref, v_ref, g_ref):
    xb = x_ref[...].astype(BF16)
    hqk = jnp.dot(xb, w_ref[:, :2 * RET_QK], preferred_element_type=F32)
    cos, sin = _tile_cos_sin(tt_ref, rt_ref)
    lane = lax.broadcasted_iota(jnp.int32, (1, LANES), 1)
    rot = _rotate_blocks(hqk, 2 * RET_QK // LANES, cos, sin, 1, (lane % 2) == 0)
    nq = RET_QK // LANES
    for j in range(nq):
        q_ref[:, j * LANES:(j + 1) * LANES] = rot[j].astype(BF16)
        k_ref[:, j * LANES:(j + 1) * LANES] = rot[nq + j] * (RET_QK_DIM ** -0.5)
    v_ref[...] = jnp.dot(xb, w_ref[:, 2 * RET_QK:2 * RET_QK + RET_V],
                         preferred_element_type=F32).astype(BF16)
    g_ref[...] = _silu(jnp.dot(xb, w_ref[:, 2 * RET_QK + RET_V:], preferred_element_type=F32)).astype(BF16)


def _ret_proj(x, w, tables):
    n = x.shape[0]
    width = 2 * RET_QK + 2 * RET_V
    return pl.pallas_call(
        _ret_proj_kernel,
        grid=(n // ROW_TILE,),
        in_specs=[_rows(D_MODEL), _resident((D_MODEL, width)), *_table_specs(RET_QK_DIM)],
        out_specs=[_rows(RET_QK), _rows(RET_QK), _rows(RET_V), _rows(RET_V)],
        out_shape=[jax.ShapeDtypeStruct((n, RET_QK), BF16), jax.ShapeDtypeStruct((n, RET_QK), F32),
                   jax.ShapeDtypeStruct((n, RET_V), BF16), jax.ShapeDtypeStruct((n, RET_V), BF16)],
        compiler_params=_params("parallel"),
        name="ret_proj",
    )(x, w, *tables)


def _xpos_inv_lane():
    inv = 1.0 / (RET_THETA ** jnp.linspace(0.0, 1.0, RET_QK_DIM // 2, dtype=F32))
    return jnp.repeat(inv, 2)


def _ret_log_decay():
    return jnp.log(1.0 - 2.0 ** (-5.0 - jnp.arange(RET_HEADS, dtype=F32)))


def _ret_tables(L):
    lg = _ret_log_decay()
    idx = jnp.arange(L, dtype=F32)
    diff = idx[:, None] - idx[None, :]
    causal = diff >= 0
    dmat = jnp.where(causal[None], jnp.exp(lg[:, None, None] * jnp.where(causal, diff, 0.0)[None]), 0.0)
    xi = jnp.exp(lg[:, None] * (idx[None, :] + 1.0))
    zeta = jnp.exp(lg[:, None] * (L - 1.0 - idx[None, :]))
    decay = jnp.exp(lg * L)
    rep = lambda a: jnp.broadcast_to(a[:, :, None], (RET_HEADS, L, LANES))
    return dmat, rep(xi), rep(zeta), decay


def _lanes(t, width):
    return jnp.concatenate([t] * (width // LANES), axis=1)


def _group_norm_gate(o, gate):
    mu = jnp.mean(o, axis=-1, keepdims=True)
    d = o - mu
    var = jnp.mean(d * d, axis=-1, keepdims=True)
    return gate * (d * lax.rsqrt(var + LN_EPS))


def _retention_head(qh, kh, vh, s_old, dmat, xi, zeta, decay):
    att = lax.dot_general(qh, kh.astype(BF16), NT_DIMS, preferred_element_type=F32) * dmat
    o = jnp.dot(att.astype(BF16), vh, preferred_element_type=F32)
    o = o + jnp.dot(qh, s_old.astype(BF16), preferred_element_type=F32) * _lanes(xi, RET_V_DIM)
    kz = (kh * _lanes(zeta, RET_QK_DIM)).astype(BF16)
    upd = lax.dot_general(kz, vh, TN_DIMS, preferred_element_type=F32)
    return o, decay * s_old + upd


RET_SAMPLE_BATCH = DEC_BATCH // (SEQ // RET_CHUNK)


def _ret_kernel(q_ref, k_ref, v_ref, g_ref, dmat_ref, xi_ref, zeta_ref, decay_ref,
                x_ref, wo_ref, lg_ref, lb_ref,
                qs_ref, ks_ref, vs_ref, gs_ref, s_in_ref, sdmat_ref, sxi_ref, szeta_ref, sdecay_ref,
                xo_ref, sfin_ref, ys_ref, s_out_ref, s_ref, ys_scr):
    i = pl.program_id(0)

    @pl.when(i == 0)
    def _():
        s_ref[...] = jnp.zeros_like(s_ref)

    L = DEC_SEQ
    qa = qs_ref[...].astype(F32)
    va = vs_ref[...].astype(F32)
    ga = gs_ref[...].astype(F32)
    mixed = None
    for h in range(RET_HEADS):
        qk = slice(h * RET_QK_DIM, (h + 1) * RET_QK_DIM)
        vg = slice(h * RET_V_DIM, (h + 1) * RET_V_DIM)
        o, s_new = _retention_head(q_ref[:, qk], k_ref[:, qk], v_ref[:, vg], s_ref[h],
                                   dmat_ref[h], xi_ref[h], zeta_ref[h], decay_ref[h])
        s_ref[h] = s_new
        yh = _group_norm_gate(o, g_ref[:, vg]).astype(BF16)
        d = jnp.dot(yh, wo_ref[vg, :].astype(BF16), preferred_element_type=F32)
        mixed = d if mixed is None else mixed + d
        for b in range(RET_SAMPLE_BATCH):
            rows = slice(b * L, (b + 1) * L)
            o, s_new = _retention_head(qa[rows, qk].astype(BF16), ks_ref[rows, qk], va[rows, vg].astype(BF16),
                                       s_in_ref[b, h], sdmat_ref[h], sxi_ref[h], szeta_ref[h], sdecay_ref[h])
            s_out_ref[b, h] = s_new
            ys_scr[rows, vg] = _group_norm_gate(o, ga[rows, vg])
    ys_ref[...] = ys_scr[...].astype(BF16)
    xo_ref[...] = _layer_norm(ALPHA * x_ref[...] + mixed, lg_ref[...], lb_ref[...])

    @pl.when(i == pl.num_programs(0) - 1)
    def _():
        sfin_ref[...] = s_ref[...]


def _ret(q, k, v, g, state, tables, sample_tables, x, w_o, ln):
    nc = SEQ // RET_CHUNK
    bb = RET_SAMPLE_BATCH
    rows = bb * DEC_SEQ
    cur = lambda i: (i, 0)
    tok = lambda i: (i + SEQ // rows, 0)
    st = lambda i: (i, 0, 0, 0)
    state_shape = (RET_HEADS, RET_QK_DIM, RET_V_DIM)
    state_block = (bb,) + state_shape
    table_specs = lambda n: [_resident((RET_HEADS, n, n)), _resident((RET_HEADS, n, LANES)),
                             _resident((RET_HEADS, n, LANES)), _SMEM]
    return pl.pallas_call(
        _ret_kernel,
        grid=(nc,),
        in_specs=[pl.BlockSpec((RET_CHUNK, RET_QK), cur), pl.BlockSpec((RET_CHUNK, RET_QK), cur),
                  pl.BlockSpec((RET_CHUNK, RET_V), cur), pl.BlockSpec((RET_CHUNK, RET_V), cur),
                  *table_specs(RET_CHUNK),
                  pl.BlockSpec((RET_CHUNK, D_MODEL), cur), _layer_weight((RET_V, D_MODEL), 0),
                  _resident((1, D_MODEL)), _resident((1, D_MODEL)),
                  pl.BlockSpec((rows, RET_QK), tok), pl.BlockSpec((rows, RET_QK), tok),
                  pl.BlockSpec((rows, RET_V), tok), pl.BlockSpec((rows, RET_V), tok),
                  pl.BlockSpec(state_block, st), *table_specs(DEC_SEQ)],
        out_specs=[pl.BlockSpec((RET_CHUNK, D_MODEL), cur), pl.BlockSpec(state_shape, lambda i: (0, 0, 0)),
                   pl.BlockSpec((rows, RET_V), cur), pl.BlockSpec(state_block, st)],
        out_shape=[jax.ShapeDtypeStruct((SEQ, D_MODEL), F32), jax.ShapeDtypeStruct(state_shape, F32),
                   jax.ShapeDtypeStruct((N_SAMPLE, RET_V), BF16),
                   jax.ShapeDtypeStruct((DEC_BATCH,) + state_shape, F32)],
        scratch_shapes=[pltpu.VMEM(state_shape, F32), pltpu.VMEM((rows, RET_V), F32)],
        compiler_params=_params("arbitrary"),
        name="ret",
    )(q, k, v, g, *tables, x, w_o, *ln, q, k, v, g, state, *sample_tables)


def kernel(x_prompt, x_sample, cache_k_win, cache_v_win, state_ret, ffn1_w_gate, ffn1_w_up, ffn1_w_down,
           ffn2_w_gate, ffn2_w_up, ffn2_w_down, ln_g, ln_b, attn_w_qkv, attn_w_o, attn_sinks, ret_w_in,
           ret_w_o):
    ln = lambda i, j: (ln_g[i, j][None, :], ln_b[i, j][None, :])
    ffn1 = (ffn1_w_gate, ffn1_w_up, ffn1_w_down)
    ffn2 = (ffn2_w_gate, ffn2_w_up, ffn2_w_down)

    x = _half_ffn((x_prompt.reshape(SEQ, D_MODEL), x_sample.reshape(N_SAMPLE, D_MODEL)), ffn1, 0, ln(0, 0))
    w_qkv = jnp.concatenate([_pair_heads(attn_w_qkv[0][:, :D_MODEL], 1), attn_w_qkv[0][:, D_MODEL:]],
                            axis=1).astype(BF16)
    w_o = _pair_heads(attn_w_o, 1).astype(BF16)
    q, k, v = _qkv_proj(x, w_qkv, _angle_tables(_rope_inv_lane()))
    sinks = attn_sinks[0]
    kv_major = lambda c: jnp.transpose(c[0], (0, 2, 3, 1)).reshape(DEC_BATCH, KV_DIM, WINDOW)
    o_p, o_s, nk_s, nv_s = _swa(q, k, v, kv_major(cache_k_win), kv_major(cache_v_win), sinks)
    x = _half_ffn((x,), ffn2, 0, ln(0, 2), mix=((o_p, o_s), w_o, ln(0, 1)))

    kv_shape = (1, 1, WINDOW, N_KV_HEADS, HEAD_DIM)
    new_k_p = k[SEQ - WINDOW:SEQ].reshape(kv_shape)
    new_v_p = v[SEQ - WINDOW:SEQ].reshape(kv_shape)
    new_k_s = nk_s.reshape(1, DEC_BATCH, WINDOW, N_KV_HEADS, HEAD_DIM)
    new_v_s = nv_s.reshape(1, DEC_BATCH, WINDOW, N_KV_HEADS, HEAD_DIM)

    x = _half_ffn((x,), ffn1, 1, ln(1, 0))
    rq, rk, rv, rg = _ret_proj(x, ret_w_in[0].astype(BF16), _angle_tables(_xpos_inv_lane()))
    x_p, s_p, y_s, s_s = _ret(rq, rk, rv, rg, state_ret[0], _ret_tables(RET_CHUNK), _ret_tables(DEC_SEQ),
                              x, ret_w_o, ln(1, 1))
    x_s = _mix_out((y_s,), x, ret_w_o, ln(1, 1))
    y_p, y_s = _half_ffn((x_p, x_s), ffn2, 1, ln(1, 2), split_out=True)

    return (y_p.reshape(1, SEQ, D_MODEL), y_s.reshape(DEC_BATCH, DEC_SEQ, D_MODEL),
            new_k_p, new_v_p, new_k_s, new_v_s, s_p[None, None], s_s[None])
```

```python
import functools
import math

import jax
import jax.numpy as jnp
import numpy as np
from jax import lax
from jax.experimental import pallas as pl
from jax.experimental.pallas import tpu as pltpu

F32 = jnp.float32
BF16 = jnp.bfloat16

D_MODEL = 1024
SEQ = 16384
DEPTH = 2
DEC_BATCH = 128
DEC_SEQ = 8
PAST_LEN = 16384
N_HEADS = 16
N_KV_HEADS = 4
GROUP = N_HEADS // N_KV_HEADS
HEAD_DIM = D_MODEL // N_HEADS
WINDOW = 128
ROPE_THETA = 10000.0
ATTN_SCALE = 1.0 / math.sqrt(HEAD_DIM)
RET_HEADS = 4
RET_QK_DIM = D_MODEL // RET_HEADS
RET_V_DIM = 2 * D_MODEL // RET_HEADS
RET_CHUNK = 256
RET_THETA = 10000.0
FFN_DIM = 2816
LN_EPS = 1e-5
ALPHA = (2.0 * DEPTH) ** 0.25

N_SAMPLE = DEC_BATCH * DEC_SEQ
N_TOK = SEQ + N_SAMPLE
KV_DIM = N_KV_HEADS * HEAD_DIM
RET_QK = RET_HEADS * RET_QK_DIM
RET_V = RET_HEADS * RET_V_DIM

LANES = 128
ROW_TILE = 512
FFN_TILE = 512
FFN_CHUNK = 256
VMEM_LIMIT = 56 * 1024 * 1024

N_TILES = N_TOK // ROW_TILE
N_PROMPT_TILES = SEQ // ROW_TILE
FFN_TILES = N_TOK // FFN_TILE
FFN_PROMPT_TILES = SEQ // FFN_TILE
KV_PAIRS = N_KV_HEADS // 2

NT_DIMS = (((1,), (1,)), ((), ()))
TN_DIMS = (((0,), (0,)), ((), ()))


def _params(*sem):
    return pltpu.CompilerParams(dimension_semantics=sem, vmem_limit_bytes=VMEM_LIMIT)


def _resident(shape):
    return pl.BlockSpec(shape, lambda *_: (0,) * len(shape), pipeline_mode=pl.Buffered(1))


def _rows(width, offset_blocks=0, tile=ROW_TILE):
    return pl.BlockSpec((tile, width), lambda i: (i + offset_blocks, 0))


_ALIASED = pl.BlockSpec(memory_space=pl.ANY)
_SMEM = pl.BlockSpec(memory_space=pltpu.SMEM)


def _layer_norm(y, g, b):
    mu = jnp.mean(y, axis=-1, keepdims=True)
    d = y - mu
    var = jnp.mean(d * d, axis=-1, keepdims=True)
    return d * lax.rsqrt(var + LN_EPS) * g + b


def _silu(x):
    return x / (1.0 + jnp.exp(-x))


def _ffn_kernel(split_in, split_out, mix, *refs):
    n_in = 2 if split_in else 1
    x_refs, refs = refs[:n_in], refs[n_in:]
    if mix:
        (yp_ref, ys_ref, wo_ref, mg_ref, mb_ref), refs = refs[:5], refs[5:]
    (wg_ref, wu_ref, wd_ref, g_ref, b_ref), o_refs = refs[:5], refs[5:]
    i = pl.program_id(0)
    x = x_refs[0][...]
    if split_in:
        x = jnp.where(i < FFN_PROMPT_TILES, x, x_refs[1][...])
    if mix:
        y = jnp.where(i < FFN_PROMPT_TILES, yp_ref[...], ys_ref[...].astype(BF16))
        m = jnp.dot(y, wo_ref[...].astype(BF16), preferred_element_type=F32)
        x = _layer_norm(ALPHA * x + m, mg_ref[...], mb_ref[...])
    xb = x.astype(BF16)
    acc = None
    for c in range(FFN_DIM // FFN_CHUNK):
        sl = slice(c * FFN_CHUNK, (c + 1) * FFN_CHUNK)
        gate = jnp.dot(xb, wg_ref[:, sl].astype(BF16), preferred_element_type=F32)
        up = jnp.dot(xb, wu_ref[:, sl].astype(BF16), preferred_element_type=F32)
        h = (_silu(gate) * up).astype(BF16)
        d = jnp.dot(h, wd_ref[sl, :].astype(BF16), preferred_element_type=F32)
        acc = d if acc is None else acc + d
    y = _layer_norm(ALPHA * x + 0.5 * acc, g_ref[...], b_ref[...])
    if split_out:
        @pl.when(i < FFN_PROMPT_TILES)
        def _():
            o_refs[0][...] = y

        @pl.when(i >= FFN_PROMPT_TILES)
        def _():
            o_refs[1][...] = y
    else:
        o_refs[0][...] = y


_PROMPT_TILE = lambda i: (jnp.minimum(i, FFN_PROMPT_TILES - 1), 0)
_SAMPLE_TILE = lambda i: (jnp.maximum(i - FFN_PROMPT_TILES, 0), 0)


def _layer_weight(shape, layer):
    return pl.BlockSpec((None,) + shape, lambda i: (layer, 0, 0), pipeline_mode=pl.Buffered(1))


def _half_ffn(xs, weights, layer, ln, *, split_out=False, mix=None):
    split_in = len(xs) == 2
    tile = (FFN_TILE, D_MODEL)
    x_specs = ([pl.BlockSpec(tile, _PROMPT_TILE), pl.BlockSpec(tile, _SAMPLE_TILE)]
               if split_in else [_rows(D_MODEL, tile=FFN_TILE)])
    mix_args = ()
    if mix is not None:
        ys, w_o, mix_ln = mix
        k = w_o.shape[1]
        x_specs += [pl.BlockSpec((FFN_TILE, k), _PROMPT_TILE), pl.BlockSpec((FFN_TILE, k), _SAMPLE_TILE),
                    _layer_weight((k, D_MODEL), 0), _resident((1, D_MODEL)), _resident((1, D_MODEL))]
        mix_args = (*ys, w_o, *mix_ln)
    if split_out:
        out_specs = [pl.BlockSpec(tile, _PROMPT_TILE), pl.BlockSpec(tile, _SAMPLE_TILE)]
        out_shape = [jax.ShapeDtypeStruct((SEQ, D_MODEL), F32), jax.ShapeDtypeStruct((N_SAMPLE, D_MODEL), F32)]
    else:
        out_specs = _rows(D_MODEL, tile=FFN_TILE)
        out_shape = jax.ShapeDtypeStruct((N_TOK, D_MODEL), F32)
    return pl.pallas_call(
        functools.partial(_ffn_kernel, split_in, split_out, mix is not None),
        grid=(FFN_TILES,),
        in_specs=x_specs + [_layer_weight((D_MODEL, FFN_DIM), layer), _layer_weight((D_MODEL, FFN_DIM), layer),
                            _layer_weight((FFN_DIM, D_MODEL), layer),
                            _resident((1, D_MODEL)), _resident((1, D_MODEL))],
        out_specs=out_specs,
        out_shape=out_shape,
        compiler_params=_params("arbitrary" if split_out else "parallel"),
        name="half_ffn",
    )(*xs, *mix_args, *weights, *ln)


def _mix_out_kernel(split, *refs):
    n_y = 2 if split else 1
    y_refs, (x_ref, w_ref, g_ref, b_ref, o_ref) = refs[:n_y], refs[n_y:]
    y = y_refs[-1][...].astype(BF16)
    if split:
        y = jnp.where(pl.program_id(0) < N_PROMPT_TILES, y_refs[0][...], y)
    m = jnp.dot(y, w_ref[...].astype(BF16), preferred_element_type=F32)
    o_ref[...] = _layer_norm(ALPHA * x_ref[...] + m, g_ref[...], b_ref[...])


def _mix_out(ys, x, w, ln):
    split = len(ys) == 2
    k = ys[0].shape[1]
    if split:
        y_specs = [pl.BlockSpec((ROW_TILE, k), lambda i: (jnp.minimum(i, N_PROMPT_TILES - 1), 0)),
                   pl.BlockSpec((ROW_TILE, k), lambda i: (jnp.maximum(i - N_PROMPT_TILES, 0), 0))]
        n_tiles, x_off = N_TILES, 0
    else:
        y_specs = [_rows(k)]
        n_tiles, x_off = N_TILES - N_PROMPT_TILES, N_PROMPT_TILES
    return pl.pallas_call(
        functools.partial(_mix_out_kernel, split),
        grid=(n_tiles,),
        in_specs=y_specs + [_rows(D_MODEL, x_off), _layer_weight((k, D_MODEL), 0),
                            _resident((1, D_MODEL)), _resident((1, D_MODEL))],
        out_specs=_rows(D_MODEL),
        out_shape=jax.ShapeDtypeStruct((n_tiles * ROW_TILE, D_MODEL), F32),
        compiler_params=_params("parallel"),
        name="mix_out",
    )(*ys, x, w, *ln)


def _angle_tables(inv_lane, tile=ROW_TILE):
    base = jnp.concatenate([jnp.arange(SEQ // tile) * tile,
                            jnp.full((N_SAMPLE // tile,), PAST_LEN)]).astype(F32)
    row = jnp.stack([jnp.arange(tile), jnp.arange(tile) % DEC_SEQ]).astype(F32)
    ta = base[:, None] * inv_lane[None, :]
    ra = row[:, :, None] * inv_lane[None, None, :]
    return jnp.stack([jnp.cos(ta), jnp.sin(ta)], axis=1), jnp.stack([jnp.cos(ra), jnp.sin(ra)], axis=1)


def _table_specs(width, tile=ROW_TILE, first_tile=0):
    return [pl.BlockSpec((1, 2, width), lambda i: (i + first_tile, 0, 0)),
            pl.BlockSpec((1, 2, tile, width), lambda i: ((i + first_tile) // (SEQ // tile), 0, 0, 0))]


def _tile_cos_sin(tt_ref, rt_ref):
    cb, sb = tt_ref[0, 0:1, :], tt_ref[0, 1:2, :]
    cr, sr = rt_ref[0, 0], rt_ref[0, 1]
    return cb * cr - sb * sr, sb * cr + cb * sr


def _rotate_blocks(h, n_blocks, cos, sin, shift, partner_above):
    width = cos.shape[1]
    sign = jnp.where(partner_above, -1.0, 1.0).astype(F32)
    sin = sin * jnp.concatenate([sign] * (width // LANES), axis=1)
    out = []
    for j in range(n_blocks):
        t = (j * LANES) % width
        blk = h[:, j * LANES:(j + 1) * LANES]
        partner = jnp.where(partner_above, pltpu.roll(blk, LANES - shift, 1), pltpu.roll(blk, shift, 1))
        out.append(blk * cos[:, t:t + LANES] + partner * sin[:, t:t + LANES])
    return out


def _qkv_kernel(x_ref, w_ref, tt_ref, rt_ref, q_ref, k_ref, v_ref):
    xb = x_ref[...].astype(BF16)
    h = jnp.dot(xb, w_ref[...].astype(BF16), preferred_element_type=F32)
    cos, sin = _tile_cos_sin(tt_ref, rt_ref)
    lane = lax.broadcasted_iota(jnp.int32, (1, LANES), 1)
    n_rot = (D_MODEL + KV_DIM) // LANES
    rot = _rotate_blocks(h, n_rot, cos, sin, HEAD_DIM // 2, (lane % HEAD_DIM) < HEAD_DIM // 2)
    nq = D_MODEL // LANES
    for j in range(nq):
        q_ref[:, j * LANES:(j + 1) * LANES] = (rot[j] * ATTN_SCALE).astype(BF16)
    for j in range(nq, n_rot):
        k_ref[:, (j - nq) * LANES:(j - nq + 1) * LANES] = rot[j]
    v_ref[...] = h[:, D_MODEL + KV_DIM:]


def _qkv_proj(x, w, tables):
    n = x.shape[0]
    width = D_MODEL + 2 * KV_DIM
    return pl.pallas_call(
        _qkv_kernel,
        grid=(n // ROW_TILE,),
        in_specs=[_rows(D_MODEL), _resident((D_MODEL, width)), *_table_specs(LANES)],
        out_specs=[_rows(D_MODEL), _rows(KV_DIM), _rows(KV_DIM)],
        out_shape=[jax.ShapeDtypeStruct((n, D_MODEL), BF16), jax.ShapeDtypeStruct((n, KV_DIM), F32),
                   jax.ShapeDtypeStruct((n, KV_DIM), F32)],
        compiler_params=_params("parallel"),
        name="attn_qkv",
    )(x, w, *tables)


def _rope_inv_lane():
    inv = ROPE_THETA ** (-jnp.arange(0, HEAD_DIM, 2, dtype=F32) / HEAD_DIM)
    return jnp.tile(inv, LANES // (HEAD_DIM // 2))


def _pair_heads(w, axis):
    shape = w.shape
    w = w.reshape(shape[:axis] + (KV_PAIRS, 2, GROUP, HEAD_DIM) + shape[axis + 1:])
    return jnp.swapaxes(w, axis + 1, axis + 2).reshape(shape)


def _pair_operands(kall, vall):
    s = kall.shape[0]
    low = lax.broadcasted_iota(jnp.int32, (s, LANES), 1) < HEAD_DIM
    kb, vb = kall.astype(BF16), vall.astype(BF16)
    zero = jnp.zeros_like(kb)
    one_lo = jnp.where(low, 1.0, 0.0).astype(BF16)
    one_hi = jnp.where(low, 0.0, 1.0).astype(BF16)
    kcat = jnp.concatenate([jnp.where(low, kb, zero), jnp.where(low, zero, kb)], axis=0)
    vcat = jnp.concatenate([jnp.concatenate([jnp.where(low, vb, zero), one_lo], axis=1),
                            jnp.concatenate([jnp.where(low, zero, vb), one_hi], axis=1)], axis=0)
    return kcat, vcat


def _pair_softmax(s, sink_a, sink_b):
    half = s.shape[1] // 2
    sa, sb = s[:, :half], s[:, half:]
    ma = jnp.maximum(jnp.max(sa, axis=-1, keepdims=True), sink_a)
    mb = jnp.maximum(jnp.max(sb, axis=-1, keepdims=True), sink_b)
    e = jnp.concatenate([jnp.exp(sa - ma), jnp.exp(sb - mb)], axis=1).astype(BF16)
    low = lax.broadcasted_iota(jnp.int32, (s.shape[0], LANES), 1) < HEAD_DIM
    return e, jnp.where(low, jnp.exp(sink_a - ma), jnp.exp(sink_b - mb))


SWA_SAMPLE_KEYS = 2 * WINDOW


SWA_STEP = 2


def _swa_kernel(sinks_ref, bias_ref, sbias_ref, q_ref, kp_ref, kc_ref, vp_ref, vc_ref,
                qs_ref, kn_ref, vn_ref, ck_ref, cv_ref,
                o_ref, os_ref, nk_ref, nv_ref, s_ref, e_ref, t_ref, os_scr):
    first = jnp.minimum(pl.program_id(0), 1)
    qall = qs_ref[...].astype(F32)
    for j in range(SWA_STEP):
        seq_j = slice(j * DEC_SEQ, (j + 1) * DEC_SEQ)
        _swa_sample_seq(sinks_ref, sbias_ref[...], qall[seq_j], kn_ref[seq_j, :], vn_ref[seq_j, :],
                        ck_ref[j].T, cv_ref[j].T, os_scr.at[seq_j], nk_ref.at[j], nv_ref.at[j])
        rows_j = slice(j * WINDOW, (j + 1) * WINDOW)
        bias = bias_ref[first] if j == 0 else bias_ref[1]
        for p in range(KV_PAIRS):
            lanes = slice(p * LANES, (p + 1) * LANES)
            k_prev = kp_ref[:, lanes] if j == 0 else kc_ref[(j - 1) * WINDOW:j * WINDOW, lanes]
            v_prev = vp_ref[:, lanes] if j == 0 else vc_ref[(j - 1) * WINDOW:j * WINDOW, lanes]
            kcat, vcat = _pair_operands(jnp.concatenate([k_prev, kc_ref[rows_j, lanes]], axis=0),
                                        jnp.concatenate([v_prev, vc_ref[rows_j, lanes]], axis=0))
            q = q_ref[rows_j, p * GROUP * LANES:(p + 1) * GROUP * LANES]
            qs = jnp.concatenate([q[:, g * LANES:(g + 1) * LANES] for g in range(GROUP)], axis=0)
            c = j * KV_PAIRS + p
            s_ref[c] = lax.dot_general(qs, kcat, NT_DIMS, preferred_element_type=F32)
            for g in range(GROUP):
                rows = slice(g * WINDOW, (g + 1) * WINDOW)
                head_a = (2 * p) * GROUP + g
                e, t = _pair_softmax(s_ref[c, rows, :] + bias, sinks_ref[head_a], sinks_ref[head_a + GROUP])
                e_ref[c, rows, :] = e
                t_ref[c, rows, :] = t
            r = jnp.dot(e_ref[c], vcat, preferred_element_type=F32)
            o = r[:, :LANES] / (r[:, LANES:] + t_ref[c])
            for g in range(GROUP):
                blk = p * GROUP + g
                o_ref[rows_j, blk * LANES:(blk + 1) * LANES] = o[g * WINDOW:(g + 1) * WINDOW].astype(BF16)
    os_ref[...] = os_scr[...].astype(BF16)


def _window_bias(n_rows, row_of, key_dist_and_pos):
    dist, kpos = key_dist_and_pos(row_of(np.arange(n_rows))[:, None])
    valid = (dist >= 0) & (dist < WINDOW) & (kpos >= 0)
    m = np.where(valid, 0.0, -np.inf).astype(np.float32)
    return np.concatenate([m, m], axis=1)


def _swa_sample_seq(sinks_ref, bias, qb, kn, vn, kc, vc, o_ref, nk_ref, nv_ref):
    L = DEC_SEQ
    pad = jnp.zeros((SWA_SAMPLE_KEYS - WINDOW - L, LANES), F32)
    nk_ref[0:WINDOW - L, :] = kc[L:, :]
    nk_ref[WINDOW - L:WINDOW, :] = kn
    nv_ref[0:WINDOW - L, :] = vc[L:, :]
    nv_ref[WINDOW - L:WINDOW, :] = vn
    for p in range(KV_PAIRS):
        lanes = slice(p * LANES, (p + 1) * LANES)
        kcat, vcat = _pair_operands(jnp.concatenate([kc[:, lanes], kn[:, lanes], pad], axis=0),
                                    jnp.concatenate([vc[:, lanes], vn[:, lanes], pad], axis=0))
        qs = jnp.concatenate(
            [qb[:, (p * GROUP + g) * LANES:(p * GROUP + g + 1) * LANES] for g in range(GROUP)],
            axis=0).astype(BF16)
        sink_col = lambda first: jnp.concatenate(
            [jnp.full((L, 1), sinks_ref[first + g], F32) for g in range(GROUP)], axis=0)
        s = lax.dot_general(qs, kcat, NT_DIMS, preferred_element_type=F32) + bias
        e, t = _pair_softmax(s, sink_col(2 * p * GROUP), sink_col((2 * p + 1) * GROUP))
        r = jnp.dot(e, vcat, preferred_element_type=F32)
        o = r[:, :LANES] / (r[:, LANES:] + t)
        for g in range(GROUP):
            blk = p * GROUP + g
            o_ref[:, blk * LANES:(blk + 1) * LANES] = o[g * L:(g + 1) * L]


def _swa(q, k, v, cache_k, cache_v, sinks):
    assert SEQ // WINDOW == DEC_BATCH
    rows = GROUP * WINDOW
    blk_rows, seq_rows = SWA_STEP * WINDOW, SWA_STEP * DEC_SEQ
    si = np.arange(2 * WINDOW)[None, :]
    bias = jnp.asarray(np.stack([
        _window_bias(WINDOW, lambda r: r, lambda qi: (qi + WINDOW - si, blk * WINDOW - WINDOW + si))
        for blk in (0, 1)]))
    cj = np.arange(SWA_SAMPLE_KEYS)[None, :]
    sbias = jnp.asarray(_window_bias(
        GROUP * DEC_SEQ, lambda r: r % DEC_SEQ,
        lambda qi: (np.where(cj < WINDOW + DEC_SEQ, PAST_LEN + qi - (PAST_LEN - WINDOW + cj), -1),
                    PAST_LEN - WINDOW + cj)))
    cur = lambda i: (i, 0)
    prev = lambda i: (jnp.maximum(SWA_STEP * i - 1, 0), 0)
    seq = lambda i: (i + SEQ // seq_rows, 0)
    cache = lambda i: (i, 0, 0)
    n_sets = SWA_STEP * KV_PAIRS
    return pl.pallas_call(
        _swa_kernel,
        grid=(DEC_BATCH // SWA_STEP,),
        in_specs=[_SMEM, _resident((2, WINDOW, 4 * WINDOW)),
                  _resident((GROUP * DEC_SEQ, 2 * SWA_SAMPLE_KEYS)),
                  pl.BlockSpec((blk_rows, D_MODEL), cur),
                  pl.BlockSpec((WINDOW, KV_DIM), prev), pl.BlockSpec((blk_rows, KV_DIM), cur),
                  pl.BlockSpec((WINDOW, KV_DIM), prev), pl.BlockSpec((blk_rows, KV_DIM), cur),
                  pl.BlockSpec((seq_rows, D_MODEL), seq),
                  pl.BlockSpec((seq_rows, KV_DIM), seq), pl.BlockSpec((seq_rows, KV_DIM), seq),
                  pl.BlockSpec((SWA_STEP, KV_DIM, WINDOW), cache), pl.BlockSpec((SWA_STEP, KV_DIM, WINDOW), cache)],
        out_specs=[pl.BlockSpec((blk_rows, D_MODEL), cur), pl.BlockSpec((seq_rows, D_MODEL), cur),
                   pl.BlockSpec((SWA_STEP, WINDOW, KV_DIM), cache), pl.BlockSpec((SWA_STEP, WINDOW, KV_DIM), cache)],
        out_shape=[jax.ShapeDtypeStruct((SEQ, D_MODEL), BF16), jax.ShapeDtypeStruct((N_SAMPLE, D_MODEL), BF16),
                   jax.ShapeDtypeStruct((DEC_BATCH, WINDOW, KV_DIM), F32),
                   jax.ShapeDtypeStruct((DEC_BATCH, WINDOW, KV_DIM), F32)],
        scratch_shapes=[pltpu.VMEM((n_sets, rows, 4 * WINDOW), F32),
                        pltpu.VMEM((n_sets, rows, 4 * WINDOW), BF16),
                        pltpu.VMEM((n_sets, rows, LANES), F32),
                        pltpu.VMEM((seq_rows, D_MODEL), F32)],
        compiler_params=_params("parallel"),
        name="swa",
    )(sinks, bias, sbias, q, k, k, v, v, q, k, v, cache_k, cache_v)


RET_IN_WIDTH = 2 * RET_QK + 2 * RET_V


def _ret_project(x, w_ref, tt_ref, rt_ref):
    xb = x.astype(BF16)
    hqk = jnp.dot(xb, w_ref[:, :2 * RET_QK], preferred_element_type=F32)
    cos, sin = _tile_cos_sin(tt_ref, rt_ref)
    lane = lax.broadcasted_iota(jnp.int32, (1, LANES), 1)
    rot = _rotate_blocks(hqk, 2 * RET_QK // LANES, cos, sin, 1, (lane % 2) == 0)
    nq = RET_QK // LANES
    q = jnp.concatenate(rot[:nq], axis=1).astype(BF16)
    k = jnp.concatenate(rot[nq:], axis=1) * (RET_QK_DIM ** -0.5)
    v = jnp.dot(xb, w_ref[:, 2 * RET_QK:2 * RET_QK + RET_V], preferred_element_type=F32).astype(BF16)
    gate = _silu(jnp.dot(xb, w_ref[:, 2 * RET_QK + RET_V:], preferred_element_type=F32)).astype(BF16)
    return q, k, v, gate


def _ret_proj_kernel(x_ref, w_ref, tt_ref, rt_ref, q_ref, k_ref, v_ref, g_ref):
    q_ref[...], k_ref[...], v_ref[...], g_ref[...] = _ret_project(x_ref[...], w_ref, tt_ref, rt_ref)


def _ret_proj_sample(x, w, tables):
    first = N_PROMPT_TILES
    return pl.pallas_call(
        _ret_proj_kernel,
        grid=(N_TILES - first,),
        in_specs=[_rows(D_MODEL, first), _resident((D_MODEL, RET_IN_WIDTH)),
                  *_table_specs(RET_QK_DIM, first_tile=first)],
        out_specs=[_rows(RET_QK), _rows(RET_QK), _rows(RET_V), _rows(RET_V)],
        out_shape=[jax.ShapeDtypeStruct((N_SAMPLE, RET_QK), BF16), jax.ShapeDtypeStruct((N_SAMPLE, RET_QK), F32),
                   jax.ShapeDtypeStruct((N_SAMPLE, RET_V), BF16), jax.ShapeDtypeStruct((N_SAMPLE, RET_V), BF16)],
        compiler_params=_params("parallel"),
        name="ret_proj",
    )(x, w, *tables)


def _xpos_inv_lane():
    inv = 1.0 / (RET_THETA ** jnp.linspace(0.0, 1.0, RET_QK_DIM // 2, dtype=F32))
    return jnp.repeat(inv, 2)


def _ret_log_decay():
    return jnp.log(1.0 - 2.0 ** (-5.0 - jnp.arange(RET_HEADS, dtype=F32)))


def _ret_tables(L):
    lg = _ret_log_decay()
    idx = jnp.arange(L, dtype=F32)
    diff = idx[:, None] - idx[None, :]
    causal = diff >= 0
    dmat = jnp.where(causal[None], jnp.exp(lg[:, None, None] * jnp.where(causal, diff, 0.0)[None]), 0.0)
    xi = jnp.exp(lg[:, None] * (idx[None, :] + 1.0))
    zeta = jnp.exp(lg[:, None] * (L - 1.0 - idx[None, :]))
    decay = jnp.exp(lg * L)
    rep = lambda a: jnp.broadcast_to(a[:, :, None], (RET_HEADS, L, LANES))
    return dmat, rep(xi), rep(zeta), decay


def _lanes(t, width):
    return jnp.concatenate([t] * (width // LANES), axis=1)


def _group_norm_gate(o, gate):
    mu = jnp.mean(o, axis=-1, keepdims=True)
    d = o - mu
    var = jnp.mean(d * d, axis=-1, keepdims=True)
    return gate * (d * lax.rsqrt(var + LN_EPS))


def _retention_head(qh, kh, vh, s_old, dmat, xi, zeta, decay):
    att = lax.dot_general(qh, kh.astype(BF16), NT_DIMS, preferred_element_type=F32) * dmat
    o = jnp.dot(att.astype(BF16), vh, preferred_element_type=F32)
    o = o + jnp.dot(qh, s_old.astype(BF16), preferred_element_type=F32) * _lanes(xi, RET_V_DIM)
    kz = (kh * _lanes(zeta, RET_QK_DIM)).astype(BF16)
    upd = lax.dot_general(kz, vh, TN_DIMS, preferred_element_type=F32)
    return o, decay * s_old + upd


RET_SAMPLE_BATCH = DEC_BATCH // (SEQ // RET_CHUNK)


def _ret_kernel(x_ref, w_in_ref, tt_ref, rt_ref, dmat_ref, xi_ref, zeta_ref, decay_ref,
                wo_ref, lg_ref, lb_ref,
                qs_ref, ks_ref, vs_ref, gs_ref, s_in_ref, sdmat_ref, sxi_ref, szeta_ref, sdecay_ref,
                xo_ref, sfin_ref, ys_ref, s_out_ref, s_ref, ys_scr, q_ref, k_ref, v_ref, g_ref):
    i = pl.program_id(0)

    @pl.when(i == 0)
    def _():
        s_ref[...] = jnp.zeros_like(s_ref)

    q_ref[...], k_ref[...], v_ref[...], g_ref[...] = _ret_project(x_ref[...], w_in_ref, tt_ref, rt_ref)
    L = DEC_SEQ
    qa = qs_ref[...].astype(F32)
    va = vs_ref[...].astype(F32)
    ga = gs_ref[...].astype(F32)
    mixed = None
    for h in range(RET_HEADS):
        qk = slice(h * RET_QK_DIM, (h + 1) * RET_QK_DIM)
        vg = slice(h * RET_V_DIM, (h + 1) * RET_V_DIM)
        o, s_new = _retention_head(q_ref[:, qk], k_ref[:, qk], v_ref[:, vg], s_ref[h],
                                   dmat_ref[h], xi_ref[h], zeta_ref[h], decay_ref[h])
        s_ref[h] = s_new
        yh = _group_norm_gate(o, g_ref[:, vg]).astype(BF16)
        d = jnp.dot(yh, wo_ref[vg, :].astype(BF16), preferred_element_type=F32)
        mixed = d if mixed is None else mixed + d
        for b in range(RET_SAMPLE_BATCH):
            rows = slice(b * L, (b + 1) * L)
            o, s_new = _retention_head(qa[rows, qk].astype(BF16), ks_ref[rows, qk], va[rows, vg].astype(BF16),
                                       s_in_ref[b, h], sdmat_ref[h], sxi_ref[h], szeta_ref[h], sdecay_ref[h])
            s_out_ref[b, h] = s_new
            ys_scr[rows, vg] = _group_norm_gate(o, ga[rows, vg])
    ys_ref[...] = ys_scr[...].astype(BF16)
    xo_ref[...] = _layer_norm(ALPHA * x_ref[...] + mixed, lg_ref[...], lb_ref[...])

    @pl.when(i == pl.num_programs(0) - 1)
    def _():
        sfin_ref[...] = s_ref[...]


def _ret(x, w_in, angle_tables, sample_qkvg, state, tables, sample_tables, w_o, ln):
    nc = SEQ // RET_CHUNK
    bb = RET_SAMPLE_BATCH
    rows = bb * DEC_SEQ
    cur = lambda i: (i, 0)
    st = lambda i: (i, 0, 0, 0)
    state_shape = (RET_HEADS, RET_QK_DIM, RET_V_DIM)
    state_block = (bb,) + state_shape
    table_specs = lambda n: [_resident((RET_HEADS, n, n)), _resident((RET_HEADS, n, LANES)),
                             _resident((RET_HEADS, n, LANES)), _SMEM]
    return pl.pallas_call(
        _ret_kernel,
        grid=(nc,),
        in_specs=[pl.BlockSpec((RET_CHUNK, D_MODEL), cur), _resident((D_MODEL, RET_IN_WIDTH)),
                  *_table_specs(RET_QK_DIM, tile=RET_CHUNK),
                  *table_specs(RET_CHUNK),
                  _layer_weight((RET_V, D_MODEL), 0), _resident((1, D_MODEL)), _resident((1, D_MODEL)),
                  pl.BlockSpec((rows, RET_QK), cur), pl.BlockSpec((rows, RET_QK), cur),
                  pl.BlockSpec((rows, RET_V), cur), pl.BlockSpec((rows, RET_V), cur),
                  pl.BlockSpec(state_block, st), *table_specs(DEC_SEQ)],
        out_specs=[pl.BlockSpec((RET_CHUNK, D_MODEL), cur), pl.BlockSpec(state_shape, lambda i: (0, 0, 0)),
                   pl.BlockSpec((rows, RET_V), cur), pl.BlockSpec(state_block, st)],
        out_shape=[jax.ShapeDtypeStruct((SEQ, D_MODEL), F32), jax.ShapeDtypeStruct(state_shape, F32),
                   jax.ShapeDtypeStruct((N_SAMPLE, RET_V), BF16),
                   jax.ShapeDtypeStruct((DEC_BATCH,) + state_shape, F32)],
        scratch_shapes=[pltpu.VMEM(state_shape, F32), pltpu.VMEM((rows, RET_V), F32),
                        pltpu.VMEM((RET_CHUNK, RET_QK), BF16), pltpu.VMEM((RET_CHUNK, RET_QK), F32),
                        pltpu.VMEM((RET_CHUNK, RET_V), BF16), pltpu.VMEM((RET_CHUNK, RET_V), BF16)],
        compiler_params=_params("arbitrary"),
        name="ret",
    )(x, w_in, *angle_tables, *tables, w_o, *ln, *sample_qkvg, state, *sample_tables)


def kernel(x_prompt, x_sample, cache_k_win, cache_v_win, state_ret, ffn1_w_gate, ffn1_w_up, ffn1_w_down,
           ffn2_w_gate, ffn2_w_up, ffn2_w_down, ln_g, ln_b, attn_w_qkv, attn_w_o, attn_sinks, ret_w_in,
           ret_w_o):
    ln = lambda i, j: (ln_g[i, j][None, :], ln_b[i, j][None, :])
    ffn1 = (ffn1_w_gate, ffn1_w_up, ffn1_w_down)
    ffn2 = (ffn2_w_gate, ffn2_w_up, ffn2_w_down)

    x = _half_ffn((x_prompt.reshape(SEQ, D_MODEL), x_sample.reshape(N_SAMPLE, D_MODEL)), ffn1, 0, ln(0, 0))
    w_qkv = jnp.concatenate([_pair_heads(attn_w_qkv[0][:, :D_MODEL], 1), attn_w_qkv[0][:, D_MODEL:]],
                            axis=1).astype(BF16)
    w_o = _pair_heads(attn_w_o, 1).astype(BF16)
    q, k, v = _qkv_proj(x, w_qkv, _angle_tables(_rope_inv_lane()))
    sinks = attn_sinks[0]
    kv_major = lambda c: jnp.transpose(c[0], (0, 2, 3, 1)).reshape(DEC_BATCH, KV_DIM, WINDOW)
    o_p, o_s, nk_s, nv_s = _swa(q, k, v, kv_major(cache_k_win), kv_major(cache_v_win), sinks)
    x = _half_ffn((x,), ffn2, 0, ln(0, 2), mix=((o_p, o_s), w_o, ln(0, 1)))

    kv_shape = (1, 1, WINDOW, N_KV_HEADS, HEAD_DIM)
    new_k_p = k[SEQ - WINDOW:SEQ].reshape(kv_shape)
    new_v_p = v[SEQ - WINDOW:SEQ].reshape(kv_shape)
    new_k_s = nk_s.reshape(1, DEC_BATCH, WINDOW, N_KV_HEADS, HEAD_DIM)
    new_v_s = nv_s.reshape(1, DEC_BATCH, WINDOW, N_KV_HEADS, HEAD_DIM)

    x = _half_ffn((x,), ffn1, 1, ln(1, 0))
    w_in = ret_w_in[0].astype(BF16)
    w_ro = ret_w_o.astype(BF16)
    inv = _xpos_inv_lane()
    sample_qkvg = _ret_proj_sample(x, w_in, _angle_tables(inv))
    x_p, s_p, y_s, s_s = _ret(x, w_in, _angle_tables(inv, RET_CHUNK), sample_qkvg, state_ret[0],
                              _ret_tables(RET_CHUNK), _ret_tables(DEC_SEQ), w_ro, ln(1, 1))
    x_s = _mix_out((y_s,), x, w_ro, ln(1, 1))
    y_p, y_s = _half_ffn((x_p, x_s), ffn2, 1, ln(1, 2), split_out=True)

    return (y_p.reshape(1, SEQ, D_MODEL), y_s.reshape(DEC_BATCH, DEC_SEQ, D_MODEL),
            new_k_p, new_v_p, new_k_s, new_v_s, s_p[None, None], s_s[None])
```

```python
import functools
import math

import jax
import jax.numpy as jnp
import numpy as np
from jax import lax
from jax.experimental import pallas as pl
from jax.experimental.pallas import tpu as pltpu

F32 = jnp.float32
BF16 = jnp.bfloat16

D_MODEL = 1024
SEQ = 16384
DEPTH = 2
DEC_BATCH = 128
DEC_SEQ = 8
PAST_LEN = 16384
N_HEADS = 16
N_KV_HEADS = 4
GROUP = N_HEADS // N_KV_HEADS
HEAD_DIM = D_MODEL // N_HEADS
WINDOW = 128
ROPE_THETA = 10000.0
ATTN_SCALE = 1.0 / math.sqrt(HEAD_DIM)
RET_HEADS = 4
RET_QK_DIM = D_MODEL // RET_HEADS
RET_V_DIM = 2 * D_MODEL // RET_HEADS
RET_CHUNK = 256
RET_THETA = 10000.0
FFN_DIM = 2816
LN_EPS = 1e-5
ALPHA = (2.0 * DEPTH) ** 0.25

N_SAMPLE = DEC_BATCH * DEC_SEQ
N_TOK = SEQ + N_SAMPLE
KV_DIM = N_KV_HEADS * HEAD_DIM
RET_QK = RET_HEADS * RET_QK_DIM
RET_V = RET_HEADS * RET_V_DIM

LANES = 128
ROW_TILE = 512
FFN_TILE = 512
FFN_CHUNK = 256
VMEM_LIMIT = 56 * 1024 * 1024

N_TILES = N_TOK // ROW_TILE
N_PROMPT_TILES = SEQ // ROW_TILE
FFN_TILES = N_TOK // FFN_TILE
FFN_PROMPT_TILES = SEQ // FFN_TILE
KV_PAIRS = N_KV_HEADS // 2

NT_DIMS = (((1,), (1,)), ((), ()))
TN_DIMS = (((0,), (0,)), ((), ()))


def _params(*sem):
    return pltpu.CompilerParams(dimension_semantics=sem, vmem_limit_bytes=VMEM_LIMIT)


def _resident(shape):
    return pl.BlockSpec(shape, lambda *_: (0,) * len(shape), pipeline_mode=pl.Buffered(1))


def _rows(width, offset_blocks=0, tile=ROW_TILE):
    return pl.BlockSpec((tile, width), lambda i: (i + offset_blocks, 0))


_SMEM = pl.BlockSpec(memory_space=pltpu.SMEM)


def _layer_norm(y, g, b):
    mu = jnp.mean(y, axis=-1, keepdims=True)
    d = y - mu
    var = jnp.mean(d * d, axis=-1, keepdims=True)
    return d * lax.rsqrt(var + LN_EPS) * g + b


def _silu(x):
    return x / (1.0 + jnp.exp(-x))


def _ffn_kernel(split_in, split_out, mix, *refs):
    n_in = 2 if split_in else 1
    x_refs, refs = refs[:n_in], refs[n_in:]
    if mix:
        (yp_ref, ys_ref, wo_ref, mg_ref, mb_ref), refs = refs[:5], refs[5:]
    (wg_ref, wu_ref, wd_ref, g_ref, b_ref), o_refs = refs[:5], refs[5:]
    i = pl.program_id(0)
    x = x_refs[0][...]
    if split_in:
        x = jnp.where(i < FFN_PROMPT_TILES, x, x_refs[1][...])
    if mix:
        y = jnp.where(i < FFN_PROMPT_TILES, yp_ref[...], ys_ref[...].astype(BF16))
        m = jnp.dot(y, wo_ref[...].astype(BF16), preferred_element_type=F32)
        x = _layer_norm(ALPHA * x + m, mg_ref[...], mb_ref[...])
    xb = x.astype(BF16)
    acc = None
    for c in range(FFN_DIM // FFN_CHUNK):
        sl = slice(c * FFN_CHUNK, (c + 1) * FFN_CHUNK)
        gate = jnp.dot(xb, wg_ref[:, sl].astype(BF16), preferred_element_type=F32)
        up = jnp.dot(xb, wu_ref[:, sl].astype(BF16), preferred_element_type=F32)
        h = (_silu(gate) * up).astype(BF16)
        d = jnp.dot(h, wd_ref[sl, :].astype(BF16), preferred_element_type=F32)
        acc = d if acc is None else acc + d
    y = _layer_norm(ALPHA * x + 0.5 * acc, g_ref[...], b_ref[...])
    if split_out:
        @pl.when(i < FFN_PROMPT_TILES)
        def _():
            o_refs[0][...] = y

        @pl.when(i >= FFN_PROMPT_TILES)
        def _():
            o_refs[1][...] = y
    else:
        o_refs[0][...] = y


_PROMPT_TILE = lambda i: (jnp.minimum(i, FFN_PROMPT_TILES - 1), 0)
_SAMPLE_TILE = lambda i: (jnp.maximum(i - FFN_PROMPT_TILES, 0), 0)


def _layer_weight(shape, layer):
    return pl.BlockSpec((None,) + shape, lambda i: (layer, 0, 0), pipeline_mode=pl.Buffered(1))


def _half_ffn(xs, weights, layer, ln, *, split_out=False, mix=None):
    split_in = len(xs) == 2
    tile = (FFN_TILE, D_MODEL)
    x_specs = ([pl.BlockSpec(tile, _PROMPT_TILE), pl.BlockSpec(tile, _SAMPLE_TILE)]
               if split_in else [_rows(D_MODEL, tile=FFN_TILE)])
    mix_args = ()
    if mix is not None:
        ys, w_o, mix_ln = mix
        k = w_o.shape[1]
        x_specs += [pl.BlockSpec((FFN_TILE, k), _PROMPT_TILE), pl.BlockSpec((FFN_TILE, k), _SAMPLE_TILE),
                    _layer_weight((k, D_MODEL), 0), _resident((1, D_MODEL)), _resident((1, D_MODEL))]
        mix_args = (*ys, w_o, *mix_ln)
    if split_out:
        out_specs = [pl.BlockSpec(tile, _PROMPT_TILE), pl.BlockSpec(tile, _SAMPLE_TILE)]
        out_shape = [jax.ShapeDtypeStruct((SEQ, D_MODEL), F32), jax.ShapeDtypeStruct((N_SAMPLE, D_MODEL), F32)]
    else:
        out_specs = _rows(D_MODEL, tile=FFN_TILE)
        out_shape = jax.ShapeDtypeStruct((N_TOK, D_MODEL), F32)
    return pl.pallas_call(
        functools.partial(_ffn_kernel, split_in, split_out, mix is not None),
        grid=(FFN_TILES,),
        in_specs=x_specs + [_layer_weight((D_MODEL, FFN_DIM), layer), _layer_weight((D_MODEL, FFN_DIM), layer),
                            _layer_weight((FFN_DIM, D_MODEL), layer),
                            _resident((1, D_MODEL)), _resident((1, D_MODEL))],
        out_specs=out_specs,
        out_shape=out_shape,
        compiler_params=_params("arbitrary" if split_out else "parallel"),
        name="half_ffn",
    )(*xs, *mix_args, *weights, *ln)


def _mix_out_kernel(y_ref, x_ref, w_ref, g_ref, b_ref, o_ref):
    m = jnp.dot(y_ref[...], w_ref[...], preferred_element_type=F32)
    o_ref[...] = _layer_norm(ALPHA * x_ref[...] + m, g_ref[...], b_ref[...])


def _mix_out_sample(y, x, w, ln):
    k = y.shape[1]
    return pl.pallas_call(
        _mix_out_kernel,
        grid=(N_TILES - N_PROMPT_TILES,),
        in_specs=[_rows(k), _rows(D_MODEL, N_PROMPT_TILES), _layer_weight((k, D_MODEL), 0),
                  _resident((1, D_MODEL)), _resident((1, D_MODEL))],
        out_specs=_rows(D_MODEL),
        out_shape=jax.ShapeDtypeStruct((N_SAMPLE, D_MODEL), F32),
        compiler_params=_params("parallel"),
        name="mix_out",
    )(y, x, w, *ln)


def _angle_tables(inv_lane, tile=ROW_TILE):
    base = jnp.concatenate([jnp.arange(SEQ // tile) * tile,
                            jnp.full((N_SAMPLE // tile,), PAST_LEN)]).astype(F32)
    row = jnp.stack([jnp.arange(tile), jnp.arange(tile) % DEC_SEQ]).astype(F32)
    ta = base[:, None] * inv_lane[None, :]
    ra = row[:, :, None] * inv_lane[None, None, :]
    return jnp.stack([jnp.cos(ta), jnp.sin(ta)], axis=1), jnp.stack([jnp.cos(ra), jnp.sin(ra)], axis=1)


def _table_specs(width, tile=ROW_TILE, first_tile=0):
    return [pl.BlockSpec((1, 2, width), lambda i: (i + first_tile, 0, 0)),
            pl.BlockSpec((1, 2, tile, width), lambda i: ((i + first_tile) // (SEQ // tile), 0, 0, 0))]


def _tile_cos_sin(tt_ref, rt_ref):
    cb, sb = tt_ref[0, 0:1, :], tt_ref[0, 1:2, :]
    cr, sr = rt_ref[0, 0], rt_ref[0, 1]
    return cb * cr - sb * sr, sb * cr + cb * sr


def _rotate_blocks(h, n_blocks, cos, sin, shift, partner_above):
    width = cos.shape[1]
    sign = jnp.where(partner_above, -1.0, 1.0).astype(F32)
    sin = sin * jnp.concatenate([sign] * (width // LANES), axis=1)
    out = []
    for j in range(n_blocks):
        t = (j * LANES) % width
        blk = h[:, j * LANES:(j + 1) * LANES]
        partner = jnp.where(partner_above, pltpu.roll(blk, LANES - shift, 1), pltpu.roll(blk, shift, 1))
        out.append(blk * cos[:, t:t + LANES] + partner * sin[:, t:t + LANES])
    return out


def _qkv_kernel(x_ref, w_ref, tt_ref, rt_ref, q_ref, k_ref, v_ref):
    xb = x_ref[...].astype(BF16)
    h = jnp.dot(xb, w_ref[...].astype(BF16), preferred_element_type=F32)
    cos, sin = _tile_cos_sin(tt_ref, rt_ref)
    lane = lax.broadcasted_iota(jnp.int32, (1, LANES), 1)
    n_rot = (D_MODEL + KV_DIM) // LANES
    rot = _rotate_blocks(h, n_rot, cos, sin, HEAD_DIM // 2, (lane % HEAD_DIM) < HEAD_DIM // 2)
    nq = D_MODEL // LANES
    for j in range(nq):
        q_ref[:, j * LANES:(j + 1) * LANES] = (rot[j] * ATTN_SCALE).astype(BF16)
    for j in range(nq, n_rot):
        k_ref[:, (j - nq) * LANES:(j - nq + 1) * LANES] = rot[j]
    v_ref[...] = h[:, D_MODEL + KV_DIM:]


QKV_TILE = 1024


def _qkv_proj(x, w, tables):
    n = x.shape[0]
    width = D_MODEL + 2 * KV_DIM
    rows = functools.partial(_rows, tile=QKV_TILE)
    return pl.pallas_call(
        _qkv_kernel,
        grid=(n // QKV_TILE,),
        in_specs=[rows(D_MODEL), _resident((D_MODEL, width)), *_table_specs(LANES, tile=QKV_TILE)],
        out_specs=[rows(D_MODEL), rows(KV_DIM), rows(KV_DIM)],
        out_shape=[jax.ShapeDtypeStruct((n, D_MODEL), BF16), jax.ShapeDtypeStruct((n, KV_DIM), F32),
                   jax.ShapeDtypeStruct((n, KV_DIM), F32)],
        compiler_params=_params("parallel"),
        name="attn_qkv",
    )(x, w, *tables)


def _rope_inv_lane():
    inv = ROPE_THETA ** (-jnp.arange(0, HEAD_DIM, 2, dtype=F32) / HEAD_DIM)
    return jnp.tile(inv, LANES // (HEAD_DIM // 2))


def _pair_heads(w, axis):
    shape = w.shape
    w = w.reshape(shape[:axis] + (KV_PAIRS, 2, GROUP, HEAD_DIM) + shape[axis + 1:])
    return jnp.swapaxes(w, axis + 1, axis + 2).reshape(shape)


def _pair_operands(kall, vall):
    s = kall.shape[0]
    low = lax.broadcasted_iota(jnp.int32, (s, LANES), 1) < HEAD_DIM
    kb, vb = kall.astype(BF16), vall.astype(BF16)
    zero = jnp.zeros_like(kb)
    one_lo = jnp.where(low, 1.0, 0.0).astype(BF16)
    one_hi = jnp.where(low, 0.0, 1.0).astype(BF16)
    kcat = jnp.concatenate([jnp.where(low, kb, zero), jnp.where(low, zero, kb)], axis=0)
    vcat = jnp.concatenate([jnp.concatenate([jnp.where(low, vb, zero), one_lo], axis=1),
                            jnp.concatenate([jnp.where(low, zero, vb), one_hi], axis=1)], axis=0)
    return kcat, vcat


def _pair_softmax(s, sink_a, sink_b):
    half = s.shape[1] // 2
    sa, sb = s[:, :half], s[:, half:]
    ma = jnp.maximum(jnp.max(sa, axis=-1, keepdims=True), sink_a)
    mb = jnp.maximum(jnp.max(sb, axis=-1, keepdims=True), sink_b)
    e = jnp.concatenate([jnp.exp(sa - ma), jnp.exp(sb - mb)], axis=1).astype(BF16)
    low = lax.broadcasted_iota(jnp.int32, (s.shape[0], LANES), 1) < HEAD_DIM
    return e, jnp.where(low, jnp.exp(sink_a - ma), jnp.exp(sink_b - mb))


SWA_SAMPLE_KEYS = 2 * WINDOW


SWA_STEP = 4


def _swa_kernel(sinks_ref, bias_ref, sbias_ref, q_ref, kp_ref, kc_ref, vp_ref, vc_ref,
                qs_ref, kn_ref, vn_ref, ck_ref, cv_ref,
                o_ref, os_ref, nk_ref, nv_ref, s_ref, e_ref, t_ref, os_scr):
    first = jnp.minimum(pl.program_id(0), 1)
    qall = qs_ref[...].astype(F32)
    for j in range(SWA_STEP):
        seq_j = slice(j * DEC_SEQ, (j + 1) * DEC_SEQ)
        _swa_sample_seq(sinks_ref, sbias_ref[...], qall[seq_j], kn_ref[seq_j, :], vn_ref[seq_j, :],
                        ck_ref[j].T, cv_ref[j].T, os_scr.at[seq_j], nk_ref.at[j], nv_ref.at[j])
        rows_j = slice(j * WINDOW, (j + 1) * WINDOW)
        bias = bias_ref[first] if j == 0 else bias_ref[1]
        for p in range(KV_PAIRS):
            lanes = slice(p * LANES, (p + 1) * LANES)
            k_prev = kp_ref[:, lanes] if j == 0 else kc_ref[(j - 1) * WINDOW:j * WINDOW, lanes]
            v_prev = vp_ref[:, lanes] if j == 0 else vc_ref[(j - 1) * WINDOW:j * WINDOW, lanes]
            kcat, vcat = _pair_operands(jnp.concatenate([k_prev, kc_ref[rows_j, lanes]], axis=0),
                                        jnp.concatenate([v_prev, vc_ref[rows_j, lanes]], axis=0))
            q = q_ref[rows_j, p * GROUP * LANES:(p + 1) * GROUP * LANES]
            qs = jnp.concatenate([q[:, g * LANES:(g + 1) * LANES] for g in range(GROUP)], axis=0)
            c = j * KV_PAIRS + p
            s_ref[c] = lax.dot_general(qs, kcat, NT_DIMS, preferred_element_type=F32)
            for g in range(GROUP):
                rows = slice(g * WINDOW, (g + 1) * WINDOW)
                head_a = (2 * p) * GROUP + g
                e, t = _pair_softmax(s_ref[c, rows, :] + bias, sinks_ref[head_a], sinks_ref[head_a + GROUP])
                e_ref[c, rows, :] = e
                t_ref[c, rows, :] = t
            r = jnp.dot(e_ref[c], vcat, preferred_element_type=F32)
            o = r[:, :LANES] / (r[:, LANES:] + t_ref[c])
            for g in range(GROUP):
                blk = p * GROUP + g
                o_ref[rows_j, blk * LANES:(blk + 1) * LANES] = o[g * WINDOW:(g + 1) * WINDOW].astype(BF16)
    os_ref[...] = os_scr[...].astype(BF16)


def _window_bias(n_rows, row_of, key_dist_and_pos):
    dist, kpos = key_dist_and_pos(row_of(np.arange(n_rows))[:, None])
    valid = (dist >= 0) & (dist < WINDOW) & (kpos >= 0)
    m = np.where(valid, 0.0, -np.inf).astype(np.float32)
    return np.concatenate([m, m], axis=1)


def _swa_sample_seq(sinks_ref, bias, qb, kn, vn, kc, vc, o_ref, nk_ref, nv_ref):
    L = DEC_SEQ
    pad = jnp.zeros((SWA_SAMPLE_KEYS - WINDOW - L, LANES), F32)
    nk_ref[0:WINDOW - L, :] = kc[L:, :]
    nk_ref[WINDOW - L:WINDOW, :] = kn
    nv_ref[0:WINDOW - L, :] = vc[L:, :]
    nv_ref[WINDOW - L:WINDOW, :] = vn
    for p in range(KV_PAIRS):
        lanes = slice(p * LANES, (p + 1) * LANES)
        kcat, vcat = _pair_operands(jnp.concatenate([kc[:, lanes], kn[:, lanes], pad], axis=0),
                                    jnp.concatenate([vc[:, lanes], vn[:, lanes], pad], axis=0))
        qs = jnp.concatenate(
            [qb[:, (p * GROUP + g) * LANES:(p * GROUP + g + 1) * LANES] for g in range(GROUP)],
            axis=0).astype(BF16)
        sink_col = lambda first: jnp.concatenate(
            [jnp.full((L, 1), sinks_ref[first + g], F32) for g in range(GROUP)], axis=0)
        s = lax.dot_general(qs, kcat, NT_DIMS, preferred_element_type=F32) + bias
        e, t = _pair_softmax(s, sink_col(2 * p * GROUP), sink_col((2 * p + 1) * GROUP))
        r = jnp.dot(e, vcat, preferred_element_type=F32)
        o = r[:, :LANES] / (r[:, LANES:] + t)
        for g in range(GROUP):
            blk = p * GROUP + g
            o_ref[:, blk * LANES:(blk + 1) * LANES] = o[g * L:(g + 1) * L]


def _swa(q, k, v, cache_k, cache_v, sinks):
    assert SEQ // WINDOW == DEC_BATCH
    rows = GROUP * WINDOW
    blk_rows, seq_rows = SWA_STEP * WINDOW, SWA_STEP * DEC_SEQ
    si = np.arange(2 * WINDOW)[None, :]
    bias = jnp.asarray(np.stack([
        _window_bias(WINDOW, lambda r: r, lambda qi: (qi + WINDOW - si, blk * WINDOW - WINDOW + si))
        for blk in (0, 1)]))
    cj = np.arange(SWA_SAMPLE_KEYS)[None, :]
    sbias = jnp.asarray(_window_bias(
        GROUP * DEC_SEQ, lambda r: r % DEC_SEQ,
        lambda qi: (np.where(cj < WINDOW + DEC_SEQ, PAST_LEN + qi - (PAST_LEN - WINDOW + cj), -1),
                    PAST_LEN - WINDOW + cj)))
    cur = lambda i: (i, 0)
    prev = lambda i: (jnp.maximum(SWA_STEP * i - 1, 0), 0)
    seq = lambda i: (i + SEQ // seq_rows, 0)
    cache = lambda i: (i, 0, 0)
    n_sets = SWA_STEP * KV_PAIRS
    return pl.pallas_call(
        _swa_kernel,
        grid=(DEC_BATCH // SWA_STEP,),
        in_specs=[_SMEM, _resident((2, WINDOW, 4 * WINDOW)),
                  _resident((GROUP * DEC_SEQ, 2 * SWA_SAMPLE_KEYS)),
                  pl.BlockSpec((blk_rows, D_MODEL), cur),
                  pl.BlockSpec((WINDOW, KV_DIM), prev), pl.BlockSpec((blk_rows, KV_DIM), cur),
                  pl.BlockSpec((WINDOW, KV_DIM), prev), pl.BlockSpec((blk_rows, KV_DIM), cur),
                  pl.BlockSpec((seq_rows, D_MODEL), seq),
                  pl.BlockSpec((seq_rows, KV_DIM), seq), pl.BlockSpec((seq_rows, KV_DIM), seq),
                  pl.BlockSpec((SWA_STEP, KV_DIM, WINDOW), cache), pl.BlockSpec((SWA_STEP, KV_DIM, WINDOW), cache)],
        out_specs=[pl.BlockSpec((blk_rows, D_MODEL), cur), pl.BlockSpec((seq_rows, D_MODEL), cur),
                   pl.BlockSpec((SWA_STEP, WINDOW, KV_DIM), cache), pl.BlockSpec((SWA_STEP, WINDOW, KV_DIM), cache)],
        out_shape=[jax.ShapeDtypeStruct((SEQ, D_MODEL), BF16), jax.ShapeDtypeStruct((N_SAMPLE, D_MODEL), BF16),
                   jax.ShapeDtypeStruct((DEC_BATCH, WINDOW, KV_DIM), F32),
                   jax.ShapeDtypeStruct((DEC_BATCH, WINDOW, KV_DIM), F32)],
        scratch_shapes=[pltpu.VMEM((n_sets, rows, 4 * WINDOW), F32),
                        pltpu.VMEM((n_sets, rows, 4 * WINDOW), BF16),
                        pltpu.VMEM((n_sets, rows, LANES), F32),
                        pltpu.VMEM((seq_rows, D_MODEL), F32)],
        compiler_params=_params("parallel"),
        name="swa",
    )(sinks, bias, sbias, q, k, k, v, v, q, k, v, cache_k, cache_v)


RET_IN_WIDTH = 2 * RET_QK + 2 * RET_V


def _ret_project(x, w_ref, tt_ref, rt_ref):
    xb = x.astype(BF16)
    hqk = jnp.dot(xb, w_ref[:, :2 * RET_QK], preferred_element_type=F32)
    cos, sin = _tile_cos_sin(tt_ref, rt_ref)
    lane = lax.broadcasted_iota(jnp.int32, (1, LANES), 1)
    rot = _rotate_blocks(hqk, 2 * RET_QK // LANES, cos, sin, 1, (lane % 2) == 0)
    nq = RET_QK // LANES
    q = jnp.concatenate(rot[:nq], axis=1).astype(BF16)
    k = jnp.concatenate(rot[nq:], axis=1) * (RET_QK_DIM ** -0.5)
    v = jnp.dot(xb, w_ref[:, 2 * RET_QK:2 * RET_QK + RET_V], preferred_element_type=F32).astype(BF16)
    gate = _silu(jnp.dot(xb, w_ref[:, 2 * RET_QK + RET_V:], preferred_element_type=F32)).astype(BF16)
    return q, k, v, gate


def _ret_proj_kernel(x_ref, w_ref, tt_ref, rt_ref, q_ref, k_ref, v_ref, g_ref):
    q_ref[...], k_ref[...], v_ref[...], g_ref[...] = _ret_project(x_ref[...], w_ref, tt_ref, rt_ref)


def _ret_proj_sample(x, w, tables):
    first = N_PROMPT_TILES
    return pl.pallas_call(
        _ret_proj_kernel,
        grid=(N_TILES - first,),
        in_specs=[_rows(D_MODEL, first), _resident((D_MODEL, RET_IN_WIDTH)),
                  *_table_specs(RET_QK_DIM, first_tile=first)],
        out_specs=[_rows(RET_QK), _rows(RET_QK), _rows(RET_V), _rows(RET_V)],
        out_shape=[jax.ShapeDtypeStruct((N_SAMPLE, RET_QK), BF16), jax.ShapeDtypeStruct((N_SAMPLE, RET_QK), F32),
                   jax.ShapeDtypeStruct((N_SAMPLE, RET_V), BF16), jax.ShapeDtypeStruct((N_SAMPLE, RET_V), BF16)],
        compiler_params=_params("parallel"),
        name="ret_proj",
    )(x, w, *tables)


def _xpos_inv_lane():
    inv = 1.0 / (RET_THETA ** jnp.linspace(0.0, 1.0, RET_QK_DIM // 2, dtype=F32))
    return jnp.repeat(inv, 2)


def _ret_log_decay():
    return jnp.log(1.0 - 2.0 ** (-5.0 - jnp.arange(RET_HEADS, dtype=F32)))


def _ret_tables(L):
    lg = _ret_log_decay()
    idx = jnp.arange(L, dtype=F32)
    diff = idx[:, None] - idx[None, :]
    causal = diff >= 0
    dmat = jnp.where(causal[None], jnp.exp(lg[:, None, None] * jnp.where(causal, diff, 0.0)[None]), 0.0)
    xi = jnp.exp(lg[:, None] * (idx[None, :] + 1.0))
    zeta = jnp.exp(lg[:, None] * (L - 1.0 - idx[None, :]))
    decay = jnp.exp(lg * L)
    rep = lambda a: jnp.broadcast_to(a[:, :, None], (RET_HEADS, L, LANES))
    return dmat, rep(xi), rep(zeta), decay


def _lanes(t, width):
    return jnp.concatenate([t] * (width // LANES), axis=1)


def _group_norm_gate(o, gate):
    mu = jnp.mean(o, axis=-1, keepdims=True)
    d = o - mu
    var = jnp.mean(d * d, axis=-1, keepdims=True)
    return gate * (d * lax.rsqrt(var + LN_EPS))


def _retention_head(qh, kh, vh, s_old, dmat, xi, zeta, decay):
    att = lax.dot_general(qh, kh.astype(BF16), NT_DIMS, preferred_element_type=F32) * dmat
    o = jnp.dot(att.astype(BF16), vh, preferred_element_type=F32)
    o = o + jnp.dot(qh, s_old.astype(BF16), preferred_element_type=F32) * _lanes(xi, RET_V_DIM)
    kz = (kh * _lanes(zeta, RET_QK_DIM)).astype(BF16)
    upd = lax.dot_general(kz, vh, TN_DIMS, preferred_element_type=F32)
    return o, decay * s_old + upd


RET_SAMPLE_BATCH = DEC_BATCH // (SEQ // RET_CHUNK)


def _ret_kernel(x_ref, w_in_ref, tt_ref, rt_ref, dmat_ref, xi_ref, zeta_ref, decay_ref,
                wo_ref, lg_ref, lb_ref,
                qs_ref, ks_ref, vs_ref, gs_ref, s_in_ref, sdmat_ref, sxi_ref, szeta_ref, sdecay_ref,
                xo_ref, sfin_ref, ys_ref, s_out_ref, s_ref, ys_scr, q_ref, k_ref, v_ref, g_ref):
    i = pl.program_id(0)

    @pl.when(i == 0)
    def _():
        s_ref[...] = jnp.zeros_like(s_ref)

    q_ref[...], k_ref[...], v_ref[...], g_ref[...] = _ret_project(x_ref[...], w_in_ref, tt_ref, rt_ref)
    L = DEC_SEQ
    qa = qs_ref[...].astype(F32)
    va = vs_ref[...].astype(F32)
    ga = gs_ref[...].astype(F32)
    mixed = None
    for h in range(RET_HEADS):
        qk = slice(h * RET_QK_DIM, (h + 1) * RET_QK_DIM)
        vg = slice(h * RET_V_DIM, (h + 1) * RET_V_DIM)
        o, s_new = _retention_head(q_ref[:, qk], k_ref[:, qk], v_ref[:, vg], s_ref[h],
                                   dmat_ref[h], xi_ref[h], zeta_ref[h], decay_ref[h])
        s_ref[h] = s_new
        yh = _group_norm_gate(o, g_ref[:, vg]).astype(BF16)
        d = jnp.dot(yh, wo_ref[vg, :].astype(BF16), preferred_element_type=F32)
        mixed = d if mixed is None else mixed + d
        for b in range(RET_SAMPLE_BATCH):
            rows = slice(b * L, (b + 1) * L)
            o, s_new = _retention_head(qa[rows, qk].astype(BF16), ks_ref[rows, qk], va[rows, vg].astype(BF16),
                                       s_in_ref[b, h], sdmat_ref[h], sxi_ref[h], szeta_ref[h], sdecay_ref[h])
            s_out_ref[b, h] = s_new
            ys_scr[rows, vg] = _group_norm_gate(o, ga[rows, vg])
    ys_ref[...] = ys_scr[...].astype(BF16)
    xo_ref[...] = _layer_norm(ALPHA * x_ref[...] + mixed, lg_ref[...], lb_ref[...])

    @pl.when(i == pl.num_programs(0) - 1)
    def _():
        sfin_ref[...] = s_ref[...]


def _ret(x, w_in, angle_tables, sample_qkvg, state, tables, sample_tables, w_o, ln):
    nc = SEQ // RET_CHUNK
    bb = RET_SAMPLE_BATCH
    rows = bb * DEC_SEQ
    cur = lambda i: (i, 0)
    st = lambda i: (i, 0, 0, 0)
    state_shape = (RET_HEADS, RET_QK_DIM, RET_V_DIM)
    state_block = (bb,) + state_shape
    table_specs = lambda n: [_resident((RET_HEADS, n, n)), _resident((RET_HEADS, n, LANES)),
                             _resident((RET_HEADS, n, LANES)), _SMEM]
    return pl.pallas_call(
        _ret_kernel,
        grid=(nc,),
        in_specs=[pl.BlockSpec((RET_CHUNK, D_MODEL), cur), _resident((D_MODEL, RET_IN_WIDTH)),
                  *_table_specs(RET_QK_DIM, tile=RET_CHUNK),
                  *table_specs(RET_CHUNK),
                  _layer_weight((RET_V, D_MODEL), 0), _resident((1, D_MODEL)), _resident((1, D_MODEL)),
                  pl.BlockSpec((rows, RET_QK), cur), pl.BlockSpec((rows, RET_QK), cur),
                  pl.BlockSpec((rows, RET_V), cur), pl.BlockSpec((rows, RET_V), cur),
                  pl.BlockSpec(state_block, st), *table_specs(DEC_SEQ)],
        out_specs=[pl.BlockSpec((RET_CHUNK, D_MODEL), cur), pl.BlockSpec(state_shape, lambda i: (0, 0, 0)),
                   pl.BlockSpec((rows, RET_V), cur), pl.BlockSpec(state_block, st)],
        out_shape=[jax.ShapeDtypeStruct((SEQ, D_MODEL), F32), jax.ShapeDtypeStruct(state_shape, F32),
                   jax.ShapeDtypeStruct((N_SAMPLE, RET_V), BF16),
                   jax.ShapeDtypeStruct((DEC_BATCH,) + state_shape, F32)],
        scratch_shapes=[pltpu.VMEM(state_shape, F32), pltpu.VMEM((rows, RET_V), F32),
                        pltpu.VMEM((RET_CHUNK, RET_QK), BF16), pltpu.VMEM((RET_CHUNK, RET_QK), F32),
                        pltpu.VMEM((RET_CHUNK, RET_V), BF16), pltpu.VMEM((RET_CHUNK, RET_V), BF16)],
        compiler_params=_params("arbitrary"),
        name="ret",
    )(x, w_in, *angle_tables, *tables, w_o, *ln, *sample_qkvg, state, *sample_tables)


def kernel(x_prompt, x_sample, cache_k_win, cache_v_win, state_ret, ffn1_w_gate, ffn1_w_up, ffn1_w_down,
           ffn2_w_gate, ffn2_w_up, ffn2_w_down, ln_g, ln_b, attn_w_qkv, attn_w_o, attn_sinks, ret_w_in,
           ret_w_o):
    ln = lambda i, j: (ln_g[i, j][None, :], ln_b[i, j][None, :])
    ffn1 = (ffn1_w_gate, ffn1_w_up, ffn1_w_down)
    ffn2 = (ffn2_w_gate, ffn2_w_up, ffn2_w_down)

    x = _half_ffn((x_prompt.reshape(SEQ, D_MODEL), x_sample.reshape(N_SAMPLE, D_MODEL)), ffn1, 0, ln(0, 0))
    w_qkv = jnp.concatenate([_pair_heads(attn_w_qkv[0][:, :D_MODEL], 1), attn_w_qkv[0][:, D_MODEL:]],
                            axis=1).astype(BF16)
    w_o = _pair_heads(attn_w_o, 1).astype(BF16)
    q, k, v = _qkv_proj(x, w_qkv, _angle_tables(_rope_inv_lane(), QKV_TILE))
    sinks = attn_sinks[0]
    kv_major = lambda c: jnp.transpose(c[0], (0, 2, 3, 1)).reshape(DEC_BATCH, KV_DIM, WINDOW)
    o_p, o_s, nk_s, nv_s = _swa(q, k, v, kv_major(cache_k_win), kv_major(cache_v_win), sinks)
    x = _half_ffn((x,), ffn2, 0, ln(0, 2), mix=((o_p, o_s), w_o, ln(0, 1)))

    kv_shape = (1, 1, WINDOW, N_KV_HEADS, HEAD_DIM)
    new_k_p = k[SEQ - WINDOW:SEQ].reshape(kv_shape)
    new_v_p = v[SEQ - WINDOW:SEQ].reshape(kv_shape)
    new_k_s = nk_s.reshape(1, DEC_BATCH, WINDOW, N_KV_HEADS, HEAD_DIM)
    new_v_s = nv_s.reshape(1, DEC_BATCH, WINDOW, N_KV_HEADS, HEAD_DIM)

    x = _half_ffn((x,), ffn1, 1, ln(1, 0))
    w_in = ret_w_in[0].astype(BF16)
    w_ro = ret_w_o.astype(BF16)
    inv = _xpos_inv_lane()
    sample_qkvg = _ret_proj_sample(x, w_in, _angle_tables(inv))
    x_p, s_p, y_s, s_s = _ret(x, w_in, _angle_tables(inv, RET_CHUNK), sample_qkvg, state_ret[0],
                              _ret_tables(RET_CHUNK), _ret_tables(DEC_SEQ), w_ro, ln(1, 1))
    x_s = _mix_out_sample(y_s, x, w_ro, ln(1, 1))
    y_p, y_s = _half_ffn((x_p, x_s), ffn2, 1, ln(1, 2), split_out=True)

    return (y_p.reshape(1, SEQ, D_MODEL), y_s.reshape(DEC_BATCH, DEC_SEQ, D_MODEL),
            new_k_p, new_v_p, new_k_s, new_v_s, s_p[None, None], s_s[None])
```

```python
import functools
import math

import jax
import jax.numpy as jnp
import numpy as np
from jax import lax
from jax.experimental import pallas as pl
from jax.experimental.pallas import tpu as pltpu

F32 = jnp.float32
BF16 = jnp.bfloat16

D_MODEL = 1024
SEQ = 16384
DEPTH = 2
DEC_BATCH = 128
DEC_SEQ = 8
PAST_LEN = 16384
N_HEADS = 16
N_KV_HEADS = 4
GROUP = N_HEADS // N_KV_HEADS
HEAD_DIM = D_MODEL // N_HEADS
WINDOW = 128
ROPE_THETA = 10000.0
ATTN_SCALE = 1.0 / math.sqrt(HEAD_DIM)
RET_HEADS = 4
RET_QK_DIM = D_MODEL // RET_HEADS
RET_V_DIM = 2 * D_MODEL // RET_HEADS
RET_CHUNK = 256
RET_THETA = 10000.0
FFN_DIM = 2816
LN_EPS = 1e-5
ALPHA = (2.0 * DEPTH) ** 0.25

N_SAMPLE = DEC_BATCH * DEC_SEQ
N_TOK = SEQ + N_SAMPLE
KV_DIM = N_KV_HEADS * HEAD_DIM
RET_QK = RET_HEADS * RET_QK_DIM
RET_V = RET_HEADS * RET_V_DIM

LANES = 128
ROW_TILE = 512
FFN_TILE = 512
FFN_CHUNK = 256
VMEM_LIMIT = 56 * 1024 * 1024

N_TILES = N_TOK // ROW_TILE
N_PROMPT_TILES = SEQ // ROW_TILE
FFN_TILES = N_TOK // FFN_TILE
FFN_PROMPT_TILES = SEQ // FFN_TILE
KV_PAIRS = N_KV_HEADS // 2

NT_DIMS = (((1,), (1,)), ((), ()))
TN_DIMS = (((0,), (0,)), ((), ()))


def _params(*sem):
    return pltpu.CompilerParams(dimension_semantics=sem, vmem_limit_bytes=VMEM_LIMIT)


def _resident(shape):
    return pl.BlockSpec(shape, lambda *_: (0,) * len(shape), pipeline_mode=pl.Buffered(1))


def _rows(width, offset_blocks=0, tile=ROW_TILE):
    return pl.BlockSpec((tile, width), lambda i: (i + offset_blocks, 0))


_SMEM = pl.BlockSpec(memory_space=pltpu.SMEM)


def _layer_norm(y, g, b):
    mu = jnp.mean(y, axis=-1, keepdims=True)
    d = y - mu
    var = jnp.mean(d * d, axis=-1, keepdims=True)
    return d * lax.rsqrt(var + LN_EPS) * g + b


def _silu(x):
    return x / (1.0 + jnp.exp(-x))


def _ffn_kernel(split_in, split_out, mix, *refs):
    n_in = 2 if split_in else 1
    x_refs, refs = refs[:n_in], refs[n_in:]
    if mix:
        (yp_ref, ys_ref, wo_ref, mg_ref, mb_ref), refs = refs[:5], refs[5:]
    (wg_ref, wu_ref, wd_ref, g_ref, b_ref), o_refs = refs[:5], refs[5:]
    i = pl.program_id(0)
    x = x_refs[0][...]
    if split_in:
        x = jnp.where(i < FFN_PROMPT_TILES, x, x_refs[1][...])
    if mix:
        y = jnp.where(i < FFN_PROMPT_TILES, yp_ref[...], ys_ref[...].astype(BF16))
        m = jnp.dot(y, wo_ref[...].astype(BF16), preferred_element_type=F32)
        x = _layer_norm(ALPHA * x + m, mg_ref[...], mb_ref[...])
    xb = x.astype(BF16)
    acc = None
    for c in range(FFN_DIM // FFN_CHUNK):
        sl = slice(c * FFN_CHUNK, (c + 1) * FFN_CHUNK)
        gate = jnp.dot(xb, wg_ref[:, sl].astype(BF16), preferred_element_type=F32)
        up = jnp.dot(xb, wu_ref[:, sl].astype(BF16), preferred_element_type=F32)
        h = (_silu(gate) * up).astype(BF16)
        d = jnp.dot(h, wd_ref[sl, :].astype(BF16), preferred_element_type=F32)
        acc = d if acc is None else acc + d
    y = _layer_norm(ALPHA * x + 0.5 * acc, g_ref[...], b_ref[...])
    if split_out:
        @pl.when(i < FFN_PROMPT_TILES)
        def _():
            o_refs[0][...] = y

        @pl.when(i >= FFN_PROMPT_TILES)
        def _():
            o_refs[1][...] = y
    else:
        o_refs[0][...] = y


_PROMPT_TILE = lambda i: (jnp.minimum(i, FFN_PROMPT_TILES - 1), 0)
_SAMPLE_TILE = lambda i: (jnp.maximum(i - FFN_PROMPT_TILES, 0), 0)


def _layer_weight(shape, layer):
    return pl.BlockSpec((None,) + shape, lambda i: (layer, 0, 0), pipeline_mode=pl.Buffered(1))


def _half_ffn(xs, weights, layer, ln, *, split_out=False, mix=None):
    split_in = len(xs) == 2
    tile = (FFN_TILE, D_MODEL)
    x_specs = ([pl.BlockSpec(tile, _PROMPT_TILE), pl.BlockSpec(tile, _SAMPLE_TILE)]
               if split_in else [_rows(D_MODEL, tile=FFN_TILE)])
    mix_args = ()
    if mix is not None:
        ys, w_o, mix_ln = mix
        k = w_o.shape[1]
        x_specs += [pl.BlockSpec((FFN_TILE, k), _PROMPT_TILE), pl.BlockSpec((FFN_TILE, k), _SAMPLE_TILE),
                    _layer_weight((k, D_MODEL), 0), _resident((1, D_MODEL)), _resident((1, D_MODEL))]
        mix_args = (*ys, w_o, *mix_ln)
    if split_out:
        out_specs = [pl.BlockSpec(tile, _PROMPT_TILE), pl.BlockSpec(tile, _SAMPLE_TILE)]
        out_shape = [jax.ShapeDtypeStruct((SEQ, D_MODEL), F32), jax.ShapeDtypeStruct((N_SAMPLE, D_MODEL), F32)]
    else:
        out_specs = _rows(D_MODEL, tile=FFN_TILE)
        out_shape = jax.ShapeDtypeStruct((N_TOK, D_MODEL), F32)
    return pl.pallas_call(
        functools.partial(_ffn_kernel, split_in, split_out, mix is not None),
        grid=(FFN_TILES,),
        in_specs=x_specs + [_layer_weight((D_MODEL, FFN_DIM), layer), _layer_weight((D_MODEL, FFN_DIM), layer),
                            _layer_weight((FFN_DIM, D_MODEL), layer),
                            _resident((1, D_MODEL)), _resident((1, D_MODEL))],
        out_specs=out_specs,
        out_shape=out_shape,
        compiler_params=_params("arbitrary" if split_out else "parallel"),
        name="half_ffn",
    )(*xs, *mix_args, *weights, *ln)


def _mix_out_kernel(y_ref, x_ref, w_ref, g_ref, b_ref, o_ref):
    m = jnp.dot(y_ref[...], w_ref[...].astype(BF16), preferred_element_type=F32)
    o_ref[...] = _layer_norm(ALPHA * x_ref[...] + m, g_ref[...], b_ref[...])


def _mix_out_sample(y, x, w, ln):
    k = y.shape[1]
    return pl.pallas_call(
        _mix_out_kernel,
        grid=(N_TILES - N_PROMPT_TILES,),
        in_specs=[_rows(k), _rows(D_MODEL, N_PROMPT_TILES), _layer_weight((k, D_MODEL), 0),
                  _resident((1, D_MODEL)), _resident((1, D_MODEL))],
        out_specs=_rows(D_MODEL),
        out_shape=jax.ShapeDtypeStruct((N_SAMPLE, D_MODEL), F32),
        compiler_params=_params("parallel"),
        name="mix_out",
    )(y, x, w, *ln)


def _angle_tables(inv_lane, tile=ROW_TILE):
    base = jnp.concatenate([jnp.arange(SEQ // tile) * tile,
                            jnp.full((N_SAMPLE // tile,), PAST_LEN)]).astype(F32)
    row = jnp.stack([jnp.arange(tile), jnp.arange(tile) % DEC_SEQ]).astype(F32)
    ta = base[:, None] * inv_lane[None, :]
    ra = row[:, :, None] * inv_lane[None, None, :]
    return jnp.stack([jnp.cos(ta), jnp.sin(ta)], axis=1), jnp.stack([jnp.cos(ra), jnp.sin(ra)], axis=1)


def _table_specs(width, tile=ROW_TILE, first_tile=0):
    return [pl.BlockSpec((1, 2, width), lambda i: (i + first_tile, 0, 0)),
            pl.BlockSpec((1, 2, tile, width), lambda i: ((i + first_tile) // (SEQ // tile), 0, 0, 0))]


def _tile_cos_sin(tt_ref, rt_ref):
    cb, sb = tt_ref[0, 0:1, :], tt_ref[0, 1:2, :]
    cr, sr = rt_ref[0, 0], rt_ref[0, 1]
    return cb * cr - sb * sr, sb * cr + cb * sr


def _rotate_blocks(h, n_blocks, cos, sin, shift, partner_above):
    width = cos.shape[1]
    sign = jnp.where(partner_above, -1.0, 1.0).astype(F32)
    sin = sin * jnp.concatenate([sign] * (width // LANES), axis=1)
    out = []
    for j in range(n_blocks):
        t = (j * LANES) % width
        blk = h[:, j * LANES:(j + 1) * LANES]
        partner = jnp.where(partner_above, pltpu.roll(blk, LANES - shift, 1), pltpu.roll(blk, shift, 1))
        out.append(blk * cos[:, t:t + LANES] + partner * sin[:, t:t + LANES])
    return out


def _qkv_kernel(x_ref, w_ref, tt_ref, rt_ref, q_ref, k_ref, v_ref):
    xb = x_ref[...].astype(BF16)
    h = jnp.dot(xb, w_ref[...].astype(BF16), preferred_element_type=F32)
    cos, sin = _tile_cos_sin(tt_ref, rt_ref)
    lane = lax.broadcasted_iota(jnp.int32, (1, LANES), 1)
    n_rot = (D_MODEL + KV_DIM) // LANES
    rot = _rotate_blocks(h, n_rot, cos, sin, HEAD_DIM // 2, (lane % HEAD_DIM) < HEAD_DIM // 2)
    nq = D_MODEL // LANES
    for j in range(nq):
        q_ref[:, j * LANES:(j + 1) * LANES] = (rot[j] * ATTN_SCALE).astype(BF16)
    for j in range(nq, n_rot):
        k_ref[:, (j - nq) * LANES:(j - nq + 1) * LANES] = rot[j]
    v_ref[...] = h[:, D_MODEL + KV_DIM:]


QKV_TILE = 1024


def _qkv_proj(x, w, tables):
    n = x.shape[0]
    width = D_MODEL + 2 * KV_DIM
    rows = functools.partial(_rows, tile=QKV_TILE)
    return pl.pallas_call(
        _qkv_kernel,
        grid=(n // QKV_TILE,),
        in_specs=[rows(D_MODEL), _resident((D_MODEL, width)), *_table_specs(LANES, tile=QKV_TILE)],
        out_specs=[rows(D_MODEL), rows(KV_DIM), rows(KV_DIM)],
        out_shape=[jax.ShapeDtypeStruct((n, D_MODEL), BF16), jax.ShapeDtypeStruct((n, KV_DIM), F32),
                   jax.ShapeDtypeStruct((n, KV_DIM), F32)],
        compiler_params=_params("parallel"),
        name="attn_qkv",
    )(x, w, *tables)


def _rope_inv_lane():
    inv = ROPE_THETA ** (-jnp.arange(0, HEAD_DIM, 2, dtype=F32) / HEAD_DIM)
    return jnp.tile(inv, LANES // (HEAD_DIM // 2))


def _pair_heads(w, axis):
    shape = w.shape
    w = w.reshape(shape[:axis] + (KV_PAIRS, 2, GROUP, HEAD_DIM) + shape[axis + 1:])
    return jnp.swapaxes(w, axis + 1, axis + 2).reshape(shape)


def _pair_operands(kall, vall):
    s = kall.shape[0]
    low = lax.broadcasted_iota(jnp.int32, (s, LANES), 1) < HEAD_DIM
    kb, vb = kall.astype(BF16), vall.astype(BF16)
    zero = jnp.zeros_like(kb)
    one_lo = jnp.where(low, 1.0, 0.0).astype(BF16)
    one_hi = jnp.where(low, 0.0, 1.0).astype(BF16)
    kcat = jnp.concatenate([jnp.where(low, kb, zero), jnp.where(low, zero, kb)], axis=0)
    vcat = jnp.concatenate([jnp.concatenate([jnp.where(low, vb, zero), one_lo], axis=1),
                            jnp.concatenate([jnp.where(low, zero, vb), one_hi], axis=1)], axis=0)
    return kcat, vcat


def _pair_softmax(s, sink_a, sink_b):
    half = s.shape[1] // 2
    sa, sb = s[:, :half], s[:, half:]
    ma = jnp.maximum(jnp.max(sa, axis=-1, keepdims=True), sink_a)
    mb = jnp.maximum(jnp.max(sb, axis=-1, keepdims=True), sink_b)
    e = jnp.concatenate([jnp.exp(sa - ma), jnp.exp(sb - mb)], axis=1).astype(BF16)
    low = lax.broadcasted_iota(jnp.int32, (s.shape[0], LANES), 1) < HEAD_DIM
    return e, jnp.where(low, jnp.exp(sink_a - ma), jnp.exp(sink_b - mb))


SWA_SAMPLE_KEYS = 2 * WINDOW


SWA_STEP = 2


def _swa_kernel(sinks_ref, bias_ref, sbias_ref, q_ref, kp_ref, kc_ref, vp_ref, vc_ref,
                qs_ref, kn_ref, vn_ref, ck_ref, cv_ref,
                o_ref, os_ref, nk_ref, nv_ref, s_ref, e_ref, t_ref, os_scr):
    first = jnp.minimum(pl.program_id(0), 1)
    qall = qs_ref[...].astype(F32)
    for j in range(SWA_STEP):
        seq_j = slice(j * DEC_SEQ, (j + 1) * DEC_SEQ)
        _swa_sample_seq(sinks_ref, sbias_ref[...], qall[seq_j], kn_ref[seq_j, :], vn_ref[seq_j, :],
                        ck_ref[j].T, cv_ref[j].T, os_scr.at[seq_j], nk_ref.at[j], nv_ref.at[j])
        rows_j = slice(j * WINDOW, (j + 1) * WINDOW)
        bias = bias_ref[first] if j == 0 else bias_ref[1]
        for p in range(KV_PAIRS):
            lanes = slice(p * LANES, (p + 1) * LANES)
            k_prev = kp_ref[:, lanes] if j == 0 else kc_ref[(j - 1) * WINDOW:j * WINDOW, lanes]
            v_prev = vp_ref[:, lanes] if j == 0 else vc_ref[(j - 1) * WINDOW:j * WINDOW, lanes]
            kcat, vcat = _pair_operands(jnp.concatenate([k_prev, kc_ref[rows_j, lanes]], axis=0),
                                        jnp.concatenate([v_prev, vc_ref[rows_j, lanes]], axis=0))
            q = q_ref[rows_j, p * GROUP * LANES:(p + 1) * GROUP * LANES]
            qs = jnp.concatenate([q[:, g * LANES:(g + 1) * LANES] for g in range(GROUP)], axis=0)
            c = j * KV_PAIRS + p
            s_ref[c] = lax.dot_general(qs, kcat, NT_DIMS, preferred_element_type=F32)
            for g in range(GROUP):
                rows = slice(g * WINDOW, (g + 1) * WINDOW)
                head_a = (2 * p) * GROUP + g
                e, t = _pair_softmax(s_ref[c, rows, :] + bias, sinks_ref[head_a], sinks_ref[head_a + GROUP])
                e_ref[c, rows, :] = e
                t_ref[c, rows, :] = t
            r = jnp.dot(e_ref[c], vcat, preferred_element_type=F32)
            o = r[:, :LANES] / (r[:, LANES:] + t_ref[c])
            for g in range(GROUP):
                blk = p * GROUP + g
                o_ref[rows_j, blk * LANES:(blk + 1) * LANES] = o[g * WINDOW:(g + 1) * WINDOW].astype(BF16)
    os_ref[...] = os_scr[...].astype(BF16)


def _window_bias(n_rows, row_of, key_dist_and_pos):
    dist, kpos = key_dist_and_pos(row_of(np.arange(n_rows))[:, None])
    valid = (dist >= 0) & (dist < WINDOW) & (kpos >= 0)
    m = np.where(valid, 0.0, -np.inf).astype(np.float32)
    return np.concatenate([m, m], axis=1)


def _swa_sample_seq(sinks_ref, bias, qb, kn, vn, kc, vc, o_ref, nk_ref, nv_ref):
    L = DEC_SEQ
    pad = jnp.zeros((SWA_SAMPLE_KEYS - WINDOW - L, LANES), F32)
    nk_ref[0:WINDOW - L, :] = kc[L:, :]
    nk_ref[WINDOW - L:WINDOW, :] = kn
    nv_ref[0:WINDOW - L, :] = vc[L:, :]
    nv_ref[WINDOW - L:WINDOW, :] = vn
    for p in range(KV_PAIRS):
        lanes = slice(p * LANES, (p + 1) * LANES)
        kcat, vcat = _pair_operands(jnp.concatenate([kc[:, lanes], kn[:, lanes], pad], axis=0),
                                    jnp.concatenate([vc[:, lanes], vn[:, lanes], pad], axis=0))
        qs = jnp.concatenate(
            [qb[:, (p * GROUP + g) * LANES:(p * GROUP + g + 1) * LANES] for g in range(GROUP)],
            axis=0).astype(BF16)
        sink_col = lambda first: jnp.concatenate(
            [jnp.full((L, 1), sinks_ref[first + g], F32) for g in range(GROUP)], axis=0)
        s = lax.dot_general(qs, kcat, NT_DIMS, preferred_element_type=F32) + bias
        e, t = _pair_softmax(s, sink_col(2 * p * GROUP), sink_col((2 * p + 1) * GROUP))
        r = jnp.dot(e, vcat, preferred_element_type=F32)
        o = r[:, :LANES] / (r[:, LANES:] + t)
        for g in range(GROUP):
            blk = p * GROUP + g
            o_ref[:, blk * LANES:(blk + 1) * LANES] = o[g * L:(g + 1) * L]


def _swa(q, k, v, cache_k, cache_v, sinks):
    assert SEQ // WINDOW == DEC_BATCH
    rows = GROUP * WINDOW
    blk_rows, seq_rows = SWA_STEP * WINDOW, SWA_STEP * DEC_SEQ
    si = np.arange(2 * WINDOW)[None, :]
    bias = jnp.asarray(np.stack([
        _window_bias(WINDOW, lambda r: r, lambda qi: (qi + WINDOW - si, blk * WINDOW - WINDOW + si))
        for blk in (0, 1)]))
    cj = np.arange(SWA_SAMPLE_KEYS)[None, :]
    sbias = jnp.asarray(_window_bias(
        GROUP * DEC_SEQ, lambda r: r % DEC_SEQ,
        lambda qi: (np.where(cj < WINDOW + DEC_SEQ, PAST_LEN + qi - (PAST_LEN - WINDOW + cj), -1),
                    PAST_LEN - WINDOW + cj)))
    cur = lambda i: (i, 0)
    prev = lambda i: (jnp.maximum(SWA_STEP * i - 1, 0), 0)
    seq = lambda i: (i + SEQ // seq_rows, 0)
    cache = lambda i: (i, 0, 0)
    n_sets = SWA_STEP * KV_PAIRS
    return pl.pallas_call(
        _swa_kernel,
        grid=(DEC_BATCH // SWA_STEP,),
        in_specs=[_SMEM, _resident((2, WINDOW, 4 * WINDOW)),
                  _resident((GROUP * DEC_SEQ, 2 * SWA_SAMPLE_KEYS)),
                  pl.BlockSpec((blk_rows, D_MODEL), cur),
                  pl.BlockSpec((WINDOW, KV_DIM), prev), pl.BlockSpec((blk_rows, KV_DIM), cur),
                  pl.BlockSpec((WINDOW, KV_DIM), prev), pl.BlockSpec((blk_rows, KV_DIM), cur),
                  pl.BlockSpec((seq_rows, D_MODEL), seq),
                  pl.BlockSpec((seq_rows, KV_DIM), seq), pl.BlockSpec((seq_rows, KV_DIM), seq),
                  pl.BlockSpec((SWA_STEP, KV_DIM, WINDOW), cache), pl.BlockSpec((SWA_STEP, KV_DIM, WINDOW), cache)],
        out_specs=[pl.BlockSpec((blk_rows, D_MODEL), cur), pl.BlockSpec((seq_rows, D_MODEL), cur),
                   pl.BlockSpec((SWA_STEP, WINDOW, KV_DIM), cache), pl.BlockSpec((SWA_STEP, WINDOW, KV_DIM), cache)],
        out_shape=[jax.ShapeDtypeStruct((SEQ, D_MODEL), BF16), jax.ShapeDtypeStruct((N_SAMPLE, D_MODEL), BF16),
                   jax.ShapeDtypeStruct((DEC_BATCH, WINDOW, KV_DIM), F32),
                   jax.ShapeDtypeStruct((DEC_BATCH, WINDOW, KV_DIM), F32)],
        scratch_shapes=[pltpu.VMEM((n_sets, rows, 4 * WINDOW), F32),
                        pltpu.VMEM((n_sets, rows, 4 * WINDOW), BF16),
                        pltpu.VMEM((n_sets, rows, LANES), F32),
                        pltpu.VMEM((seq_rows, D_MODEL), F32)],
        compiler_params=_params("parallel"),
        name="swa",
    )(sinks, bias, sbias, q, k, k, v, v, q, k, v, cache_k, cache_v)


RET_IN_WIDTH = 2 * RET_QK + 2 * RET_V


def _ret_project(x, w_ref, tt_ref, rt_ref):
    xb = x.astype(BF16)
    hqk = jnp.dot(xb, w_ref[:, :2 * RET_QK].astype(BF16), preferred_element_type=F32)
    cos, sin = _tile_cos_sin(tt_ref, rt_ref)
    lane = lax.broadcasted_iota(jnp.int32, (1, LANES), 1)
    rot = _rotate_blocks(hqk, 2 * RET_QK // LANES, cos, sin, 1, (lane % 2) == 0)
    nq = RET_QK // LANES
    q = jnp.concatenate(rot[:nq], axis=1).astype(BF16)
    k = jnp.concatenate(rot[nq:], axis=1) * (RET_QK_DIM ** -0.5)
    v = jnp.dot(xb, w_ref[:, 2 * RET_QK:2 * RET_QK + RET_V].astype(BF16),
                preferred_element_type=F32).astype(BF16)
    gate = _silu(jnp.dot(xb, w_ref[:, 2 * RET_QK + RET_V:].astype(BF16), preferred_element_type=F32)).astype(BF16)
    return q, k, v, gate


def _ret_proj_kernel(x_ref, w_ref, tt_ref, rt_ref, q_ref, k_ref, v_ref, g_ref):
    q_ref[...], k_ref[...], v_ref[...], g_ref[...] = _ret_project(x_ref[...], w_ref, tt_ref, rt_ref)


def _ret_proj(x, w, tables):
    n = x.shape[0]
    return pl.pallas_call(
        _ret_proj_kernel,
        grid=(n // ROW_TILE,),
        in_specs=[_rows(D_MODEL), _layer_weight((D_MODEL, RET_IN_WIDTH), 0), *_table_specs(RET_QK_DIM)],
        out_specs=[_rows(RET_QK), _rows(RET_QK), _rows(RET_V), _rows(RET_V)],
        out_shape=[jax.ShapeDtypeStruct((n, RET_QK), BF16), jax.ShapeDtypeStruct((n, RET_QK), F32),
                   jax.ShapeDtypeStruct((n, RET_V), BF16), jax.ShapeDtypeStruct((n, RET_V), BF16)],
        compiler_params=_params("parallel"),
        name="ret_proj",
    )(x, w, *tables)


def _xpos_inv_lane():
    inv = 1.0 / (RET_THETA ** jnp.linspace(0.0, 1.0, RET_QK_DIM // 2, dtype=F32))
    return jnp.repeat(inv, 2)


def _ret_log_decay():
    return jnp.log(1.0 - 2.0 ** (-5.0 - jnp.arange(RET_HEADS, dtype=F32)))


def _ret_tables(L):
    lg = _ret_log_decay()
    idx = jnp.arange(L, dtype=F32)
    diff = idx[:, None] - idx[None, :]
    causal = diff >= 0
    dmat = jnp.where(causal[None], jnp.exp(lg[:, None, None] * jnp.where(causal, diff, 0.0)[None]), 0.0)
    xi = jnp.exp(lg[:, None] * (idx[None, :] + 1.0))
    zeta = jnp.exp(lg[:, None] * (L - 1.0 - idx[None, :]))
    decay = jnp.exp(lg * L)
    rep = lambda a: jnp.broadcast_to(a[:, :, None], (RET_HEADS, L, LANES))
    return dmat, rep(xi), rep(zeta), decay


def _lanes(t, width):
    return jnp.concatenate([t] * (width // LANES), axis=1)


def _group_norm_gate(o, gate):
    mu = jnp.mean(o, axis=-1, keepdims=True)
    d = o - mu
    var = jnp.mean(d * d, axis=-1, keepdims=True)
    return gate * (d * lax.rsqrt(var + LN_EPS))


def _retention_head(qh, kh, vh, s_old, dmat, xi, zeta, decay):
    att = lax.dot_general(qh, kh.astype(BF16), NT_DIMS, preferred_element_type=F32) * dmat
    o = jnp.dot(att.astype(BF16), vh, preferred_element_type=F32)
    o = o + jnp.dot(qh, s_old.astype(BF16), preferred_element_type=F32) * _lanes(xi, RET_V_DIM)
    kz = (kh * _lanes(zeta, RET_QK_DIM)).astype(BF16)
    upd = lax.dot_general(kz, vh, TN_DIMS, preferred_element_type=F32)
    return o, decay * s_old + upd


RET_SAMPLE_BATCH = DEC_BATCH // (SEQ // RET_CHUNK)


def _ret_kernel(q_ref, k_ref, v_ref, g_ref, dmat_ref, xi_ref, zeta_ref, decay_ref,
                x_ref, wo_ref, lg_ref, lb_ref,
                qs_ref, ks_ref, vs_ref, gs_ref, s_in_ref, sdmat_ref, sxi_ref, szeta_ref, sdecay_ref,
                xo_ref, sfin_ref, ys_ref, s_out_ref, s_ref, ys_scr):
    i = pl.program_id(0)

    @pl.when(i == 0)
    def _():
        s_ref[...] = jnp.zeros_like(s_ref)

    L = DEC_SEQ
    qa = qs_ref[...].astype(F32)
    va = vs_ref[...].astype(F32)
    ga = gs_ref[...].astype(F32)
    mixed = None
    for h in range(RET_HEADS):
        qk = slice(h * RET_QK_DIM, (h + 1) * RET_QK_DIM)
        vg = slice(h * RET_V_DIM, (h + 1) * RET_V_DIM)
        o, s_new = _retention_head(q_ref[:, qk], k_ref[:, qk], v_ref[:, vg], s_ref[h],
                                   dmat_ref[h], xi_ref[h], zeta_ref[h], decay_ref[h])
        s_ref[h] = s_new
        yh = _group_norm_gate(o, g_ref[:, vg]).astype(BF16)
        d = jnp.dot(yh, wo_ref[vg, :].astype(BF16), preferred_element_type=F32)
        mixed = d if mixed is None else mixed + d
        for b in range(RET_SAMPLE_BATCH):
            rows = slice(b * L, (b + 1) * L)
            o, s_new = _retention_head(qa[rows, qk].astype(BF16), ks_ref[rows, qk], va[rows, vg].astype(BF16),
                                       s_in_ref[b, h], sdmat_ref[h], sxi_ref[h], szeta_ref[h], sdecay_ref[h])
            s_out_ref[b, h] = s_new
            ys_scr[rows, vg] = _group_norm_gate(o, ga[rows, vg])
    ys_ref[...] = ys_scr[...].astype(BF16)
    xo_ref[...] = _layer_norm(ALPHA * x_ref[...] + mixed, lg_ref[...], lb_ref[...])

    @pl.when(i == pl.num_programs(0) - 1)
    def _():
        sfin_ref[...] = s_ref[...]


def _ret(q, k, v, g, state, tables, sample_tables, x, w_o, ln):
    nc = SEQ // RET_CHUNK
    bb = RET_SAMPLE_BATCH
    rows = bb * DEC_SEQ
    cur = lambda i: (i, 0)
    tok = lambda i: (i + SEQ // rows, 0)
    st = lambda i: (i, 0, 0, 0)
    state_shape = (RET_HEADS, RET_QK_DIM, RET_V_DIM)
    state_block = (bb,) + state_shape
    table_specs = lambda n: [_resident((RET_HEADS, n, n)), _resident((RET_HEADS, n, LANES)),
                             _resident((RET_HEADS, n, LANES)), _SMEM]
    return pl.pallas_call(
        _ret_kernel,
        grid=(nc,),
        in_specs=[pl.BlockSpec((RET_CHUNK, RET_QK), cur), pl.BlockSpec((RET_CHUNK, RET_QK), cur),
                  pl.BlockSpec((RET_CHUNK, RET_V), cur), pl.BlockSpec((RET_CHUNK, RET_V), cur),
                  *table_specs(RET_CHUNK),
                  pl.BlockSpec((RET_CHUNK, D_MODEL), cur), _layer_weight((RET_V, D_MODEL), 0),
                  _resident((1, D_MODEL)), _resident((1, D_MODEL)),
                  pl.BlockSpec((rows, RET_QK), tok), pl.BlockSpec((rows, RET_QK), tok),
                  pl.BlockSpec((rows, RET_V), tok), pl.BlockSpec((rows, RET_V), tok),
                  pl.BlockSpec(state_block, st), *table_specs(DEC_SEQ)],
        out_specs=[pl.BlockSpec((RET_CHUNK, D_MODEL), cur), pl.BlockSpec(state_shape, lambda i: (0, 0, 0)),
                   pl.BlockSpec((rows, RET_V), cur), pl.BlockSpec(state_block, st)],
        out_shape=[jax.ShapeDtypeStruct((SEQ, D_MODEL), F32), jax.ShapeDtypeStruct(state_shape, F32),
                   jax.ShapeDtypeStruct((N_SAMPLE, RET_V), BF16),
                   jax.ShapeDtypeStruct((DEC_BATCH,) + state_shape, F32)],
        scratch_shapes=[pltpu.VMEM(state_shape, F32), pltpu.VMEM((rows, RET_V), F32)],
        compiler_params=_params("arbitrary"),
        name="ret",
    )(q, k, v, g, *tables, x, w_o, *ln, q, k, v, g, state, *sample_tables)


def kernel(x_prompt, x_sample, cache_k_win, cache_v_win, state_ret, ffn1_w_gate, ffn1_w_up, ffn1_w_down,
           ffn2_w_gate, ffn2_w_up, ffn2_w_down, ln_g, ln_b, attn_w_qkv, attn_w_o, attn_sinks, ret_w_in,
           ret_w_o):
    ln = lambda i, j: (ln_g[i, j][None, :], ln_b[i, j][None, :])
    ffn1 = (ffn1_w_gate, ffn1_w_up, ffn1_w_down)
    ffn2 = (ffn2_w_gate, ffn2_w_up, ffn2_w_down)

    x = _half_ffn((x_prompt.reshape(SEQ, D_MODEL), x_sample.reshape(N_SAMPLE, D_MODEL)), ffn1, 0, ln(0, 0))
    w_qkv = jnp.concatenate([_pair_heads(attn_w_qkv[0][:, :D_MODEL], 1), attn_w_qkv[0][:, D_MODEL:]],
                            axis=1).astype(BF16)
    w_o = _pair_heads(attn_w_o, 1).astype(BF16)
    q, k, v = _qkv_proj(x, w_qkv, _angle_tables(_rope_inv_lane(), QKV_TILE))
    sinks = attn_sinks[0]
    kv_major = lambda c: jnp.transpose(c[0], (0, 2, 3, 1)).reshape(DEC_BATCH, KV_DIM, WINDOW)
    o_p, o_s, nk_s, nv_s = _swa(q, k, v, kv_major(cache_k_win), kv_major(cache_v_win), sinks)
    x = _half_ffn((x,), ffn2, 0, ln(0, 2), mix=((o_p, o_s), w_o, ln(0, 1)))

    kv_shape = (1, 1, WINDOW, N_KV_HEADS, HEAD_DIM)
    new_k_p = k[SEQ - WINDOW:SEQ].reshape(kv_shape)
    new_v_p = v[SEQ - WINDOW:SEQ].reshape(kv_shape)
    new_k_s = nk_s.reshape(1, DEC_BATCH, WINDOW, N_KV_HEADS, HEAD_DIM)
    new_v_s = nv_s.reshape(1, DEC_BATCH, WINDOW, N_KV_HEADS, HEAD_DIM)

    x = _half_ffn((x,), ffn1, 1, ln(1, 0))
    rq, rk, rv, rg = _ret_proj(x, ret_w_in, _angle_tables(_xpos_inv_lane()))
    x_p, s_p, y_s, s_s = _ret(rq, rk, rv, rg, state_ret[0], _ret_tables(RET_CHUNK), _ret_tables(DEC_SEQ),
                              x, ret_w_o, ln(1, 1))
    x_s = _mix_out_sample(y_s, x, ret_w_o, ln(1, 1))
    y_p, y_s = _half_ffn((x_p, x_s), ffn2, 1, ln(1, 2), split_out=True)

    return (y_p.reshape(1, SEQ, D_MODEL), y_s.reshape(DEC_BATCH, DEC_SEQ, D_MODEL),
            new_k_p, new_v_p, new_k_s, new_v_s, s_p[None, None], s_s[None])
```

```python
import functools
import math

import jax
import jax.numpy as jnp
import numpy as np
from jax import lax
from jax.experimental import pallas as pl
from jax.experimental.pallas import tpu as pltpu

F32 = jnp.float32
BF16 = jnp.bfloat16

D_MODEL = 1024
SEQ = 16384
DEPTH = 2
DEC_BATCH = 128
DEC_SEQ = 8
PAST_LEN = 16384
N_HEADS = 16
N_KV_HEADS = 4
GROUP = N_HEADS // N_KV_HEADS
HEAD_DIM = D_MODEL // N_HEADS
WINDOW = 128
ROPE_THETA = 10000.0
ATTN_SCALE = 1.0 / math.sqrt(HEAD_DIM)
RET_HEADS = 4
RET_QK_DIM = D_MODEL // RET_HEADS
RET_V_DIM = 2 * D_MODEL // RET_HEADS
RET_CHUNK = 256
RET_THETA = 10000.0
FFN_DIM = 2816
LN_EPS = 1e-5
ALPHA = (2.0 * DEPTH) ** 0.25

N_SAMPLE = DEC_BATCH * DEC_SEQ
N_TOK = SEQ + N_SAMPLE
KV_DIM = N_KV_HEADS * HEAD_DIM
RET_QK = RET_HEADS * RET_QK_DIM
RET_V = RET_HEADS * RET_V_DIM

LANES = 128
ROW_TILE = 512
FFN_TILE = 512
FFN_CHUNK = 256
VMEM_LIMIT = 62 * 1024 * 1024

N_TILES = N_TOK // ROW_TILE
N_PROMPT_TILES = SEQ // ROW_TILE
FFN_TILES = N_TOK // FFN_TILE
FFN_PROMPT_TILES = SEQ // FFN_TILE
KV_PAIRS = N_KV_HEADS // 2

NT_DIMS = (((1,), (1,)), ((), ()))
TN_DIMS = (((0,), (0,)), ((), ()))


def _params(*sem):
    return pltpu.CompilerParams(dimension_semantics=sem, vmem_limit_bytes=VMEM_LIMIT)


def _resident(shape):
    return pl.BlockSpec(shape, lambda *_: (0,) * len(shape), pipeline_mode=pl.Buffered(1))


def _rows(width, offset_blocks=0, tile=ROW_TILE):
    return pl.BlockSpec((tile, width), lambda i: (i + offset_blocks, 0))


_SMEM = pl.BlockSpec(memory_space=pltpu.SMEM)


def _layer_norm(y, g, b):
    mu = jnp.mean(y, axis=-1, keepdims=True)
    d = y - mu
    var = jnp.mean(d * d, axis=-1, keepdims=True)
    return d * lax.rsqrt(var + LN_EPS) * g + b


def _silu(x):
    return x / (1.0 + jnp.exp(-x))


FFN_CHUNKS = FFN_DIM // FFN_CHUNK


def _ffn_weight_copies(layer, hbm_refs, vmem_refs, sem):
    wg_hbm, wu_hbm, wd_hbm = hbm_refs
    wg_ref, wu_ref, wd_ref = vmem_refs
    copies = []
    for c in range(FFN_CHUNKS):
        sl = pl.ds(c * FFN_CHUNK, FFN_CHUNK)
        copies.append((pltpu.make_async_copy(wg_hbm.at[layer, :, sl], wg_ref.at[:, sl], sem.at[0, c]),
                       pltpu.make_async_copy(wu_hbm.at[layer, :, sl], wu_ref.at[:, sl], sem.at[1, c]),
                       pltpu.make_async_copy(wd_hbm.at[layer, sl, :], wd_ref.at[sl, :], sem.at[2, c])))
    return copies


def _ffn_kernel(split_in, split_out, mix, layer, *refs):
    n_in = 2 if split_in else 1
    x_refs, refs = refs[:n_in], refs[n_in:]
    if mix:
        (yp_ref, ys_ref, wo_ref, mg_ref, mb_ref), refs = refs[:5], refs[5:]
    n_out = 2 if split_out else 1
    hbm_refs, (g_ref, b_ref), o_refs = refs[:3], refs[3:5], refs[5:5 + n_out]
    wg_ref, wu_ref, wd_ref, sem = refs[5 + n_out:]
    i = pl.program_id(0)
    copies = _ffn_weight_copies(layer, hbm_refs, (wg_ref, wu_ref, wd_ref), sem)

    def tile(first_step):
        x = x_refs[0][...]
        if split_in:
            x = jnp.where(i < FFN_PROMPT_TILES, x, x_refs[1][...])
        if mix:
            y = jnp.where(i < FFN_PROMPT_TILES, yp_ref[...], ys_ref[...].astype(BF16))
            m = jnp.dot(y, wo_ref[...].astype(BF16), preferred_element_type=F32)
            x = _layer_norm(ALPHA * x + m, mg_ref[...], mb_ref[...])
        xb = x.astype(BF16)
        acc = None
        for c in range(FFN_CHUNKS):
            if first_step:
                for cp in copies[c]:
                    cp.wait()
            sl = slice(c * FFN_CHUNK, (c + 1) * FFN_CHUNK)
            gate = jnp.dot(xb, wg_ref[:, sl].astype(BF16), preferred_element_type=F32)
            up = jnp.dot(xb, wu_ref[:, sl].astype(BF16), preferred_element_type=F32)
            h = (_silu(gate) * up).astype(BF16)
            d = jnp.dot(h, wd_ref[sl, :].astype(BF16), preferred_element_type=F32)
            acc = d if acc is None else acc + d
        y = _layer_norm(ALPHA * x + 0.5 * acc, g_ref[...], b_ref[...])
        if split_out:
            @pl.when(i < FFN_PROMPT_TILES)
            def _():
                o_refs[0][...] = y

            @pl.when(i >= FFN_PROMPT_TILES)
            def _():
                o_refs[1][...] = y
        else:
            o_refs[0][...] = y

    @pl.when(i == 0)
    def _():
        for chunk in copies:
            for cp in chunk:
                cp.start()
        tile(True)

    @pl.when(i > 0)
    def _():
        tile(False)


_PROMPT_TILE = lambda i: (jnp.minimum(i, FFN_PROMPT_TILES - 1), 0)
_SAMPLE_TILE = lambda i: (jnp.maximum(i - FFN_PROMPT_TILES, 0), 0)


def _layer_weight(shape, layer):
    return pl.BlockSpec((None,) + shape, lambda i: (layer, 0, 0), pipeline_mode=pl.Buffered(1))


def _half_ffn(xs, weights, layer, ln, *, split_out=False, mix=None):
    split_in = len(xs) == 2
    tile = (FFN_TILE, D_MODEL)
    x_specs = ([pl.BlockSpec(tile, _PROMPT_TILE), pl.BlockSpec(tile, _SAMPLE_TILE)]
               if split_in else [_rows(D_MODEL, tile=FFN_TILE)])
    mix_args = ()
    if mix is not None:
        ys, w_o, mix_ln = mix
        k = w_o.shape[1]
        x_specs += [pl.BlockSpec((FFN_TILE, k), _PROMPT_TILE), pl.BlockSpec((FFN_TILE, k), _SAMPLE_TILE),
                    _layer_weight((k, D_MODEL), 0), _resident((1, D_MODEL)), _resident((1, D_MODEL))]
        mix_args = (*ys, w_o, *mix_ln)
    if split_out:
        out_specs = [pl.BlockSpec(tile, _PROMPT_TILE), pl.BlockSpec(tile, _SAMPLE_TILE)]
        out_shape = [jax.ShapeDtypeStruct((SEQ, D_MODEL), F32), jax.ShapeDtypeStruct((N_SAMPLE, D_MODEL), F32)]
    else:
        out_specs = _rows(D_MODEL, tile=FFN_TILE)
        out_shape = jax.ShapeDtypeStruct((N_TOK, D_MODEL), F32)
    hbm = pl.BlockSpec(memory_space=pl.ANY)
    return pl.pallas_call(
        functools.partial(_ffn_kernel, split_in, split_out, mix is not None, layer),
        grid=(FFN_TILES,),
        in_specs=x_specs + [hbm, hbm, hbm, _resident((1, D_MODEL)), _resident((1, D_MODEL))],
        out_specs=out_specs,
        out_shape=out_shape,
        scratch_shapes=[pltpu.VMEM((D_MODEL, FFN_DIM), F32), pltpu.VMEM((D_MODEL, FFN_DIM), F32),
                        pltpu.VMEM((FFN_DIM, D_MODEL), F32), pltpu.SemaphoreType.DMA((3, FFN_CHUNKS))],
        compiler_params=_params("arbitrary"),
        name="half_ffn",
    )(*xs, *mix_args, *weights, *ln)


def _mix_out_kernel(y_ref, x_ref, w_ref, g_ref, b_ref, o_ref):
    m = jnp.dot(y_ref[...], w_ref[...].astype(BF16), preferred_element_type=F32)
    o_ref[...] = _layer_norm(ALPHA * x_ref[...] + m, g_ref[...], b_ref[...])


def _mix_out_sample(y, x, w, ln):
    k = y.shape[1]
    return pl.pallas_call(
        _mix_out_kernel,
        grid=(N_TILES - N_PROMPT_TILES,),
        in_specs=[_rows(k), _rows(D_MODEL, N_PROMPT_TILES), _layer_weight((k, D_MODEL), 0),
                  _resident((1, D_MODEL)), _resident((1, D_MODEL))],
        out_specs=_rows(D_MODEL),
        out_shape=jax.ShapeDtypeStruct((N_SAMPLE, D_MODEL), F32),
        compiler_params=_params("parallel"),
        name="mix_out",
    )(y, x, w, *ln)


def _angle_tables(inv_lane, tile=ROW_TILE):
    base = jnp.concatenate([jnp.arange(SEQ // tile) * tile,
                            jnp.full((N_SAMPLE // tile,), PAST_LEN)]).astype(F32)
    row = jnp.stack([jnp.arange(tile), jnp.arange(tile) % DEC_SEQ]).astype(F32)
    ta = base[:, None] * inv_lane[None, :]
    ra = row[:, :, None] * inv_lane[None, None, :]
    return jnp.stack([jnp.cos(ta), jnp.sin(ta)], axis=1), jnp.stack([jnp.cos(ra), jnp.sin(ra)], axis=1)


def _table_specs(width, tile=ROW_TILE, first_tile=0):
    return [pl.BlockSpec((1, 2, width), lambda i: (i + first_tile, 0, 0)),
            pl.BlockSpec((1, 2, tile, width), lambda i: ((i + first_tile) // (SEQ // tile), 0, 0, 0))]


def _tile_cos_sin(tt_ref, rt_ref):
    cb, sb = tt_ref[0, 0:1, :], tt_ref[0, 1:2, :]
    cr, sr = rt_ref[0, 0], rt_ref[0, 1]
    return cb * cr - sb * sr, sb * cr + cb * sr


def _rotate_blocks(h, n_blocks, cos, sin, shift, partner_above):
    width = cos.shape[1]
    sign = jnp.where(partner_above, -1.0, 1.0).astype(F32)
    sin = sin * jnp.concatenate([sign] * (width // LANES), axis=1)
    out = []
    for j in range(n_blocks):
        t = (j * LANES) % width
        blk = h[:, j * LANES:(j + 1) * LANES]
        partner = jnp.where(partner_above, pltpu.roll(blk, LANES - shift, 1), pltpu.roll(blk, shift, 1))
        out.append(blk * cos[:, t:t + LANES] + partner * sin[:, t:t + LANES])
    return out


def _qkv_kernel(x_ref, w_ref, tt_ref, rt_ref, q_ref, k_ref, v_ref):
    xb = x_ref[...].astype(BF16)
    h = jnp.dot(xb, w_ref[...].astype(BF16), preferred_element_type=F32)
    cos, sin = _tile_cos_sin(tt_ref, rt_ref)
    lane = lax.broadcasted_iota(jnp.int32, (1, LANES), 1)
    n_rot = (D_MODEL + KV_DIM) // LANES
    rot = _rotate_blocks(h, n_rot, cos, sin, HEAD_DIM // 2, (lane % HEAD_DIM) < HEAD_DIM // 2)
    nq = D_MODEL // LANES
    for j in range(nq):
        q_ref[:, j * LANES:(j + 1) * LANES] = (rot[j] * ATTN_SCALE).astype(BF16)
    for j in range(nq, n_rot):
        k_ref[:, (j - nq) * LANES:(j - nq + 1) * LANES] = rot[j]
    v_ref[...] = h[:, D_MODEL + KV_DIM:]


QKV_TILE = 1024


def _qkv_proj(x, w, tables):
    n = x.shape[0]
    width = D_MODEL + 2 * KV_DIM
    rows = functools.partial(_rows, tile=QKV_TILE)
    return pl.pallas_call(
        _qkv_kernel,
        grid=(n // QKV_TILE,),
        in_specs=[rows(D_MODEL), _resident((D_MODEL, width)), *_table_specs(LANES, tile=QKV_TILE)],
        out_specs=[rows(D_MODEL), rows(KV_DIM), rows(KV_DIM)],
        out_shape=[jax.ShapeDtypeStruct((n, D_MODEL), BF16), jax.ShapeDtypeStruct((n, KV_DIM), F32),
                   jax.ShapeDtypeStruct((n, KV_DIM), F32)],
        compiler_params=_params("parallel"),
        name="attn_qkv",
    )(x, w, *tables)


def _rope_inv_lane():
    inv = ROPE_THETA ** (-jnp.arange(0, HEAD_DIM, 2, dtype=F32) / HEAD_DIM)
    return jnp.tile(inv, LANES // (HEAD_DIM // 2))


def _pair_heads(w, axis):
    shape = w.shape
    w = w.reshape(shape[:axis] + (KV_PAIRS, 2, GROUP, HEAD_DIM) + shape[axis + 1:])
    return jnp.swapaxes(w, axis + 1, axis + 2).reshape(shape)


def _pair_operands(kall, vall):
    s = kall.shape[0]
    low = lax.broadcasted_iota(jnp.int32, (s, LANES), 1) < HEAD_DIM
    kb, vb = kall.astype(BF16), vall.astype(BF16)
    zero = jnp.zeros_like(kb)
    one_lo = jnp.where(low, 1.0, 0.0).astype(BF16)
    one_hi = jnp.where(low, 0.0, 1.0).astype(BF16)
    kcat = jnp.concatenate([jnp.where(low, kb, zero), jnp.where(low, zero, kb)], axis=0)
    vcat = jnp.concatenate([jnp.concatenate([jnp.where(low, vb, zero), one_lo], axis=1),
                            jnp.concatenate([jnp.where(low, zero, vb), one_hi], axis=1)], axis=0)
    return kcat, vcat


def _pair_softmax(s, sink_a, sink_b):
    half = s.shape[1] // 2
    sa, sb = s[:, :half], s[:, half:]
    ma = jnp.maximum(jnp.max(sa, axis=-1, keepdims=True), sink_a)
    mb = jnp.maximum(jnp.max(sb, axis=-1, keepdims=True), sink_b)
    e = jnp.concatenate([jnp.exp(sa - ma), jnp.exp(sb - mb)], axis=1).astype(BF16)
    low = lax.broadcasted_iota(jnp.int32, (s.shape[0], LANES), 1) < HEAD_DIM
    return e, jnp.where(low, jnp.exp(sink_a - ma), jnp.exp(sink_b - mb))


SWA_SAMPLE_KEYS = 2 * WINDOW


SWA_STEP = 2


def _swa_kernel(sinks_ref, bias_ref, sbias_ref, q_ref, kp_ref, kc_ref, vp_ref, vc_ref,
                qs_ref, kn_ref, vn_ref, ck_ref, cv_ref,
                o_ref, os_ref, nk_ref, nv_ref, s_ref, e_ref, t_ref, os_scr):
    first = jnp.minimum(pl.program_id(0), 1)
    qall = qs_ref[...].astype(F32)
    for j in range(SWA_STEP):
        seq_j = slice(j * DEC_SEQ, (j + 1) * DEC_SEQ)
        _swa_sample_seq(sinks_ref, sbias_ref[...], qall[seq_j], kn_ref[seq_j, :], vn_ref[seq_j, :],
                        ck_ref[j].T, cv_ref[j].T, os_scr.at[seq_j], nk_ref.at[j], nv_ref.at[j])
        rows_j = slice(j * WINDOW, (j + 1) * WINDOW)
        bias = bias_ref[first] if j == 0 else bias_ref[1]
        for p in range(KV_PAIRS):
            lanes = slice(p * LANES, (p + 1) * LANES)
            k_prev = kp_ref[:, lanes] if j == 0 else kc_ref[(j - 1) * WINDOW:j * WINDOW, lanes]
            v_prev = vp_ref[:, lanes] if j == 0 else vc_ref[(j - 1) * WINDOW:j * WINDOW, lanes]
            kcat, vcat = _pair_operands(jnp.concatenate([k_prev, kc_ref[rows_j, lanes]], axis=0),
                                        jnp.concatenate([v_prev, vc_ref[rows_j, lanes]], axis=0))
            q = q_ref[rows_j, p * GROUP * LANES:(p + 1) * GROUP * LANES]
            qs = jnp.concatenate([q[:, g * LANES:(g + 1) * LANES] for g in range(GROUP)], axis=0)
            c = j * KV_PAIRS + p
            s_ref[c] = lax.dot_general(qs, kcat, NT_DIMS, preferred_element_type=F32)
            for g in range(GROUP):
                rows = slice(g * WINDOW, (g + 1) * WINDOW)
                head_a = (2 * p) * GROUP + g
                e, t = _pair_softmax(s_ref[c, rows, :] + bias, sinks_ref[head_a], sinks_ref[head_a + GROUP])
                e_ref[c, rows, :] = e
                t_ref[c, rows, :] = t
            r = jnp.dot(e_ref[c], vcat, preferred_element_type=F32)
            o = r[:, :LANES] / (r[:, LANES:] + t_ref[c])
            for g in range(GROUP):
                blk = p * GROUP + g
                o_ref[rows_j, blk * LANES:(blk + 1) * LANES] = o[g * WINDOW:(g + 1) * WINDOW].astype(BF16)
    os_ref[...] = os_scr[...].astype(BF16)


def _window_bias(n_rows, row_of, key_dist_and_pos):
    dist, kpos = key_dist_and_pos(row_of(np.arange(n_rows))[:, None])
    valid = (dist >= 0) & (dist < WINDOW) & (kpos >= 0)
    m = np.where(valid, 0.0, -np.inf).astype(np.float32)
    return np.concatenate([m, m], axis=1)


def _swa_sample_seq(sinks_ref, bias, qb, kn, vn, kc, vc, o_ref, nk_ref, nv_ref):
    L = DEC_SEQ
    pad = jnp.zeros((SWA_SAMPLE_KEYS - WINDOW - L, LANES), F32)
    nk_ref[0:WINDOW - L, :] = kc[L:, :]
    nk_ref[WINDOW - L:WINDOW, :] = kn
    nv_ref[0:WINDOW - L, :] = vc[L:, :]
    nv_ref[WINDOW - L:WINDOW, :] = vn
    for p in range(KV_PAIRS):
        lanes = slice(p * LANES, (p + 1) * LANES)
        kcat, vcat = _pair_operands(jnp.concatenate([kc[:, lanes], kn[:, lanes], pad], axis=0),
                                    jnp.concatenate([vc[:, lanes], vn[:, lanes], pad], axis=0))
        qs = jnp.concatenate(
            [qb[:, (p * GROUP + g) * LANES:(p * GROUP + g + 1) * LANES] for g in range(GROUP)],
            axis=0).astype(BF16)
        sink_col = lambda first: jnp.concatenate(
            [jnp.full((L, 1), sinks_ref[first + g], F32) for g in range(GROUP)], axis=0)
        s = lax.dot_general(qs, kcat, NT_DIMS, preferred_element_type=F32) + bias
        e, t = _pair_softmax(s, sink_col(2 * p * GROUP), sink_col((2 * p + 1) * GROUP))
        r = jnp.dot(e, vcat, preferred_element_type=F32)
        o = r[:, :LANES] / (r[:, LANES:] + t)
        for g in range(GROUP):
            blk = p * GROUP + g
            o_ref[:, blk * LANES:(blk + 1) * LANES] = o[g * L:(g + 1) * L]


def _swa(q, k, v, cache_k, cache_v, sinks):
    assert SEQ // WINDOW == DEC_BATCH
    rows = GROUP * WINDOW
    blk_rows, seq_rows = SWA_STEP * WINDOW, SWA_STEP * DEC_SEQ
    si = np.arange(2 * WINDOW)[None, :]
    bias = jnp.asarray(np.stack([
        _window_bias(WINDOW, lambda r: r, lambda qi: (qi + WINDOW - si, blk * WINDOW - WINDOW + si))
        for blk in (0, 1)]))
    cj = np.arange(SWA_SAMPLE_KEYS)[None, :]
    sbias = jnp.asarray(_window_bias(
        GROUP * DEC_SEQ, lambda r: r % DEC_SEQ,
        lambda qi: (np.where(cj < WINDOW + DEC_SEQ, PAST_LEN + qi - (PAST_LEN - WINDOW + cj), -1),
                    PAST_LEN - WINDOW + cj)))
    cur = lambda i: (i, 0)
    prev = lambda i: (jnp.maximum(SWA_STEP * i - 1, 0), 0)
    seq = lambda i: (i + SEQ // seq_rows, 0)
    cache = lambda i: (i, 0, 0)
    n_sets = SWA_STEP * KV_PAIRS
    return pl.pallas_call(
        _swa_kernel,
        grid=(DEC_BATCH // SWA_STEP,),
        in_specs=[_SMEM, _resident((2, WINDOW, 4 * WINDOW)),
                  _resident((GROUP * DEC_SEQ, 2 * SWA_SAMPLE_KEYS)),
                  pl.BlockSpec((blk_rows, D_MODEL), cur),
                  pl.BlockSpec((WINDOW, KV_DIM), prev), pl.BlockSpec((blk_rows, KV_DIM), cur),
                  pl.BlockSpec((WINDOW, KV_DIM), prev), pl.BlockSpec((blk_rows, KV_DIM), cur),
                  pl.BlockSpec((seq_rows, D_MODEL), seq),
                  pl.BlockSpec((seq_rows, KV_DIM), seq), pl.BlockSpec((seq_rows, KV_DIM), seq),
                  pl.BlockSpec((SWA_STEP, KV_DIM, WINDOW), cache), pl.BlockSpec((SWA_STEP, KV_DIM, WINDOW), cache)],
        out_specs=[pl.BlockSpec((blk_rows, D_MODEL), cur), pl.BlockSpec((seq_rows, D_MODEL), cur),
                   pl.BlockSpec((SWA_STEP, WINDOW, KV_DIM), cache), pl.BlockSpec((SWA_STEP, WINDOW, KV_DIM), cache)],
        out_shape=[jax.ShapeDtypeStruct((SEQ, D_MODEL), BF16), jax.ShapeDtypeStruct((N_SAMPLE, D_MODEL), BF16),
                   jax.ShapeDtypeStruct((DEC_BATCH, WINDOW, KV_DIM), F32),
                   jax.ShapeDtypeStruct((DEC_BATCH, WINDOW, KV_DIM), F32)],
        scratch_shapes=[pltpu.VMEM((n_sets, rows, 4 * WINDOW), F32),
                        pltpu.VMEM((n_sets, rows, 4 * WINDOW), BF16),
                        pltpu.VMEM((n_sets, rows, LANES), F32),
                        pltpu.VMEM((seq_rows, D_MODEL), F32)],
        compiler_params=_params("parallel"),
        name="swa",
    )(sinks, bias, sbias, q, k, k, v, v, q, k, v, cache_k, cache_v)


RET_IN_WIDTH = 2 * RET_QK + 2 * RET_V


def _ret_project(x, w_ref, tt_ref, rt_ref):
    xb = x.astype(BF16)
    hqk = jnp.dot(xb, w_ref[:, :2 * RET_QK].astype(BF16), preferred_element_type=F32)
    cos, sin = _tile_cos_sin(tt_ref, rt_ref)
    lane = lax.broadcasted_iota(jnp.int32, (1, LANES), 1)
    rot = _rotate_blocks(hqk, 2 * RET_QK // LANES, cos, sin, 1, (lane % 2) == 0)
    nq = RET_QK // LANES
    q = jnp.concatenate(rot[:nq], axis=1).astype(BF16)
    k = jnp.concatenate(rot[nq:], axis=1) * (RET_QK_DIM ** -0.5)
    v = jnp.dot(xb, w_ref[:, 2 * RET_QK:2 * RET_QK + RET_V].astype(BF16),
                preferred_element_type=F32).astype(BF16)
    gate = _silu(jnp.dot(xb, w_ref[:, 2 * RET_QK + RET_V:].astype(BF16), preferred_element_type=F32)).astype(BF16)
    return q, k, v, gate


def _ret_proj_kernel(x_ref, w_ref, tt_ref, rt_ref, q_ref, k_ref, v_ref, g_ref):
    q_ref[...], k_ref[...], v_ref[...], g_ref[...] = _ret_project(x_ref[...], w_ref, tt_ref, rt_ref)


def _ret_proj(x, w, tables):
    n = x.shape[0]
    return pl.pallas_call(
        _ret_proj_kernel,
        grid=(n // ROW_TILE,),
        in_specs=[_rows(D_MODEL), _layer_weight((D_MODEL, RET_IN_WIDTH), 0), *_table_specs(RET_QK_DIM)],
        out_specs=[_rows(RET_QK), _rows(RET_QK), _rows(RET_V), _rows(RET_V)],
        out_shape=[jax.ShapeDtypeStruct((n, RET_QK), BF16), jax.ShapeDtypeStruct((n, RET_QK), F32),
                   jax.ShapeDtypeStruct((n, RET_V), BF16), jax.ShapeDtypeStruct((n, RET_V), BF16)],
        compiler_params=_params("parallel"),
        name="ret_proj",
    )(x, w, *tables)


def _xpos_inv_lane():
    inv = 1.0 / (RET_THETA ** jnp.linspace(0.0, 1.0, RET_QK_DIM // 2, dtype=F32))
    return jnp.repeat(inv, 2)


def _ret_log_decay():
    return jnp.log(1.0 - 2.0 ** (-5.0 - jnp.arange(RET_HEADS, dtype=F32)))


def _ret_tables(L):
    lg = _ret_log_decay()
    idx = jnp.arange(L, dtype=F32)
    diff = idx[:, None] - idx[None, :]
    causal = diff >= 0
    dmat = jnp.where(causal[None], jnp.exp(lg[:, None, None] * jnp.where(causal, diff, 0.0)[None]), 0.0)
    xi = jnp.exp(lg[:, None] * (idx[None, :] + 1.0))
    zeta = jnp.exp(lg[:, None] * (L - 1.0 - idx[None, :]))
    decay = jnp.exp(lg * L)
    rep = lambda a: jnp.broadcast_to(a[:, :, None], (RET_HEADS, L, LANES))
    return dmat, rep(xi), rep(zeta), decay


def _lanes(t, width):
    return jnp.concatenate([t] * (width // LANES), axis=1)


def _group_norm_gate(o, gate):
    mu = jnp.mean(o, axis=-1, keepdims=True)
    d = o - mu
    var = jnp.mean(d * d, axis=-1, keepdims=True)
    return gate * (d * lax.rsqrt(var + LN_EPS))


def _retention_head(qh, kh, vh, s_old, dmat, xi, zeta, decay):
    att = lax.dot_general(qh, kh.astype(BF16), NT_DIMS, preferred_element_type=F32) * dmat
    o = jnp.dot(att.astype(BF16), vh, preferred_element_type=F32)
    o = o + jnp.dot(qh, s_old.astype(BF16), preferred_element_type=F32) * _lanes(xi, RET_V_DIM)
    kz = (kh * _lanes(zeta, RET_QK_DIM)).astype(BF16)
    upd = lax.dot_general(kz, vh, TN_DIMS, preferred_element_type=F32)
    return o, decay * s_old + upd


RET_SAMPLE_BATCH = DEC_BATCH // (SEQ // RET_CHUNK)


def _ret_kernel(q_ref, k_ref, v_ref, g_ref, dmat_ref, xi_ref, zeta_ref, decay_ref,
                x_ref, wo_ref, lg_ref, lb_ref,
                qs_ref, ks_ref, vs_ref, gs_ref, s_in_ref, sdmat_ref, sxi_ref, szeta_ref, sdecay_ref,
                xo_ref, sfin_ref, ys_ref, s_out_ref, s_ref, ys_scr):
    i = pl.program_id(0)

    @pl.when(i == 0)
    def _():
        s_ref[...] = jnp.zeros_like(s_ref)

    L = DEC_SEQ
    qa = qs_ref[...].astype(F32)
    va = vs_ref[...].astype(F32)
    ga = gs_ref[...].astype(F32)
    mixed = None
    for h in range(RET_HEADS):
        qk = slice(h * RET_QK_DIM, (h + 1) * RET_QK_DIM)
        vg = slice(h * RET_V_DIM, (h + 1) * RET_V_DIM)
        o, s_new = _retention_head(q_ref[:, qk], k_ref[:, qk], v_ref[:, vg], s_ref[h],
                                   dmat_ref[h], xi_ref[h], zeta_ref[h], decay_ref[h])
        s_ref[h] = s_new
        yh = _group_norm_gate(o, g_ref[:, vg]).astype(BF16)
        d = jnp.dot(yh, wo_ref[vg, :].astype(BF16), preferred_element_type=F32)
        mixed = d if mixed is None else mixed + d
        for b in range(RET_SAMPLE_BATCH):
            rows = slice(b * L, (b + 1) * L)
            o, s_new = _retention_head(qa[rows, qk].astype(BF16), ks_ref[rows, qk], va[rows, vg].astype(BF16),
                                       s_in_ref[b, h], sdmat_ref[h], sxi_ref[h], szeta_ref[h], sdecay_ref[h])
            s_out_ref[b, h] = s_new
            ys_scr[rows, vg] = _group_norm_gate(o, ga[rows, vg])
    ys_ref[...] = ys_scr[...].astype(BF16)
    xo_ref[...] = _layer_norm(ALPHA * x_ref[...] + mixed, lg_ref[...], lb_ref[...])

    @pl.when(i == pl.num_programs(0) - 1)
    def _():
        sfin_ref[...] = s_ref[...]


def _ret(q, k, v, g, state, tables, sample_tables, x, w_o, ln):
    nc = SEQ // RET_CHUNK
    bb = RET_SAMPLE_BATCH
    rows = bb * DEC_SEQ
    cur = lambda i: (i, 0)
    tok = lambda i: (i + SEQ // rows, 0)
    st = lambda i: (i, 0, 0, 0)
    state_shape = (RET_HEADS, RET_QK_DIM, RET_V_DIM)
    state_block = (bb,) + state_shape
    table_specs = lambda n: [_resident((RET_HEADS, n, n)), _resident((RET_HEADS, n, LANES)),
                             _resident((RET_HEADS, n, LANES)), _SMEM]
    return pl.pallas_call(
        _ret_kernel,
        grid=(nc,),
        in_specs=[pl.BlockSpec((RET_CHUNK, RET_QK), cur), pl.BlockSpec((RET_CHUNK, RET_QK), cur),
                  pl.BlockSpec((RET_CHUNK, RET_V), cur), pl.BlockSpec((RET_CHUNK, RET_V), cur),
                  *table_specs(RET_CHUNK),
                  pl.BlockSpec((RET_CHUNK, D_MODEL), cur), _layer_weight((RET_V, D_MODEL), 0),
                  _resident((1, D_MODEL)), _resident((1, D_MODEL)),
                  pl.BlockSpec((rows, RET_QK), tok), pl.BlockSpec((rows, RET_QK), tok),
                  pl.BlockSpec((rows, RET_V), tok), pl.BlockSpec((rows, RET_V), tok),
                  pl.BlockSpec(state_block, st), *table_specs(DEC_SEQ)],
        out_specs=[pl.BlockSpec((RET_CHUNK, D_MODEL), cur), pl.BlockSpec(state_shape, lambda i: (0, 0, 0)),
                   pl.BlockSpec((rows, RET_V), cur), pl.BlockSpec(state_block, st)],
        out_shape=[jax.ShapeDtypeStruct((SEQ, D_MODEL), F32), jax.ShapeDtypeStruct(state_shape, F32),
                   jax.ShapeDtypeStruct((N_SAMPLE, RET_V), BF16),
                   jax.ShapeDtypeStruct((DEC_BATCH,) + state_shape, F32)],
        scratch_shapes=[pltpu.VMEM(state_shape, F32), pltpu.VMEM((rows, RET_V), F32)],
        compiler_params=_params("arbitrary"),
        name="ret",
    )(q, k, v, g, *tables, x, w_o, *ln, q, k, v, g, state, *sample_tables)


def kernel(x_prompt, x_sample, cache_k_win, cache_v_win, state_ret, ffn1_w_gate, ffn1_w_up, ffn1_w_down,
           ffn2_w_gate, ffn2_w_up, ffn2_w_down, ln_g, ln_b, attn_w_qkv, attn_w_o, attn_sinks, ret_w_in,
           ret_w_o):
    ln = lambda i, j: (ln_g[i, j][None, :], ln_b[i, j][None, :])
    ffn1 = (ffn1_w_gate, ffn1_w_up, ffn1_w_down)
    ffn2 = (ffn2_w_gate, ffn2_w_up, ffn2_w_down)

    x = _half_ffn((x_prompt.reshape(SEQ, D_MODEL), x_sample.reshape(N_SAMPLE, D_MODEL)), ffn1, 0, ln(0, 0))
    w_qkv = jnp.concatenate([_pair_heads(attn_w_qkv[0][:, :D_MODEL], 1), attn_w_qkv[0][:, D_MODEL:]],
                            axis=1).astype(BF16)
    w_o = _pair_heads(attn_w_o, 1).astype(BF16)
    q, k, v = _qkv_proj(x, w_qkv, _angle_tables(_rope_inv_lane(), QKV_TILE))
    sinks = attn_sinks[0]
    kv_major = lambda c: jnp.transpose(c[0], (0, 2, 3, 1)).reshape(DEC_BATCH, KV_DIM, WINDOW)
    o_p, o_s, nk_s, nv_s = _swa(q, k, v, kv_major(cache_k_win), kv_major(cache_v_win), sinks)
    x = _half_ffn((x,), ffn2, 0, ln(0, 2), mix=((o_p, o_s), w_o, ln(0, 1)))

    kv_shape = (1, 1, WINDOW, N_KV_HEADS, HEAD_DIM)
    new_k_p = k[SEQ - WINDOW:SEQ].reshape(kv_shape)
    new_v_p = v[SEQ - WINDOW:SEQ].reshape(kv_shape)
    new_k_s = nk_s.reshape(1, DEC_BATCH, WINDOW, N_KV_HEADS, HEAD_DIM)
    new_v_s = nv_s.reshape(1, DEC_BATCH, WINDOW, N_KV_HEADS, HEAD_DIM)

    x = _half_ffn((x,), ffn1, 1, ln(1, 0))
    rq, rk, rv, rg = _ret_proj(x, ret_w_in, _angle_tables(_xpos_inv_lane()))
    x_p, s_p, y_s, s_s = _ret(rq, rk, rv, rg, state_ret[0], _ret_tables(RET_CHUNK), _ret_tables(DEC_SEQ),
                              x, ret_w_o, ln(1, 1))
    x_s = _mix_out_sample(y_s, x, ret_w_o, ln(1, 1))
    y_p, y_s = _half_ffn((x_p, x_s), ffn2, 1, ln(1, 2), split_out=True)

    return (y_p.reshape(1, SEQ, D_MODEL), y_s.reshape(DEC_BATCH, DEC_SEQ, D_MODEL),
            new_k_p, new_v_p, new_k_s, new_v_s, s_p[None, None], s_s[None])
```

```python
import functools
import math

import jax
import jax.numpy as jnp
import numpy as np
from jax import lax
from jax.experimental import pallas as pl
from jax.experimental.pallas import tpu as pltpu

F32 = jnp.float32
BF16 = jnp.bfloat16

D_MODEL = 1024
SEQ = 16384
DEPTH = 2
DEC_BATCH = 128
DEC_SEQ = 8
PAST_LEN = 16384
N_HEADS = 16
N_KV_HEADS = 4
GROUP = N_HEADS // N_KV_HEADS
HEAD_DIM = D_MODEL // N_HEADS
WINDOW = 128
ROPE_THETA = 10000.0
ATTN_SCALE = 1.0 / math.sqrt(HEAD_DIM)
RET_HEADS = 4
RET_QK_DIM = D_MODEL // RET_HEADS
RET_V_DIM = 2 * D_MODEL // RET_HEADS
RET_CHUNK = 256
RET_THETA = 10000.0
FFN_DIM = 2816
LN_EPS = 1e-5
ALPHA = (2.0 * DEPTH) ** 0.25

N_SAMPLE = DEC_BATCH * DEC_SEQ
N_TOK = SEQ + N_SAMPLE
KV_DIM = N_KV_HEADS * HEAD_DIM
RET_QK = RET_HEADS * RET_QK_DIM
RET_V = RET_HEADS * RET_V_DIM

LANES = 128
ROW_TILE = 512
FFN_TILE = 512
FFN_CHUNK = 256
VMEM_LIMIT = 62 * 1024 * 1024

N_TILES = N_TOK // ROW_TILE
N_PROMPT_TILES = SEQ // ROW_TILE
FFN_TILES = N_TOK // FFN_TILE
FFN_PROMPT_TILES = SEQ // FFN_TILE
KV_PAIRS = N_KV_HEADS // 2

NT_DIMS = (((1,), (1,)), ((), ()))
TN_DIMS = (((0,), (0,)), ((), ()))


def _params(*sem):
    return pltpu.CompilerParams(dimension_semantics=sem, vmem_limit_bytes=VMEM_LIMIT)


def _resident(shape):
    return pl.BlockSpec(shape, lambda *_: (0,) * len(shape), pipeline_mode=pl.Buffered(1))


def _rows(width, offset_blocks=0, tile=ROW_TILE):
    return pl.BlockSpec((tile, width), lambda i: (i + offset_blocks, 0))


_SMEM = pl.BlockSpec(memory_space=pltpu.SMEM)


def _layer_norm(y, g, b):
    mu = jnp.mean(y, axis=-1, keepdims=True)
    d = y - mu
    var = jnp.mean(d * d, axis=-1, keepdims=True)
    return d * lax.rsqrt(var + LN_EPS) * g + b


def _silu(x):
    return x / (1.0 + jnp.exp(-x))


FFN_CHUNKS = FFN_DIM // FFN_CHUNK


def _ffn_weight_copies(layer, hbm_refs, vmem_refs, sem):
    wg_hbm, wu_hbm, wd_hbm = hbm_refs
    wg_ref, wu_ref, wd_ref = vmem_refs
    copies = []
    for c in range(FFN_CHUNKS):
        sl = pl.ds(c * FFN_CHUNK, FFN_CHUNK)
        copies.append((pltpu.make_async_copy(wg_hbm.at[layer, :, sl], wg_ref.at[:, sl], sem.at[0, c]),
                       pltpu.make_async_copy(wu_hbm.at[layer, :, sl], wu_ref.at[:, sl], sem.at[1, c]),
                       pltpu.make_async_copy(wd_hbm.at[layer, sl, :], wd_ref.at[sl, :], sem.at[2, c])))
    return copies


def _ffn_kernel(split_in, split_out, mix, layer, *refs):
    n_in = 2 if split_in else 1
    x_refs, refs = refs[:n_in], refs[n_in:]
    if mix:
        (yp_ref, ys_ref, wo_ref, mg_ref, mb_ref), refs = refs[:5], refs[5:]
    n_out = 2 if split_out else 1
    hbm_refs, (g_ref, b_ref), o_refs = refs[:3], refs[3:5], refs[5:5 + n_out]
    wg_ref, wu_ref, wd_ref, sem = refs[5 + n_out:]
    i = pl.program_id(0)
    copies = _ffn_weight_copies(layer, hbm_refs, (wg_ref, wu_ref, wd_ref), sem)

    def tile(first_step):
        x = x_refs[0][...]
        if split_in:
            x = jnp.where(i < FFN_PROMPT_TILES, x, x_refs[1][...])
        if mix:
            y = jnp.where(i < FFN_PROMPT_TILES, yp_ref[...], ys_ref[...].astype(BF16))
            m = jnp.dot(y, wo_ref[...].astype(BF16), preferred_element_type=F32)
            x = _layer_norm(ALPHA * x + m, mg_ref[...], mb_ref[...])
        xb = x.astype(BF16)
        acc = None
        for c in range(FFN_CHUNKS):
            if first_step:
                for cp in copies[c]:
                    cp.wait()
            sl = slice(c * FFN_CHUNK, (c + 1) * FFN_CHUNK)
            gate = jnp.dot(xb, wg_ref[:, sl].astype(BF16), preferred_element_type=F32)
            up = jnp.dot(xb, wu_ref[:, sl].astype(BF16), preferred_element_type=F32)
            h = (_silu(gate) * up).astype(BF16)
            d = jnp.dot(h, wd_ref[sl, :].astype(BF16), preferred_element_type=F32)
            acc = d if acc is None else acc + d
        y = _layer_norm(ALPHA * x + 0.5 * acc, g_ref[...], b_ref[...])
        if split_out:
            @pl.when(i < FFN_PROMPT_TILES)
            def _():
                o_refs[0][...] = y

            @pl.when(i >= FFN_PROMPT_TILES)
            def _():
                o_refs[1][...] = y
        else:
            o_refs[0][...] = y

    @pl.when(i == 0)
    def _():
        for chunk in copies:
            for cp in chunk:
                cp.start()
        tile(True)

    @pl.when(i > 0)
    def _():
        tile(False)


_PROMPT_TILE = lambda i: (jnp.minimum(i, FFN_PROMPT_TILES - 1), 0)
_SAMPLE_TILE = lambda i: (jnp.maximum(i - FFN_PROMPT_TILES, 0), 0)


def _layer_weight(shape, layer):
    return pl.BlockSpec((None,) + shape, lambda i: (layer, 0, 0), pipeline_mode=pl.Buffered(1))


def _half_ffn(xs, weights, layer, ln, *, split_out=False, mix=None):
    split_in = len(xs) == 2
    tile = (FFN_TILE, D_MODEL)
    x_specs = ([pl.BlockSpec(tile, _PROMPT_TILE), pl.BlockSpec(tile, _SAMPLE_TILE)]
               if split_in else [_rows(D_MODEL, tile=FFN_TILE)])
    mix_args = ()
    if mix is not None:
        ys, w_o, mix_ln = mix
        k = w_o.shape[1]
        x_specs += [pl.BlockSpec((FFN_TILE, k), _PROMPT_TILE), pl.BlockSpec((FFN_TILE, k), _SAMPLE_TILE),
                    _layer_weight((k, D_MODEL), 0), _resident((1, D_MODEL)), _resident((1, D_MODEL))]
        mix_args = (*ys, w_o, *mix_ln)
    if split_out:
        out_specs = [pl.BlockSpec(tile, _PROMPT_TILE), pl.BlockSpec(tile, _SAMPLE_TILE)]
        out_shape = [jax.ShapeDtypeStruct((SEQ, D_MODEL), F32), jax.ShapeDtypeStruct((N_SAMPLE, D_MODEL), F32)]
    else:
        out_specs = _rows(D_MODEL, tile=FFN_TILE)
        out_shape = jax.ShapeDtypeStruct((N_TOK, D_MODEL), F32)
    hbm = pl.BlockSpec(memory_space=pl.ANY)
    return pl.pallas_call(
        functools.partial(_ffn_kernel, split_in, split_out, mix is not None, layer),
        grid=(FFN_TILES,),
        in_specs=x_specs + [hbm, hbm, hbm, _resident((1, D_MODEL)), _resident((1, D_MODEL))],
        out_specs=out_specs,
        out_shape=out_shape,
        scratch_shapes=[pltpu.VMEM((D_MODEL, FFN_DIM), F32), pltpu.VMEM((D_MODEL, FFN_DIM), F32),
                        pltpu.VMEM((FFN_DIM, D_MODEL), F32), pltpu.SemaphoreType.DMA((3, FFN_CHUNKS))],
        compiler_params=_params("arbitrary"),
        name="half_ffn",
    )(*xs, *mix_args, *weights, *ln)


def _mix_out_kernel(y_ref, x_ref, w_ref, g_ref, b_ref, o_ref):
    m = jnp.dot(y_ref[...], w_ref[...].astype(BF16), preferred_element_type=F32)
    o_ref[...] = _layer_norm(ALPHA * x_ref[...] + m, g_ref[...], b_ref[...])


def _mix_out_sample(y, x, w, ln):
    k = y.shape[1]
    return pl.pallas_call(
        _mix_out_kernel,
        grid=(N_TILES - N_PROMPT_TILES,),
        in_specs=[_rows(k), _rows(D_MODEL, N_PROMPT_TILES), _layer_weight((k, D_MODEL), 0),
                  _resident((1, D_MODEL)), _resident((1, D_MODEL))],
        out_specs=_rows(D_MODEL),
        out_shape=jax.ShapeDtypeStruct((N_SAMPLE, D_MODEL), F32),
        compiler_params=_params("parallel"),
        name="mix_out",
    )(y, x, w, *ln)


def _angle_tables(inv_lane, tile=ROW_TILE):
    base = jnp.concatenate([jnp.arange(SEQ // tile) * tile,
                            jnp.full((N_SAMPLE // tile,), PAST_LEN)]).astype(F32)
    row = jnp.stack([jnp.arange(tile), jnp.arange(tile) % DEC_SEQ]).astype(F32)
    ta = base[:, None] * inv_lane[None, :]
    ra = row[:, :, None] * inv_lane[None, None, :]
    return jnp.stack([jnp.cos(ta), jnp.sin(ta)], axis=1), jnp.stack([jnp.cos(ra), jnp.sin(ra)], axis=1)


def _table_specs(width, tile=ROW_TILE, first_tile=0):
    return [pl.BlockSpec((1, 2, width), lambda i: (i + first_tile, 0, 0)),
            pl.BlockSpec((1, 2, tile, width), lambda i: ((i + first_tile) // (SEQ // tile), 0, 0, 0))]


def _tile_cos_sin(tt_ref, rt_ref):
    cb, sb = tt_ref[0, 0:1, :], tt_ref[0, 1:2, :]
    cr, sr = rt_ref[0, 0], rt_ref[0, 1]
    return cb * cr - sb * sr, sb * cr + cb * sr


def _rotate_blocks(h, n_blocks, cos, sin, shift, partner_above):
    width = cos.shape[1]
    sign = jnp.where(partner_above, -1.0, 1.0).astype(F32)
    sin = sin * jnp.concatenate([sign] * (width // LANES), axis=1)
    out = []
    for j in range(n_blocks):
        t = (j * LANES) % width
        blk = h[:, j * LANES:(j + 1) * LANES]
        partner = jnp.where(partner_above, pltpu.roll(blk, LANES - shift, 1), pltpu.roll(blk, shift, 1))
        out.append(blk * cos[:, t:t + LANES] + partner * sin[:, t:t + LANES])
    return out


def _qkv_kernel(x_ref, w_ref, tt_ref, rt_ref, q_ref, k_ref, v_ref):
    xb = x_ref[...].astype(BF16)
    h = jnp.dot(xb, w_ref[...].astype(BF16), preferred_element_type=F32)
    cos, sin = _tile_cos_sin(tt_ref, rt_ref)
    lane = lax.broadcasted_iota(jnp.int32, (1, LANES), 1)
    n_rot = (D_MODEL + KV_DIM) // LANES
    rot = _rotate_blocks(h, n_rot, cos, sin, HEAD_DIM // 2, (lane % HEAD_DIM) < HEAD_DIM // 2)
    nq = D_MODEL // LANES
    for j in range(nq):
        q_ref[:, j * LANES:(j + 1) * LANES] = (rot[j] * ATTN_SCALE).astype(BF16)
    for j in range(nq, n_rot):
        k_ref[:, (j - nq) * LANES:(j - nq + 1) * LANES] = rot[j]
    v_ref[...] = h[:, D_MODEL + KV_DIM:]


QKV_TILE = 1024


def _qkv_proj(x, w, tables):
    n = x.shape[0]
    width = D_MODEL + 2 * KV_DIM
    rows = functools.partial(_rows, tile=QKV_TILE)
    return pl.pallas_call(
        _qkv_kernel,
        grid=(n // QKV_TILE,),
        in_specs=[rows(D_MODEL), _resident((D_MODEL, width)), *_table_specs(LANES, tile=QKV_TILE)],
        out_specs=[rows(D_MODEL), rows(KV_DIM), rows(KV_DIM)],
        out_shape=[jax.ShapeDtypeStruct((n, D_MODEL), BF16), jax.ShapeDtypeStruct((n, KV_DIM), F32),
                   jax.ShapeDtypeStruct((n, KV_DIM), F32)],
        compiler_params=_params("parallel"),
        name="attn_qkv",
    )(x, w, *tables)


def _rope_inv_lane():
    inv = ROPE_THETA ** (-jnp.arange(0, HEAD_DIM, 2, dtype=F32) / HEAD_DIM)
    return jnp.tile(inv, LANES // (HEAD_DIM // 2))


def _pair_heads(w, axis):
    shape = w.shape
    w = w.reshape(shape[:axis] + (KV_PAIRS, 2, GROUP, HEAD_DIM) + shape[axis + 1:])
    return jnp.swapaxes(w, axis + 1, axis + 2).reshape(shape)


def _pair_operands(kall, vall):
    s = kall.shape[0]
    low = lax.broadcasted_iota(jnp.int32, (s, LANES), 1) < HEAD_DIM
    kb, vb = kall.astype(BF16), vall.astype(BF16)
    zero = jnp.zeros_like(kb)
    one_lo = jnp.where(low, 1.0, 0.0).astype(BF16)
    one_hi = jnp.where(low, 0.0, 1.0).astype(BF16)
    kcat = jnp.concatenate([jnp.where(low, kb, zero), jnp.where(low, zero, kb)], axis=0)
    vcat = jnp.concatenate([jnp.concatenate([jnp.where(low, vb, zero), one_lo], axis=1),
                            jnp.concatenate([jnp.where(low, zero, vb), one_hi], axis=1)], axis=0)
    return kcat, vcat


def _pair_softmax(s, sink_a, sink_b):
    half = s.shape[1] // 2
    sa, sb = s[:, :half], s[:, half:]
    ma = jnp.maximum(jnp.max(sa, axis=-1, keepdims=True), sink_a)
    mb = jnp.maximum(jnp.max(sb, axis=-1, keepdims=True), sink_b)
    e = jnp.concatenate([jnp.exp(sa - ma), jnp.exp(sb - mb)], axis=1).astype(BF16)
    low = lax.broadcasted_iota(jnp.int32, (s.shape[0], LANES), 1) < HEAD_DIM
    return e, jnp.where(low, jnp.exp(sink_a - ma), jnp.exp(sink_b - mb))


SWA_SAMPLE_KEYS = 2 * WINDOW


SWA_STEP = 2


def _swa_kernel(sinks_ref, bias_ref, sbias_ref, q_ref, kp_ref, kc_ref, vp_ref, vc_ref,
                qs_ref, kn_ref, vn_ref, ck_ref, cv_ref,
                o_ref, os_ref, nk_ref, nv_ref, s_ref, e_ref, t_ref, os_scr):
    first = jnp.minimum(pl.program_id(0), 1)
    qall = qs_ref[...].astype(F32)
    for j in range(SWA_STEP):
        seq_j = slice(j * DEC_SEQ, (j + 1) * DEC_SEQ)
        _swa_sample_seq(sinks_ref, sbias_ref[...], qall[seq_j], kn_ref[seq_j, :], vn_ref[seq_j, :],
                        ck_ref[j].T, cv_ref[j].T, os_scr.at[seq_j], nk_ref.at[j], nv_ref.at[j])
        rows_j = slice(j * WINDOW, (j + 1) * WINDOW)
        bias = bias_ref[first] if j == 0 else bias_ref[1]
        for p in range(KV_PAIRS):
            lanes = slice(p * LANES, (p + 1) * LANES)
            k_prev = kp_ref[:, lanes] if j == 0 else kc_ref[(j - 1) * WINDOW:j * WINDOW, lanes]
            v_prev = vp_ref[:, lanes] if j == 0 else vc_ref[(j - 1) * WINDOW:j * WINDOW, lanes]
            kcat, vcat = _pair_operands(jnp.concatenate([k_prev, kc_ref[rows_j, lanes]], axis=0),
                                        jnp.concatenate([v_prev, vc_ref[rows_j, lanes]], axis=0))
            q = q_ref[rows_j, p * GROUP * LANES:(p + 1) * GROUP * LANES]
            qs = jnp.concatenate([q[:, g * LANES:(g + 1) * LANES] for g in range(GROUP)], axis=0)
            c = j * KV_PAIRS + p
            s_ref[c] = lax.dot_general(qs, kcat, NT_DIMS, preferred_element_type=F32)
            for g in range(GROUP):
                rows = slice(g * WINDOW, (g + 1) * WINDOW)
                head_a = (2 * p) * GROUP + g
                e, t = _pair_softmax(s_ref[c, rows, :] + bias, sinks_ref[head_a], sinks_ref[head_a + GROUP])
                e_ref[c, rows, :] = e
                t_ref[c, rows, :] = t
            r = jnp.dot(e_ref[c], vcat, preferred_element_type=F32)
            o = r[:, :LANES] / (r[:, LANES:] + t_ref[c])
            for g in range(GROUP):
                blk = p * GROUP + g
                o_ref[rows_j, blk * LANES:(blk + 1) * LANES] = o[g * WINDOW:(g + 1) * WINDOW].astype(BF16)
    os_ref[...] = os_scr[...].astype(BF16)


def _window_bias(n_rows, row_of, key_dist_and_pos):
    dist, kpos = key_dist_and_pos(row_of(np.arange(n_rows))[:, None])
    valid = (dist >= 0) & (dist < WINDOW) & (kpos >= 0)
    m = np.where(valid, 0.0, -np.inf).astype(np.float32)
    return np.concatenate([m, m], axis=1)


def _swa_sample_seq(sinks_ref, bias, qb, kn, vn, kc, vc, o_ref, nk_ref, nv_ref):
    L = DEC_SEQ
    pad = jnp.zeros((SWA_SAMPLE_KEYS - WINDOW - L, LANES), F32)
    nk_ref[0:WINDOW - L, :] = kc[L:, :]
    nk_ref[WINDOW - L:WINDOW, :] = kn
    nv_ref[0:WINDOW - L, :] = vc[L:, :]
    nv_ref[WINDOW - L:WINDOW, :] = vn
    for p in range(KV_PAIRS):
        lanes = slice(p * LANES, (p + 1) * LANES)
        kcat, vcat = _pair_operands(jnp.concatenate([kc[:, lanes], kn[:, lanes], pad], axis=0),
                                    jnp.concatenate([vc[:, lanes], vn[:, lanes], pad], axis=0))
        qs = jnp.concatenate(
            [qb[:, (p * GROUP + g) * LANES:(p * GROUP + g + 1) * LANES] for g in range(GROUP)],
            axis=0).astype(BF16)
        sink_col = lambda first: jnp.concatenate(
            [jnp.full((L, 1), sinks_ref[first + g], F32) for g in range(GROUP)], axis=0)
        s = lax.dot_general(qs, kcat, NT_DIMS, preferred_element_type=F32) + bias
        e, t = _pair_softmax(s, sink_col(2 * p * GROUP), sink_col((2 * p + 1) * GROUP))
        r = jnp.dot(e, vcat, preferred_element_type=F32)
        o = r[:, :LANES] / (r[:, LANES:] + t)
        for g in range(GROUP):
            blk = p * GROUP + g
            o_ref[:, blk * LANES:(blk + 1) * LANES] = o[g * L:(g + 1) * L]


def _swa(q, k, v, cache_k, cache_v, sinks):
    assert SEQ // WINDOW == DEC_BATCH
    rows = GROUP * WINDOW
    blk_rows, seq_rows = SWA_STEP * WINDOW, SWA_STEP * DEC_SEQ
    si = np.arange(2 * WINDOW)[None, :]
    bias = jnp.asarray(np.stack([
        _window_bias(WINDOW, lambda r: r, lambda qi: (qi + WINDOW - si, blk * WINDOW - WINDOW + si))
        for blk in (0, 1)]))
    cj = np.arange(SWA_SAMPLE_KEYS)[None, :]
    sbias = jnp.asarray(_window_bias(
        GROUP * DEC_SEQ, lambda r: r % DEC_SEQ,
        lambda qi: (np.where(cj < WINDOW + DEC_SEQ, PAST_LEN + qi - (PAST_LEN - WINDOW + cj), -1),
                    PAST_LEN - WINDOW + cj)))
    cur = lambda i: (i, 0)
    prev = lambda i: (jnp.maximum(SWA_STEP * i - 1, 0), 0)
    seq = lambda i: (i + SEQ // seq_rows, 0)
    cache = lambda i: (i, 0, 0)
    n_sets = SWA_STEP * KV_PAIRS
    return pl.pallas_call(
        _swa_kernel,
        grid=(DEC_BATCH // SWA_STEP,),
        in_specs=[_SMEM, _resident((2, WINDOW, 4 * WINDOW)),
                  _resident((GROUP * DEC_SEQ, 2 * SWA_SAMPLE_KEYS)),
                  pl.BlockSpec((blk_rows, D_MODEL), cur),
                  pl.BlockSpec((WINDOW, KV_DIM), prev), pl.BlockSpec((blk_rows, KV_DIM), cur),
                  pl.BlockSpec((WINDOW, KV_DIM), prev), pl.BlockSpec((blk_rows, KV_DIM), cur),
                  pl.BlockSpec((seq_rows, D_MODEL), seq),
                  pl.BlockSpec((seq_rows, KV_DIM), seq), pl.BlockSpec((seq_rows, KV_DIM), seq),
                  pl.BlockSpec((SWA_STEP, KV_DIM, WINDOW), cache), pl.BlockSpec((SWA_STEP, KV_DIM, WINDOW), cache)],
        out_specs=[pl.BlockSpec((blk_rows, D_MODEL), cur), pl.BlockSpec((seq_rows, D_MODEL), cur),
                   pl.BlockSpec((SWA_STEP, WINDOW, KV_DIM), cache), pl.BlockSpec((SWA_STEP, WINDOW, KV_DIM), cache)],
        out_shape=[jax.ShapeDtypeStruct((SEQ, D_MODEL), BF16), jax.ShapeDtypeStruct((N_SAMPLE, D_MODEL), BF16),
                   jax.ShapeDtypeStruct((DEC_BATCH, WINDOW, KV_DIM), F32),
                   jax.ShapeDtypeStruct((DEC_BATCH, WINDOW, KV_DIM), F32)],
        scratch_shapes=[pltpu.VMEM((n_sets, rows, 4 * WINDOW), F32),
                        pltpu.VMEM((n_sets, rows, 4 * WINDOW), BF16),
                        pltpu.VMEM((n_sets, rows, LANES), F32),
                        pltpu.VMEM((seq_rows, D_MODEL), F32)],
        compiler_params=_params("parallel"),
        name="swa",
    )(sinks, bias, sbias, q, k, k, v, v, q, k, v, cache_k, cache_v)


RET_IN_WIDTH = 2 * RET_QK + 2 * RET_V


def _ret_project(x, w_ref, tt_ref, rt_ref):
    xb = x.astype(BF16)
    hqk = jnp.dot(xb, w_ref[:, :2 * RET_QK].astype(BF16), preferred_element_type=F32)
    cos, sin = _tile_cos_sin(tt_ref, rt_ref)
    lane = lax.broadcasted_iota(jnp.int32, (1, LANES), 1)
    rot = _rotate_blocks(hqk, 2 * RET_QK // LANES, cos, sin, 1, (lane % 2) == 0)
    nq = RET_QK // LANES
    q = jnp.concatenate(rot[:nq], axis=1).astype(BF16)
    k = jnp.concatenate(rot[nq:], axis=1) * (RET_QK_DIM ** -0.5)
    v = jnp.dot(xb, w_ref[:, 2 * RET_QK:2 * RET_QK + RET_V].astype(BF16),
                preferred_element_type=F32).astype(BF16)
    gate = _silu(jnp.dot(xb, w_ref[:, 2 * RET_QK + RET_V:].astype(BF16), preferred_element_type=F32)).astype(BF16)
    return q, k, v, gate


def _ret_proj_kernel(x_ref, w_ref, tt_ref, rt_ref, q_ref, k_ref, v_ref, g_ref):
    q_ref[...], k_ref[...], v_ref[...], g_ref[...] = _ret_project(x_ref[...], w_ref, tt_ref, rt_ref)


def _ret_proj(x, w, tables):
    n = x.shape[0]
    return pl.pallas_call(
        _ret_proj_kernel,
        grid=(n // ROW_TILE,),
        in_specs=[_rows(D_MODEL), _layer_weight((D_MODEL, RET_IN_WIDTH), 0), *_table_specs(RET_QK_DIM)],
        out_specs=[_rows(RET_QK), _rows(RET_QK), _rows(RET_V), _rows(RET_V)],
        out_shape=[jax.ShapeDtypeStruct((n, RET_QK), BF16), jax.ShapeDtypeStruct((n, RET_QK), F32),
                   jax.ShapeDtypeStruct((n, RET_V), BF16), jax.ShapeDtypeStruct((n, RET_V), BF16)],
        compiler_params=_params("parallel"),
        name="ret_proj",
    )(x, w, *tables)


def _xpos_inv_lane():
    inv = 1.0 / (RET_THETA ** jnp.linspace(0.0, 1.0, RET_QK_DIM // 2, dtype=F32))
    return jnp.repeat(inv, 2)


def _ret_log_decay():
    return jnp.log(1.0 - 2.0 ** (-5.0 - jnp.arange(RET_HEADS, dtype=F32)))


def _ret_tables(L):
    lg = _ret_log_decay()
    idx = jnp.arange(L, dtype=F32)
    diff = idx[:, None] - idx[None, :]
    causal = diff >= 0
    dmat = jnp.where(causal[None], jnp.exp(lg[:, None, None] * jnp.where(causal, diff, 0.0)[None]), 0.0)
    xi = jnp.exp(lg[:, None] * (idx[None, :] + 1.0))
    zeta = jnp.exp(lg[:, None] * (L - 1.0 - idx[None, :]))
    decay = jnp.exp(lg * L)
    rep = lambda a: jnp.broadcast_to(a[:, :, None], (RET_HEADS, L, LANES))
    return dmat, rep(xi), rep(zeta), decay


def _lanes(t, width):
    return jnp.concatenate([t] * (width // LANES), axis=1)


def _group_norm_gate(o, gate):
    mu = jnp.mean(o, axis=-1, keepdims=True)
    d = o - mu
    var = jnp.mean(d * d, axis=-1, keepdims=True)
    return gate * (d * lax.rsqrt(var + LN_EPS))


def _retention_head(qh, kh, vh, s_old, dmat, xi, zeta, decay):
    att = lax.dot_general(qh, kh.astype(BF16), NT_DIMS, preferred_element_type=F32) * dmat
    o = jnp.dot(att.astype(BF16), vh, preferred_element_type=F32)
    o = o + jnp.dot(qh, s_old.astype(BF16), preferred_element_type=F32) * _lanes(xi, RET_V_DIM)
    kz = (kh * _lanes(zeta, RET_QK_DIM)).astype(BF16)
    upd = lax.dot_general(kz, vh, TN_DIMS, preferred_element_type=F32)
    return o, decay * s_old + upd


RET_SAMPLE_BATCH = DEC_BATCH // (SEQ // RET_CHUNK)


def _ret_kernel(q_ref, k_ref, v_ref, g_ref, dmat_ref, xi_ref, zeta_ref, decay_ref,
                x_ref, wo_ref, lg_ref, lb_ref,
                qs_ref, ks_ref, vs_ref, gs_ref, s_in_ref, sdmat_ref, sxi_ref, szeta_ref, sdecay_ref,
                xo_ref, sfin_ref, ys_ref, s_out_ref, s_ref, ys_scr, wo_bf_ref):
    i = pl.program_id(0)

    @pl.when(i == 0)
    def _():
        s_ref[...] = jnp.zeros_like(s_ref)
        wo_bf_ref[...] = wo_ref[...].astype(BF16)

    L = DEC_SEQ
    qa = qs_ref[...].astype(F32)
    va = vs_ref[...].astype(F32)
    ga = gs_ref[...].astype(F32)
    mixed = None
    for h in range(RET_HEADS):
        qk = slice(h * RET_QK_DIM, (h + 1) * RET_QK_DIM)
        vg = slice(h * RET_V_DIM, (h + 1) * RET_V_DIM)
        o, s_new = _retention_head(q_ref[:, qk], k_ref[:, qk], v_ref[:, vg], s_ref[h],
                                   dmat_ref[h], xi_ref[h], zeta_ref[h], decay_ref[h])
        s_ref[h] = s_new
        yh = _group_norm_gate(o, g_ref[:, vg]).astype(BF16)
        d = jnp.dot(yh, wo_bf_ref[vg, :], preferred_element_type=F32)
        mixed = d if mixed is None else mixed + d
        for b in range(RET_SAMPLE_BATCH):
            rows = slice(b * L, (b + 1) * L)
            o, s_new = _retention_head(qa[rows, qk].astype(BF16), ks_ref[rows, qk], va[rows, vg].astype(BF16),
                                       s_in_ref[b, h], sdmat_ref[h], sxi_ref[h], szeta_ref[h], sdecay_ref[h])
            s_out_ref[b, h] = s_new
            ys_scr[rows, vg] = _group_norm_gate(o, ga[rows, vg])
    ys_ref[...] = ys_scr[...].astype(BF16)
    xo_ref[...] = _layer_norm(ALPHA * x_ref[...] + mixed, lg_ref[...], lb_ref[...])

    @pl.when(i == pl.num_programs(0) - 1)
    def _():
        sfin_ref[...] = s_ref[...]


def _ret(q, k, v, g, state, tables, sample_tables, x, w_o, ln):
    nc = SEQ // RET_CHUNK
    bb = RET_SAMPLE_BATCH
    rows = bb * DEC_SEQ
    cur = lambda i: (i, 0)
    tok = lambda i: (i + SEQ // rows, 0)
    st = lambda i: (i, 0, 0, 0)
    state_shape = (RET_HEADS, RET_QK_DIM, RET_V_DIM)
    state_block = (bb,) + state_shape
    table_specs = lambda n: [_resident((RET_HEADS, n, n)), _resident((RET_HEADS, n, LANES)),
                             _resident((RET_HEADS, n, LANES)), _SMEM]
    return pl.pallas_call(
        _ret_kernel,
        grid=(nc,),
        in_specs=[pl.BlockSpec((RET_CHUNK, RET_QK), cur), pl.BlockSpec((RET_CHUNK, RET_QK), cur),
                  pl.BlockSpec((RET_CHUNK, RET_V), cur), pl.BlockSpec((RET_CHUNK, RET_V), cur),
                  *table_specs(RET_CHUNK),
                  pl.BlockSpec((RET_CHUNK, D_MODEL), cur), _layer_weight((RET_V, D_MODEL), 0),
                  _resident((1, D_MODEL)), _resident((1, D_MODEL)),
                  pl.BlockSpec((rows, RET_QK), tok), pl.BlockSpec((rows, RET_QK), tok),
                  pl.BlockSpec((rows, RET_V), tok), pl.BlockSpec((rows, RET_V), tok),
                  pl.BlockSpec(state_block, st), *table_specs(DEC_SEQ)],
        out_specs=[pl.BlockSpec((RET_CHUNK, D_MODEL), cur), pl.BlockSpec(state_shape, lambda i: (0, 0, 0)),
                   pl.BlockSpec((rows, RET_V), cur), pl.BlockSpec(state_block, st)],
        out_shape=[jax.ShapeDtypeStruct((SEQ, D_MODEL), F32), jax.ShapeDtypeStruct(state_shape, F32),
                   jax.ShapeDtypeStruct((N_SAMPLE, RET_V), BF16),
                   jax.ShapeDtypeStruct((DEC_BATCH,) + state_shape, F32)],
        scratch_shapes=[pltpu.VMEM(state_shape, F32), pltpu.VMEM((rows, RET_V), F32),
                        pltpu.VMEM((RET_V, D_MODEL), BF16)],
        compiler_params=_params("arbitrary"),
        name="ret",
    )(q, k, v, g, *tables, x, w_o, *ln, q, k, v, g, state, *sample_tables)


def kernel(x_prompt, x_sample, cache_k_win, cache_v_win, state_ret, ffn1_w_gate, ffn1_w_up, ffn1_w_down,
           ffn2_w_gate, ffn2_w_up, ffn2_w_down, ln_g, ln_b, attn_w_qkv, attn_w_o, attn_sinks, ret_w_in,
           ret_w_o):
    ln = lambda i, j: (ln_g[i, j][None, :], ln_b[i, j][None, :])
    ffn1 = (ffn1_w_gate, ffn1_w_up, ffn1_w_down)
    ffn2 = (ffn2_w_gate, ffn2_w_up, ffn2_w_down)

    x = _half_ffn((x_prompt.reshape(SEQ, D_MODEL), x_sample.reshape(N_SAMPLE, D_MODEL)), ffn1, 0, ln(0, 0))
    w_qkv = jnp.concatenate([_pair_heads(attn_w_qkv[0][:, :D_MODEL], 1), attn_w_qkv[0][:, D_MODEL:]],
                            axis=1).astype(BF16)
    w_o = _pair_heads(attn_w_o, 1).astype(BF16)
    q, k, v = _qkv_proj(x, w_qkv, _angle_tables(_rope_inv_lane(), QKV_TILE))
    sinks = attn_sinks[0]
    kv_major = lambda c: jnp.transpose(c[0], (0, 2, 3, 1)).reshape(DEC_BATCH, KV_DIM, WINDOW)
    o_p, o_s, nk_s, nv_s = _swa(q, k, v, kv_major(cache_k_win), kv_major(cache_v_win), sinks)
    x = _half_ffn((x,), ffn2, 0, ln(0, 2), mix=((o_p, o_s), w_o, ln(0, 1)))

    kv_shape = (1, 1, WINDOW, N_KV_HEADS, HEAD_DIM)
    new_k_p = k[SEQ - WINDOW:SEQ].reshape(kv_shape)
    new_v_p = v[SEQ - WINDOW:SEQ].reshape(kv_shape)
    new_k_s = nk_s.reshape(1, DEC_BATCH, WINDOW, N_KV_HEADS, HEAD_DIM)
    new_v_s = nv_s.reshape(1, DEC_BATCH, WINDOW, N_KV_HEADS, HEAD_DIM)

    x = _half_ffn((x,), ffn1, 1, ln(1, 0))
    rq, rk, rv, rg = _ret_proj(x, ret_w_in, _angle_tables(_xpos_inv_lane()))
    x_p, s_p, y_s, s_s = _ret(rq, rk, rv, rg, state_ret[0], _ret_tables(RET_CHUNK), _ret_tables(DEC_SEQ),
                              x, ret_w_o, ln(1, 1))
    x_s = _mix_out_sample(y_s, x, ret_w_o, ln(1, 1))
    y_p, y_s = _half_ffn((x_p, x_s), ffn2, 1, ln(1, 2), split_out=True)

    return (y_p.reshape(1, SEQ, D_MODEL), y_s.reshape(DEC_BATCH, DEC_SEQ, D_MODEL),
            new_k_p, new_v_p, new_k_s, new_v_s, s_p[None, None], s_s[None])
```

```python
import functools
import math

import jax
import jax.numpy as jnp
import numpy as np
from jax import lax
from jax.experimental import pallas as pl
from jax.experimental.pallas import tpu as pltpu

F32 = jnp.float32
BF16 = jnp.bfloat16

D_MODEL = 1024
SEQ = 16384
DEPTH = 2
DEC_BATCH = 128
DEC_SEQ = 8
PAST_LEN = 16384
N_HEADS = 16
N_KV_HEADS = 4
GROUP = N_HEADS // N_KV_HEADS
HEAD_DIM = D_MODEL // N_HEADS
WINDOW = 128
ROPE_THETA = 10000.0
ATTN_SCALE = 1.0 / math.sqrt(HEAD_DIM)
RET_HEADS = 4
RET_QK_DIM = D_MODEL // RET_HEADS
RET_V_DIM = 2 * D_MODEL // RET_HEADS
RET_CHUNK = 256
RET_THETA = 10000.0
FFN_DIM = 2816
LN_EPS = 1e-5
ALPHA = (2.0 * DEPTH) ** 0.25

N_SAMPLE = DEC_BATCH * DEC_SEQ
N_TOK = SEQ + N_SAMPLE
KV_DIM = N_KV_HEADS * HEAD_DIM
RET_QK = RET_HEADS * RET_QK_DIM
RET_V = RET_HEADS * RET_V_DIM

LANES = 128
ROW_TILE = 512
FFN_TILE = 512
FFN_CHUNK = 256
VMEM_LIMIT = 62 * 1024 * 1024

N_TILES = N_TOK // ROW_TILE
N_PROMPT_TILES = SEQ // ROW_TILE
FFN_TILES = N_TOK // FFN_TILE
FFN_PROMPT_TILES = SEQ // FFN_TILE
KV_PAIRS = N_KV_HEADS // 2

NT_DIMS = (((1,), (1,)), ((), ()))
TN_DIMS = (((0,), (0,)), ((), ()))


def _params(*sem):
    return pltpu.CompilerParams(dimension_semantics=sem, vmem_limit_bytes=VMEM_LIMIT)


def _resident(shape):
    return pl.BlockSpec(shape, lambda *_: (0,) * len(shape), pipeline_mode=pl.Buffered(1))


def _rows(width, offset_blocks=0, tile=ROW_TILE):
    return pl.BlockSpec((tile, width), lambda i: (i + offset_blocks, 0))


_SMEM = pl.BlockSpec(memory_space=pltpu.SMEM)


def _layer_norm(y, g, b):
    mu = jnp.mean(y, axis=-1, keepdims=True)
    d = y - mu
    var = jnp.mean(d * d, axis=-1, keepdims=True)
    return d * lax.rsqrt(var + LN_EPS) * g + b


def _silu(x):
    return x / (1.0 + jnp.exp(-x))


FFN_CHUNKS = FFN_DIM // FFN_CHUNK


def _ffn_weight_copies(layer, hbm_refs, vmem_refs, sem):
    wg_hbm, wu_hbm, wd_hbm = hbm_refs
    wg_ref, wu_ref, wd_ref = vmem_refs
    copies = []
    for c in range(FFN_CHUNKS):
        sl = pl.ds(c * FFN_CHUNK, FFN_CHUNK)
        copies.append((pltpu.make_async_copy(wg_hbm.at[layer, :, sl], wg_ref.at[:, sl], sem.at[0, c]),
                       pltpu.make_async_copy(wu_hbm.at[layer, :, sl], wu_ref.at[:, sl], sem.at[1, c]),
                       pltpu.make_async_copy(wd_hbm.at[layer, sl, :], wd_ref.at[sl, :], sem.at[2, c])))
    return copies


def _ffn_kernel(split_in, split_out, mix, layer, *refs):
    n_in = 2 if split_in else 1
    x_refs, refs = refs[:n_in], refs[n_in:]
    if mix:
        (yp_ref, ys_ref, wo_ref, mg_ref, mb_ref), refs = refs[:5], refs[5:]
    n_out = 2 if split_out else 1
    hbm_refs, (g_ref, b_ref), o_refs = refs[:3], refs[3:5], refs[5:5 + n_out]
    wg_ref, wu_ref, wd_ref, sem = refs[5 + n_out:]
    i = pl.program_id(0)
    copies = _ffn_weight_copies(layer, hbm_refs, (wg_ref, wu_ref, wd_ref), sem)

    def tile(first_step):
        x = x_refs[0][...]
        if split_in:
            x = jnp.where(i < FFN_PROMPT_TILES, x, x_refs[1][...])
        if mix:
            y = jnp.where(i < FFN_PROMPT_TILES, yp_ref[...], ys_ref[...].astype(BF16))
            m = jnp.dot(y, wo_ref[...].astype(BF16), preferred_element_type=F32)
            x = _layer_norm(ALPHA * x + m, mg_ref[...], mb_ref[...])
        xb = x.astype(BF16)
        acc = None
        for c in range(FFN_CHUNKS):
            if first_step:
                for cp in copies[c]:
                    cp.wait()
            sl = slice(c * FFN_CHUNK, (c + 1) * FFN_CHUNK)
            gate = jnp.dot(xb, wg_ref[:, sl].astype(BF16), preferred_element_type=F32)
            up = jnp.dot(xb, wu_ref[:, sl].astype(BF16), preferred_element_type=F32)
            h = (_silu(gate) * up).astype(BF16)
            d = jnp.dot(h, wd_ref[sl, :].astype(BF16), preferred_element_type=F32)
            acc = d if acc is None else acc + d
        y = _layer_norm(ALPHA * x + 0.5 * acc, g_ref[...], b_ref[...])
        if split_out:
            @pl.when(i < FFN_PROMPT_TILES)
            def _():
                o_refs[0][...] = y

            @pl.when(i >= FFN_PROMPT_TILES)
            def _():
                o_refs[1][...] = y
        else:
            o_refs[0][...] = y

    @pl.when(i == 0)
    def _():
        for chunk in copies:
            for cp in chunk:
                cp.start()
        tile(True)

    @pl.when(i > 0)
    def _():
        tile(False)


_PROMPT_TILE = lambda i: (jnp.minimum(i, FFN_PROMPT_TILES - 1), 0)
_SAMPLE_TILE = lambda i: (jnp.maximum(i - FFN_PROMPT_TILES, 0), 0)


def _layer_weight(shape, layer):
    return pl.BlockSpec((None,) + shape, lambda i: (layer, 0, 0), pipeline_mode=pl.Buffered(1))


def _half_ffn(xs, weights, layer, ln, *, split_out=False, mix=None):
    split_in = len(xs) == 2
    tile = (FFN_TILE, D_MODEL)
    x_specs = ([pl.BlockSpec(tile, _PROMPT_TILE), pl.BlockSpec(tile, _SAMPLE_TILE)]
               if split_in else [_rows(D_MODEL, tile=FFN_TILE)])
    mix_args = ()
    if mix is not None:
        ys, w_o, mix_ln = mix
        k = w_o.shape[1]
        x_specs += [pl.BlockSpec((FFN_TILE, k), _PROMPT_TILE), pl.BlockSpec((FFN_TILE, k), _SAMPLE_TILE),
                    _layer_weight((k, D_MODEL), 0), _resident((1, D_MODEL)), _resident((1, D_MODEL))]
        mix_args = (*ys, w_o, *mix_ln)
    if split_out:
        out_specs = [pl.BlockSpec(tile, _PROMPT_TILE), pl.BlockSpec(tile, _SAMPLE_TILE)]
        out_shape = [jax.ShapeDtypeStruct((SEQ, D_MODEL), F32), jax.ShapeDtypeStruct((N_SAMPLE, D_MODEL), F32)]
    else:
        out_specs = _rows(D_MODEL, tile=FFN_TILE)
        out_shape = jax.ShapeDtypeStruct((N_TOK, D_MODEL), F32)
    hbm = pl.BlockSpec(memory_space=pl.ANY)
    return pl.pallas_call(
        functools.partial(_ffn_kernel, split_in, split_out, mix is not None, layer),
        grid=(FFN_TILES,),
        in_specs=x_specs + [hbm, hbm, hbm, _resident((1, D_MODEL)), _resident((1, D_MODEL))],
        out_specs=out_specs,
        out_shape=out_shape,
        scratch_shapes=[pltpu.VMEM((D_MODEL, FFN_DIM), F32), pltpu.VMEM((D_MODEL, FFN_DIM), F32),
                        pltpu.VMEM((FFN_DIM, D_MODEL), F32), pltpu.SemaphoreType.DMA((3, FFN_CHUNKS))],
        compiler_params=_params("arbitrary"),
        name="half_ffn",
    )(*xs, *mix_args, *weights, *ln)


def _mix_out_kernel(y_ref, x_ref, w_ref, g_ref, b_ref, o_ref):
    m = jnp.dot(y_ref[...], w_ref[...].astype(BF16), preferred_element_type=F32)
    o_ref[...] = _layer_norm(ALPHA * x_ref[...] + m, g_ref[...], b_ref[...])


def _mix_out_sample(y, x, w, ln):
    k = y.shape[1]
    return pl.pallas_call(
        _mix_out_kernel,
        grid=(N_TILES - N_PROMPT_TILES,),
        in_specs=[_rows(k), _rows(D_MODEL, N_PROMPT_TILES), _layer_weight((k, D_MODEL), 0),
                  _resident((1, D_MODEL)), _resident((1, D_MODEL))],
        out_specs=_rows(D_MODEL),
        out_shape=jax.ShapeDtypeStruct((N_SAMPLE, D_MODEL), F32),
        compiler_params=_params("parallel"),
        name="mix_out",
    )(y, x, w, *ln)


def _angle_tables(inv_lane, tile=ROW_TILE):
    base = jnp.concatenate([jnp.arange(SEQ // tile) * tile,
                            jnp.full((N_SAMPLE // tile,), PAST_LEN)]).astype(F32)
    row = jnp.stack([jnp.arange(tile), jnp.arange(tile) % DEC_SEQ]).astype(F32)
    ta = base[:, None] * inv_lane[None, :]
    ra = row[:, :, None] * inv_lane[None, None, :]
    return jnp.stack([jnp.cos(ta), jnp.sin(ta)], axis=1), jnp.stack([jnp.cos(ra), jnp.sin(ra)], axis=1)


def _table_specs(width, tile=ROW_TILE, first_tile=0):
    return [pl.BlockSpec((1, 2, width), lambda i: (i + first_tile, 0, 0)),
            pl.BlockSpec((1, 2, tile, width), lambda i: ((i + first_tile) // (SEQ // tile), 0, 0, 0))]


def _tile_cos_sin(tt_ref, rt_ref):
    cb, sb = tt_ref[0, 0:1, :], tt_ref[0, 1:2, :]
    cr, sr = rt_ref[0, 0], rt_ref[0, 1]
    return cb * cr - sb * sr, sb * cr + cb * sr


def _rotate_blocks(h, n_blocks, cos, sin, shift, partner_above):
    width = cos.shape[1]
    sign = jnp.where(partner_above, -1.0, 1.0).astype(F32)
    sin = sin * jnp.concatenate([sign] * (width // LANES), axis=1)
    out = []
    for j in range(n_blocks):
        t = (j * LANES) % width
        blk = h[:, j * LANES:(j + 1) * LANES]
        partner = jnp.where(partner_above, pltpu.roll(blk, LANES - shift, 1), pltpu.roll(blk, shift, 1))
        out.append(blk * cos[:, t:t + LANES] + partner * sin[:, t:t + LANES])
    return out


def _qkv_kernel(x_ref, w_ref, tt_ref, rt_ref, q_ref, k_ref, v_ref):
    xb = x_ref[...].astype(BF16)
    h = jnp.dot(xb, w_ref[...].astype(BF16), preferred_element_type=F32)
    cos, sin = _tile_cos_sin(tt_ref, rt_ref)
    lane = lax.broadcasted_iota(jnp.int32, (1, LANES), 1)
    n_rot = (D_MODEL + KV_DIM) // LANES
    rot = _rotate_blocks(h, n_rot, cos, sin, HEAD_DIM // 2, (lane % HEAD_DIM) < HEAD_DIM // 2)
    nq = D_MODEL // LANES
    for j in range(nq):
        q_ref[:, j * LANES:(j + 1) * LANES] = (rot[j] * ATTN_SCALE).astype(BF16)
    for j in range(nq, n_rot):
        k_ref[:, (j - nq) * LANES:(j - nq + 1) * LANES] = rot[j]
    v_ref[...] = h[:, D_MODEL + KV_DIM:]


QKV_TILE = 1024


def _qkv_proj(x, w, tables):
    n = x.shape[0]
    width = D_MODEL + 2 * KV_DIM
    rows = functools.partial(_rows, tile=QKV_TILE)
    return pl.pallas_call(
        _qkv_kernel,
        grid=(n // QKV_TILE,),
        in_specs=[rows(D_MODEL), _resident((D_MODEL, width)), *_table_specs(LANES, tile=QKV_TILE)],
        out_specs=[rows(D_MODEL), rows(KV_DIM), rows(KV_DIM)],
        out_shape=[jax.ShapeDtypeStruct((n, D_MODEL), BF16), jax.ShapeDtypeStruct((n, KV_DIM), F32),
                   jax.ShapeDtypeStruct((n, KV_DIM), F32)],
        compiler_params=_params("parallel"),
        name="attn_qkv",
    )(x, w, *tables)


def _rope_inv_lane():
    inv = ROPE_THETA ** (-jnp.arange(0, HEAD_DIM, 2, dtype=F32) / HEAD_DIM)
    return jnp.tile(inv, LANES // (HEAD_DIM // 2))


def _pair_heads(w, axis):
    shape = w.shape
    w = w.reshape(shape[:axis] + (KV_PAIRS, 2, GROUP, HEAD_DIM) + shape[axis + 1:])
    return jnp.swapaxes(w, axis + 1, axis + 2).reshape(shape)


def _pair_operands(kall, vall):
    s = kall.shape[0]
    low = lax.broadcasted_iota(jnp.int32, (s, LANES), 1) < HEAD_DIM
    kb, vb = kall.astype(BF16), vall.astype(BF16)
    zero = jnp.zeros_like(kb)
    one_lo = jnp.where(low, 1.0, 0.0).astype(BF16)
    one_hi = jnp.where(low, 0.0, 1.0).astype(BF16)
    kcat = jnp.concatenate([jnp.where(low, kb, zero), jnp.where(low, zero, kb)], axis=0)
    vcat = jnp.concatenate([jnp.concatenate([jnp.where(low, vb, zero), one_lo], axis=1),
                            jnp.concatenate([jnp.where(low, zero, vb), one_hi], axis=1)], axis=0)
    return kcat, vcat


def _pair_softmax(s, sink_a, sink_b):
    half = s.shape[1] // 2
    sa, sb = s[:, :half], s[:, half:]
    ma = jnp.maximum(jnp.max(sa, axis=-1, keepdims=True), sink_a)
    mb = jnp.maximum(jnp.max(sb, axis=-1, keepdims=True), sink_b)
    e = jnp.concatenate([jnp.exp(sa - ma), jnp.exp(sb - mb)], axis=1).astype(BF16)
    low = lax.broadcasted_iota(jnp.int32, (s.shape[0], LANES), 1) < HEAD_DIM
    return e, jnp.where(low, jnp.exp(sink_a - ma), jnp.exp(sink_b - mb))


SWA_SAMPLE_KEYS = 2 * WINDOW


SWA_STEP = 2


def _swa_kernel(sinks_ref, bias_ref, sbias_ref, q_ref, kp_ref, kc_ref, vp_ref, vc_ref,
                qs_ref, kn_ref, vn_ref, ck_ref, cv_ref,
                o_ref, os_ref, nk_ref, nv_ref, s_ref, e_ref, t_ref, os_scr):
    first = jnp.minimum(pl.program_id(0), 1)
    qall = qs_ref[...].astype(F32)
    for j in range(SWA_STEP):
        seq_j = slice(j * DEC_SEQ, (j + 1) * DEC_SEQ)
        _swa_sample_seq(sinks_ref, sbias_ref[...], qall[seq_j], kn_ref[seq_j, :], vn_ref[seq_j, :],
                        ck_ref[j].T, cv_ref[j].T, os_scr.at[seq_j], nk_ref.at[j], nv_ref.at[j])
        rows_j = slice(j * WINDOW, (j + 1) * WINDOW)
        bias = bias_ref[first] if j == 0 else bias_ref[1]
        for p in range(KV_PAIRS):
            lanes = slice(p * LANES, (p + 1) * LANES)
            k_prev = kp_ref[:, lanes] if j == 0 else kc_ref[(j - 1) * WINDOW:j * WINDOW, lanes]
            v_prev = vp_ref[:, lanes] if j == 0 else vc_ref[(j - 1) * WINDOW:j * WINDOW, lanes]
            kcat, vcat = _pair_operands(jnp.concatenate([k_prev, kc_ref[rows_j, lanes]], axis=0),
                                        jnp.concatenate([v_prev, vc_ref[rows_j, lanes]], axis=0))
            q = q_ref[rows_j, p * GROUP * LANES:(p + 1) * GROUP * LANES]
            qs = jnp.concatenate([q[:, g * LANES:(g + 1) * LANES] for g in range(GROUP)], axis=0)
            c = j * KV_PAIRS + p
            s_ref[c] = lax.dot_general(qs, kcat, NT_DIMS, preferred_element_type=F32)
            for g in range(GROUP):
                rows = slice(g * WINDOW, (g + 1) * WINDOW)
                head_a = (2 * p) * GROUP + g
                e, t = _pair_softmax(s_ref[c, rows, :] + bias, sinks_ref[head_a], sinks_ref[head_a + GROUP])
                e_ref[c, rows, :] = e
                t_ref[c, rows, :] = t
            r = jnp.dot(e_ref[c], vcat, preferred_element_type=F32)
            o = r[:, :LANES] / (r[:, LANES:] + t_ref[c])
            for g in range(GROUP):
                blk = p * GROUP + g
                o_ref[rows_j, blk * LANES:(blk + 1) * LANES] = o[g * WINDOW:(g + 1) * WINDOW].astype(BF16)
    os_ref[...] = os_scr[...].astype(BF16)


def _window_bias(n_rows, row_of, key_dist_and_pos):
    dist, kpos = key_dist_and_pos(row_of(np.arange(n_rows))[:, None])
    valid = (dist >= 0) & (dist < WINDOW) & (kpos >= 0)
    m = np.where(valid, 0.0, -np.inf).astype(np.float32)
    return np.concatenate([m, m], axis=1)


def _swa_sample_seq(sinks_ref, bias, qb, kn, vn, kc, vc, o_ref, nk_ref, nv_ref):
    L = DEC_SEQ
    pad = jnp.zeros((SWA_SAMPLE_KEYS - WINDOW - L, LANES), F32)
    nk_ref[0:WINDOW - L, :] = kc[L:, :]
    nk_ref[WINDOW - L:WINDOW, :] = kn
    nv_ref[0:WINDOW - L, :] = vc[L:, :]
    nv_ref[WINDOW - L:WINDOW, :] = vn
    for p in range(KV_PAIRS):
        lanes = slice(p * LANES, (p + 1) * LANES)
        kcat, vcat = _pair_operands(jnp.concatenate([kc[:, lanes], kn[:, lanes], pad], axis=0),
                                    jnp.concatenate([vc[:, lanes], vn[:, lanes], pad], axis=0))
        qs = jnp.concatenate(
            [qb[:, (p * GROUP + g) * LANES:(p * GROUP + g + 1) * LANES] for g in range(GROUP)],
            axis=0).astype(BF16)
        sink_col = lambda first: jnp.concatenate(
            [jnp.full((L, 1), sinks_ref[first + g], F32) for g in range(GROUP)], axis=0)
        s = lax.dot_general(qs, kcat, NT_DIMS, preferred_element_type=F32) + bias
        e, t = _pair_softmax(s, sink_col(2 * p * GROUP), sink_col((2 * p + 1) * GROUP))
        r = jnp.dot(e, vcat, preferred_element_type=F32)
        o = r[:, :LANES] / (r[:, LANES:] + t)
        for g in range(GROUP):
            blk = p * GROUP + g
            o_ref[:, blk * LANES:(blk + 1) * LANES] = o[g * L:(g + 1) * L]


def _swa(q, k, v, cache_k, cache_v, sinks):
    assert SEQ // WINDOW == DEC_BATCH
    rows = GROUP * WINDOW
    blk_rows, seq_rows = SWA_STEP * WINDOW, SWA_STEP * DEC_SEQ
    si = np.arange(2 * WINDOW)[None, :]
    bias = jnp.asarray(np.stack([
        _window_bias(WINDOW, lambda r: r, lambda qi: (qi + WINDOW - si, blk * WINDOW - WINDOW + si))
        for blk in (0, 1)]))
    cj = np.arange(SWA_SAMPLE_KEYS)[None, :]
    sbias = jnp.asarray(_window_bias(
        GROUP * DEC_SEQ, lambda r: r % DEC_SEQ,
        lambda qi: (np.where(cj < WINDOW + DEC_SEQ, PAST_LEN + qi - (PAST_LEN - WINDOW + cj), -1),
                    PAST_LEN - WINDOW + cj)))
    cur = lambda i: (i, 0)
    prev = lambda i: (jnp.maximum(SWA_STEP * i - 1, 0), 0)
    seq = lambda i: (i + SEQ // seq_rows, 0)
    cache = lambda i: (i, 0, 0)
    n_sets = SWA_STEP * KV_PAIRS
    return pl.pallas_call(
        _swa_kernel,
        grid=(DEC_BATCH // SWA_STEP,),
        in_specs=[_SMEM, _resident((2, WINDOW, 4 * WINDOW)),
                  _resident((GROUP * DEC_SEQ, 2 * SWA_SAMPLE_KEYS)),
                  pl.BlockSpec((blk_rows, D_MODEL), cur),
                  pl.BlockSpec((WINDOW, KV_DIM), prev), pl.BlockSpec((blk_rows, KV_DIM), cur),
                  pl.BlockSpec((WINDOW, KV_DIM), prev), pl.BlockSpec((blk_rows, KV_DIM), cur),
                  pl.BlockSpec((seq_rows, D_MODEL), seq),
                  pl.BlockSpec((seq_rows, KV_DIM), seq), pl.BlockSpec((seq_rows, KV_DIM), seq),
                  pl.BlockSpec((SWA_STEP, KV_DIM, WINDOW), cache), pl.BlockSpec((SWA_STEP, KV_DIM, WINDOW), cache)],
        out_specs=[pl.BlockSpec((blk_rows, D_MODEL), cur), pl.BlockSpec((seq_rows, D_MODEL), cur),
                   pl.BlockSpec((SWA_STEP, WINDOW, KV_DIM), cache), pl.BlockSpec((SWA_STEP, WINDOW, KV_DIM), cache)],
        out_shape=[jax.ShapeDtypeStruct((SEQ, D_MODEL), BF16), jax.ShapeDtypeStruct((N_SAMPLE, D_MODEL), BF16),
                   jax.ShapeDtypeStruct((DEC_BATCH, WINDOW, KV_DIM), F32),
                   jax.ShapeDtypeStruct((DEC_BATCH, WINDOW, KV_DIM), F32)],
        scratch_shapes=[pltpu.VMEM((n_sets, rows, 4 * WINDOW), F32),
                        pltpu.VMEM((n_sets, rows, 4 * WINDOW), BF16),
                        pltpu.VMEM((n_sets, rows, LANES), F32),
                        pltpu.VMEM((seq_rows, D_MODEL), F32)],
        compiler_params=_params("parallel"),
        name="swa",
    )(sinks, bias, sbias, q, k, k, v, v, q, k, v, cache_k, cache_v)


RET_IN_WIDTH = 2 * RET_QK + 2 * RET_V


def _ret_project(x, w_ref, tt_ref, rt_ref):
    xb = x.astype(BF16)
    hqk = jnp.dot(xb, w_ref[:, :2 * RET_QK].astype(BF16), preferred_element_type=F32)
    cos, sin = _tile_cos_sin(tt_ref, rt_ref)
    lane = lax.broadcasted_iota(jnp.int32, (1, LANES), 1)
    rot = _rotate_blocks(hqk, 2 * RET_QK // LANES, cos, sin, 1, (lane % 2) == 0)
    nq = RET_QK // LANES
    q = jnp.concatenate(rot[:nq], axis=1).astype(BF16)
    k = jnp.concatenate(rot[nq:], axis=1) * (RET_QK_DIM ** -0.5)
    v = jnp.dot(xb, w_ref[:, 2 * RET_QK:2 * RET_QK + RET_V].astype(BF16),
                preferred_element_type=F32).astype(BF16)
    gate = _silu(jnp.dot(xb, w_ref[:, 2 * RET_QK + RET_V:].astype(BF16), preferred_element_type=F32)).astype(BF16)
    return q, k, v, gate


RET_QVG = RET_QK + 2 * RET_V


def _ret_proj_kernel(x_ref, w_ref, tt_ref, rt_ref, qvg_ref, k_ref):
    q, k_ref[...], v, gate = _ret_project(x_ref[...], w_ref, tt_ref, rt_ref)
    qvg_ref[:, :RET_QK] = q
    qvg_ref[:, RET_QK:RET_QK + RET_V] = v
    qvg_ref[:, RET_QK + RET_V:] = gate


def _ret_proj(x, w, tables):
    n = x.shape[0]
    return pl.pallas_call(
        _ret_proj_kernel,
        grid=(n // ROW_TILE,),
        in_specs=[_rows(D_MODEL), _layer_weight((D_MODEL, RET_IN_WIDTH), 0), *_table_specs(RET_QK_DIM)],
        out_specs=[_rows(RET_QVG), _rows(RET_QK)],
        out_shape=[jax.ShapeDtypeStruct((n, RET_QVG), BF16), jax.ShapeDtypeStruct((n, RET_QK), F32)],
        compiler_params=_params("parallel"),
        name="ret_proj",
    )(x, w, *tables)


def _xpos_inv_lane():
    inv = 1.0 / (RET_THETA ** jnp.linspace(0.0, 1.0, RET_QK_DIM // 2, dtype=F32))
    return jnp.repeat(inv, 2)


def _ret_log_decay():
    return jnp.log(1.0 - 2.0 ** (-5.0 - jnp.arange(RET_HEADS, dtype=F32)))


def _ret_tables(L):
    lg = _ret_log_decay()
    idx = jnp.arange(L, dtype=F32)
    diff = idx[:, None] - idx[None, :]
    causal = diff >= 0
    dmat = jnp.where(causal[None], jnp.exp(lg[:, None, None] * jnp.where(causal, diff, 0.0)[None]), 0.0)
    xi = jnp.exp(lg[:, None] * (idx[None, :] + 1.0))
    zeta = jnp.exp(lg[:, None] * (L - 1.0 - idx[None, :]))
    decay = jnp.exp(lg * L)
    rep = lambda a: jnp.broadcast_to(a[:, :, None], (RET_HEADS, L, LANES))
    return dmat, rep(xi), rep(zeta), decay


def _lanes(t, width):
    return jnp.concatenate([t] * (width // LANES), axis=1)


def _group_norm_gate(o, gate):
    mu = jnp.mean(o, axis=-1, keepdims=True)
    d = o - mu
    var = jnp.mean(d * d, axis=-1, keepdims=True)
    return gate * (d * lax.rsqrt(var + LN_EPS))


def _retention_head(qh, kh, vh, s_old, dmat, xi, zeta, decay):
    att = lax.dot_general(qh, kh.astype(BF16), NT_DIMS, preferred_element_type=F32) * dmat
    o = jnp.dot(att.astype(BF16), vh, preferred_element_type=F32)
    o = o + jnp.dot(qh, s_old.astype(BF16), preferred_element_type=F32) * _lanes(xi, RET_V_DIM)
    kz = (kh * _lanes(zeta, RET_QK_DIM)).astype(BF16)
    upd = lax.dot_general(kz, vh, TN_DIMS, preferred_element_type=F32)
    return o, decay * s_old + upd


RET_SAMPLE_BATCH = DEC_BATCH // (SEQ // RET_CHUNK)


def _ret_kernel(qvg_ref, k_ref, dmat_ref, xi_ref, zeta_ref, decay_ref,
                x_ref, wo_ref, lg_ref, lb_ref,
                qvgs_ref, ks_ref, s_in_ref, sdmat_ref, sxi_ref, szeta_ref, sdecay_ref,
                xo_ref, sfin_ref, ys_ref, s_out_ref, s_ref, ys_scr):
    i = pl.program_id(0)

    @pl.when(i == 0)
    def _():
        s_ref[...] = jnp.zeros_like(s_ref)

    L = DEC_SEQ
    v0, g0 = RET_QK, RET_QK + RET_V
    sample = qvgs_ref[...].astype(F32)
    qa, va, ga = sample[:, :v0], sample[:, v0:g0], sample[:, g0:]
    mixed = None
    for h in range(RET_HEADS):
        qk = slice(h * RET_QK_DIM, (h + 1) * RET_QK_DIM)
        vg = slice(h * RET_V_DIM, (h + 1) * RET_V_DIM)
        vcols = slice(v0 + h * RET_V_DIM, v0 + (h + 1) * RET_V_DIM)
        gcols = slice(g0 + h * RET_V_DIM, g0 + (h + 1) * RET_V_DIM)
        o, s_new = _retention_head(qvg_ref[:, qk], k_ref[:, qk], qvg_ref[:, vcols], s_ref[h],
                                   dmat_ref[h], xi_ref[h], zeta_ref[h], decay_ref[h])
        s_ref[h] = s_new
        yh = _group_norm_gate(o, qvg_ref[:, gcols]).astype(BF16)
        d = jnp.dot(yh, wo_ref[vg, :].astype(BF16), preferred_element_type=F32)
        mixed = d if mixed is None else mixed + d
        for b in range(RET_SAMPLE_BATCH):
            rows = slice(b * L, (b + 1) * L)
            o, s_new = _retention_head(qa[rows, qk].astype(BF16), ks_ref[rows, qk], va[rows, vg].astype(BF16),
                                       s_in_ref[b, h], sdmat_ref[h], sxi_ref[h], szeta_ref[h], sdecay_ref[h])
            s_out_ref[b, h] = s_new
            ys_scr[rows, vg] = _group_norm_gate(o, ga[rows, vg])
    ys_ref[...] = ys_scr[...].astype(BF16)
    xo_ref[...] = _layer_norm(ALPHA * x_ref[...] + mixed, lg_ref[...], lb_ref[...])

    @pl.when(i == pl.num_programs(0) - 1)
    def _():
        sfin_ref[...] = s_ref[...]


def _ret(qvg, k, state, tables, sample_tables, x, w_o, ln):
    nc = SEQ // RET_CHUNK
    bb = RET_SAMPLE_BATCH
    rows = bb * DEC_SEQ
    cur = lambda i: (i, 0)
    tok = lambda i: (i + SEQ // rows, 0)
    st = lambda i: (i, 0, 0, 0)
    state_shape = (RET_HEADS, RET_QK_DIM, RET_V_DIM)
    state_block = (bb,) + state_shape
    table_specs = lambda n: [_resident((RET_HEADS, n, n)), _resident((RET_HEADS, n, LANES)),
                             _resident((RET_HEADS, n, LANES)), _SMEM]
    return pl.pallas_call(
        _ret_kernel,
        grid=(nc,),
        in_specs=[pl.BlockSpec((RET_CHUNK, RET_QVG), cur), pl.BlockSpec((RET_CHUNK, RET_QK), cur),
                  *table_specs(RET_CHUNK),
                  pl.BlockSpec((RET_CHUNK, D_MODEL), cur), _layer_weight((RET_V, D_MODEL), 0),
                  _resident((1, D_MODEL)), _resident((1, D_MODEL)),
                  pl.BlockSpec((rows, RET_QVG), tok), pl.BlockSpec((rows, RET_QK), tok),
                  pl.BlockSpec(state_block, st), *table_specs(DEC_SEQ)],
        out_specs=[pl.BlockSpec((RET_CHUNK, D_MODEL), cur), pl.BlockSpec(state_shape, lambda i: (0, 0, 0)),
                   pl.BlockSpec((rows, RET_V), cur), pl.BlockSpec(state_block, st)],
        out_shape=[jax.ShapeDtypeStruct((SEQ, D_MODEL), F32), jax.ShapeDtypeStruct(state_shape, F32),
                   jax.ShapeDtypeStruct((N_SAMPLE, RET_V), BF16),
                   jax.ShapeDtypeStruct((DEC_BATCH,) + state_shape, F32)],
        scratch_shapes=[pltpu.VMEM(state_shape, F32), pltpu.VMEM((rows, RET_V), F32)],
        compiler_params=_params("arbitrary"),
        name="ret",
    )(qvg, k, *tables, x, w_o, *ln, qvg, k, state, *sample_tables)


def kernel(x_prompt, x_sample, cache_k_win, cache_v_win, state_ret, ffn1_w_gate, ffn1_w_up, ffn1_w_down,
           ffn2_w_gate, ffn2_w_up, ffn2_w_down, ln_g, ln_b, attn_w_qkv, attn_w_o, attn_sinks, ret_w_in,
           ret_w_o):
    ln = lambda i, j: (ln_g[i, j][None, :], ln_b[i, j][None, :])
    ffn1 = (ffn1_w_gate, ffn1_w_up, ffn1_w_down)
    ffn2 = (ffn2_w_gate, ffn2_w_up, ffn2_w_down)

    x = _half_ffn((x_prompt.reshape(SEQ, D_MODEL), x_sample.reshape(N_SAMPLE, D_MODEL)), ffn1, 0, ln(0, 0))
    w_qkv = jnp.concatenate([_pair_heads(attn_w_qkv[0][:, :D_MODEL], 1), attn_w_qkv[0][:, D_MODEL:]],
                            axis=1).astype(BF16)
    w_o = _pair_heads(attn_w_o, 1).astype(BF16)
    q, k, v = _qkv_proj(x, w_qkv, _angle_tables(_rope_inv_lane(), QKV_TILE))
    sinks = attn_sinks[0]
    kv_major = lambda c: jnp.transpose(c[0], (0, 2, 3, 1)).reshape(DEC_BATCH, KV_DIM, WINDOW)
    o_p, o_s, nk_s, nv_s = _swa(q, k, v, kv_major(cache_k_win), kv_major(cache_v_win), sinks)
    x = _half_ffn((x,), ffn2, 0, ln(0, 2), mix=((o_p, o_s), w_o, ln(0, 1)))

    kv_shape = (1, 1, WINDOW, N_KV_HEADS, HEAD_DIM)
    new_k_p = k[SEQ - WINDOW:SEQ].reshape(kv_shape)
    new_v_p = v[SEQ - WINDOW:SEQ].reshape(kv_shape)
    new_k_s = nk_s.reshape(1, DEC_BATCH, WINDOW, N_KV_HEADS, HEAD_DIM)
    new_v_s = nv_s.reshape(1, DEC_BATCH, WINDOW, N_KV_HEADS, HEAD_DIM)

    x = _half_ffn((x,), ffn1, 1, ln(1, 0))
    qvg, rk = _ret_proj(x, ret_w_in, _angle_tables(_xpos_inv_lane()))
    x_p, s_p, y_s, s_s = _ret(qvg, rk, state_ret[0], _ret_tables(RET_CHUNK), _ret_tables(DEC_SEQ),
                              x, ret_w_o, ln(1, 1))
    x_s = _mix_out_sample(y_s, x, ret_w_o, ln(1, 1))
    y_p, y_s = _half_ffn((x_p, x_s), ffn2, 1, ln(1, 2), split_out=True)

    return (y_p.reshape(1, SEQ, D_MODEL), y_s.reshape(DEC_BATCH, DEC_SEQ, D_MODEL),
            new_k_p, new_v_p, new_k_s, new_v_s, s_p[None, None], s_s[None])
```

```python
import functools
import math

import jax
import jax.numpy as jnp
import numpy as np
from jax import lax
from jax.experimental import pallas as pl
from jax.experimental.pallas import tpu as pltpu

F32 = jnp.float32
BF16 = jnp.bfloat16

D_MODEL = 1024
SEQ = 16384
DEPTH = 2
DEC_BATCH = 128
DEC_SEQ = 8
PAST_LEN = 16384
N_HEADS = 16
N_KV_HEADS = 4
GROUP = N_HEADS // N_KV_HEADS
HEAD_DIM = D_MODEL // N_HEADS
WINDOW = 128
ROPE_THETA = 10000.0
ATTN_SCALE = 1.0 / math.sqrt(HEAD_DIM)
LOG2E = math.log2(math.e)
RET_HEADS = 4
RET_QK_DIM = D_MODEL // RET_HEADS
RET_V_DIM = 2 * D_MODEL // RET_HEADS
RET_CHUNK = 256
RET_THETA = 10000.0
FFN_DIM = 2816
LN_EPS = 1e-5
ALPHA = (2.0 * DEPTH) ** 0.25

N_SAMPLE = DEC_BATCH * DEC_SEQ
N_TOK = SEQ + N_SAMPLE
KV_DIM = N_KV_HEADS * HEAD_DIM
RET_QK = RET_HEADS * RET_QK_DIM
RET_V = RET_HEADS * RET_V_DIM

LANES = 128
ROW_TILE = 512
FFN_TILE = 512
FFN_CHUNK = 256
VMEM_LIMIT = 62 * 1024 * 1024

N_TILES = N_TOK // ROW_TILE
N_PROMPT_TILES = SEQ // ROW_TILE
FFN_TILES = N_TOK // FFN_TILE
FFN_PROMPT_TILES = SEQ // FFN_TILE
KV_PAIRS = N_KV_HEADS // 2

NT_DIMS = (((1,), (1,)), ((), ()))
TN_DIMS = (((0,), (0,)), ((), ()))


def _params(*sem):
    return pltpu.CompilerParams(dimension_semantics=sem, vmem_limit_bytes=VMEM_LIMIT)


def _resident(shape):
    return pl.BlockSpec(shape, lambda *_: (0,) * len(shape), pipeline_mode=pl.Buffered(1))


def _rows(width, offset_blocks=0, tile=ROW_TILE):
    return pl.BlockSpec((tile, width), lambda i: (i + offset_blocks, 0))


_SMEM = pl.BlockSpec(memory_space=pltpu.SMEM)


def _layer_norm(y, g, b):
    mu = jnp.mean(y, axis=-1, keepdims=True)
    d = y - mu
    var = jnp.mean(d * d, axis=-1, keepdims=True)
    return d * lax.rsqrt(var + LN_EPS) * g + b


def _silu(x):
    return x / (1.0 + jnp.exp(-x))


FFN_CHUNKS = FFN_DIM // FFN_CHUNK


def _ffn_weight_copies(layer, hbm_refs, vmem_refs, sem):
    wg_hbm, wu_hbm, wd_hbm = hbm_refs
    wg_ref, wu_ref, wd_ref = vmem_refs
    copies = []
    for c in range(FFN_CHUNKS):
        sl = pl.ds(c * FFN_CHUNK, FFN_CHUNK)
        copies.append((pltpu.make_async_copy(wg_hbm.at[layer, :, sl], wg_ref.at[:, sl], sem.at[0, c]),
                       pltpu.make_async_copy(wu_hbm.at[layer, :, sl], wu_ref.at[:, sl], sem.at[1, c]),
                       pltpu.make_async_copy(wd_hbm.at[layer, sl, :], wd_ref.at[sl, :], sem.at[2, c])))
    return copies


def _ffn_kernel(split_in, split_out, mix, layer, *refs):
    n_in = 2 if split_in else 1
    x_refs, refs = refs[:n_in], refs[n_in:]
    if mix:
        (yp_ref, ys_ref, wo_ref, mg_ref, mb_ref), refs = refs[:5], refs[5:]
    n_out = 2 if split_out else 1
    hbm_refs, (g_ref, b_ref), o_refs = refs[:3], refs[3:5], refs[5:5 + n_out]
    wg_ref, wu_ref, wd_ref, sem = refs[5 + n_out:]
    i = pl.program_id(0)
    copies = _ffn_weight_copies(layer, hbm_refs, (wg_ref, wu_ref, wd_ref), sem)

    def tile(first_step):
        x = x_refs[0][...]
        if split_in:
            x = jnp.where(i < FFN_PROMPT_TILES, x, x_refs[1][...])
        if mix:
            y = jnp.where(i < FFN_PROMPT_TILES, yp_ref[...], ys_ref[...].astype(BF16))
            m = jnp.dot(y, wo_ref[...].astype(BF16), preferred_element_type=F32)
            x = _layer_norm(ALPHA * x + m, mg_ref[...], mb_ref[...])
        xb = x.astype(BF16)
        acc = None
        for c in range(FFN_CHUNKS):
            if first_step:
                for cp in copies[c]:
                    cp.wait()
            sl = slice(c * FFN_CHUNK, (c + 1) * FFN_CHUNK)
            gate = jnp.dot(xb, wg_ref[:, sl].astype(BF16), preferred_element_type=F32)
            up = jnp.dot(xb, wu_ref[:, sl].astype(BF16), preferred_element_type=F32)
            h = (_silu(gate) * up).astype(BF16)
            d = jnp.dot(h, wd_ref[sl, :].astype(BF16), preferred_element_type=F32)
            acc = d if acc is None else acc + d
        y = _layer_norm(ALPHA * x + 0.5 * acc, g_ref[...], b_ref[...])
        if split_out:
            @pl.when(i < FFN_PROMPT_TILES)
            def _():
                o_refs[0][...] = y

            @pl.when(i >= FFN_PROMPT_TILES)
            def _():
                o_refs[1][...] = y
        else:
            o_refs[0][...] = y

    @pl.when(i == 0)
    def _():
        for chunk in copies:
            for cp in chunk:
                cp.start()
        tile(True)

    @pl.when(i > 0)
    def _():
        tile(False)


_PROMPT_TILE = lambda i: (jnp.minimum(i, FFN_PROMPT_TILES - 1), 0)
_SAMPLE_TILE = lambda i: (jnp.maximum(i - FFN_PROMPT_TILES, 0), 0)


def _layer_weight(shape, layer):
    return pl.BlockSpec((None,) + shape, lambda i: (layer, 0, 0), pipeline_mode=pl.Buffered(1))


def _half_ffn(xs, weights, layer, ln, *, split_out=False, mix=None):
    split_in = len(xs) == 2
    tile = (FFN_TILE, D_MODEL)
    x_specs = ([pl.BlockSpec(tile, _PROMPT_TILE), pl.BlockSpec(tile, _SAMPLE_TILE)]
               if split_in else [_rows(D_MODEL, tile=FFN_TILE)])
    mix_args = ()
    if mix is not None:
        ys, w_o, mix_ln = mix
        k = w_o.shape[1]
        x_specs += [pl.BlockSpec((FFN_TILE, k), _PROMPT_TILE), pl.BlockSpec((FFN_TILE, k), _SAMPLE_TILE),
                    _layer_weight((k, D_MODEL), 0), _resident((1, D_MODEL)), _resident((1, D_MODEL))]
        mix_args = (*ys, w_o, *mix_ln)
    if split_out:
        out_specs = [pl.BlockSpec(tile, _PROMPT_TILE), pl.BlockSpec(tile, _SAMPLE_TILE)]
        out_shape = [jax.ShapeDtypeStruct((SEQ, D_MODEL), F32), jax.ShapeDtypeStruct((N_SAMPLE, D_MODEL), F32)]
    else:
        out_specs = _rows(D_MODEL, tile=FFN_TILE)
        out_shape = jax.ShapeDtypeStruct((N_TOK, D_MODEL), F32)
    hbm = pl.BlockSpec(memory_space=pl.ANY)
    return pl.pallas_call(
        functools.partial(_ffn_kernel, split_in, split_out, mix is not None, layer),
        grid=(FFN_TILES,),
        in_specs=x_specs + [hbm, hbm, hbm, _resident((1, D_MODEL)), _resident((1, D_MODEL))],
        out_specs=out_specs,
        out_shape=out_shape,
        scratch_shapes=[pltpu.VMEM((D_MODEL, FFN_DIM), F32), pltpu.VMEM((D_MODEL, FFN_DIM), F32),
                        pltpu.VMEM((FFN_DIM, D_MODEL), F32), pltpu.SemaphoreType.DMA((3, FFN_CHUNKS))],
        compiler_params=_params("arbitrary"),
        name="half_ffn",
    )(*xs, *mix_args, *weights, *ln)


def _mix_out_kernel(y_ref, x_ref, w_ref, g_ref, b_ref, o_ref):
    m = jnp.dot(y_ref[...], w_ref[...].astype(BF16), preferred_element_type=F32)
    o_ref[...] = _layer_norm(ALPHA * x_ref[...] + m, g_ref[...], b_ref[...])


def _mix_out_sample(y, x, w, ln):
    k = y.shape[1]
    return pl.pallas_call(
        _mix_out_kernel,
        grid=(N_TILES - N_PROMPT_TILES,),
        in_specs=[_rows(k), _rows(D_MODEL, N_PROMPT_TILES), _layer_weight((k, D_MODEL), 0),
                  _resident((1, D_MODEL)), _resident((1, D_MODEL))],
        out_specs=_rows(D_MODEL),
        out_shape=jax.ShapeDtypeStruct((N_SAMPLE, D_MODEL), F32),
        compiler_params=_params("parallel"),
        name="mix_out",
    )(y, x, w, *ln)


def _angle_tables(inv_lane, tile=ROW_TILE):
    base = jnp.concatenate([jnp.arange(SEQ // tile) * tile,
                            jnp.full((N_SAMPLE // tile,), PAST_LEN)]).astype(F32)
    row = jnp.stack([jnp.arange(tile), jnp.arange(tile) % DEC_SEQ]).astype(F32)
    ta = base[:, None] * inv_lane[None, :]
    ra = row[:, :, None] * inv_lane[None, None, :]
    return jnp.stack([jnp.cos(ta), jnp.sin(ta)], axis=1), jnp.stack([jnp.cos(ra), jnp.sin(ra)], axis=1)


def _table_specs(width, tile=ROW_TILE, first_tile=0):
    return [pl.BlockSpec((1, 2, width), lambda i: (i + first_tile, 0, 0)),
            pl.BlockSpec((1, 2, tile, width), lambda i: ((i + first_tile) // (SEQ // tile), 0, 0, 0))]


def _tile_cos_sin(tt_ref, rt_ref):
    cb, sb = tt_ref[0, 0:1, :], tt_ref[0, 1:2, :]
    cr, sr = rt_ref[0, 0], rt_ref[0, 1]
    return cb * cr - sb * sr, sb * cr + cb * sr


def _rotate_blocks(h, n_blocks, cos, sin, shift, partner_above):
    width = cos.shape[1]
    sign = jnp.where(partner_above, -1.0, 1.0).astype(F32)
    sin = sin * jnp.concatenate([sign] * (width // LANES), axis=1)
    out = []
    for j in range(n_blocks):
        t = (j * LANES) % width
        blk = h[:, j * LANES:(j + 1) * LANES]
        partner = jnp.where(partner_above, pltpu.roll(blk, LANES - shift, 1), pltpu.roll(blk, shift, 1))
        out.append(blk * cos[:, t:t + LANES] + partner * sin[:, t:t + LANES])
    return out


def _qkv_kernel(x_ref, w_ref, tt_ref, rt_ref, q_ref, k_ref, v_ref):
    xb = x_ref[...].astype(BF16)
    h = jnp.dot(xb, w_ref[...].astype(BF16), preferred_element_type=F32)
    cos, sin = _tile_cos_sin(tt_ref, rt_ref)
    lane = lax.broadcasted_iota(jnp.int32, (1, LANES), 1)
    n_rot = (D_MODEL + KV_DIM) // LANES
    rot = _rotate_blocks(h, n_rot, cos, sin, HEAD_DIM // 2, (lane % HEAD_DIM) < HEAD_DIM // 2)
    nq = D_MODEL // LANES
    for j in range(nq):
        q_ref[:, j * LANES:(j + 1) * LANES] = (rot[j] * (ATTN_SCALE * LOG2E)).astype(BF16)
    for j in range(nq, n_rot):
        k_ref[:, (j - nq) * LANES:(j - nq + 1) * LANES] = rot[j]
    v_ref[...] = h[:, D_MODEL + KV_DIM:]


QKV_TILE = 1024


def _qkv_proj(x, w, tables):
    n = x.shape[0]
    width = D_MODEL + 2 * KV_DIM
    rows = functools.partial(_rows, tile=QKV_TILE)
    return pl.pallas_call(
        _qkv_kernel,
        grid=(n // QKV_TILE,),
        in_specs=[rows(D_MODEL), _resident((D_MODEL, width)), *_table_specs(LANES, tile=QKV_TILE)],
        out_specs=[rows(D_MODEL), rows(KV_DIM), rows(KV_DIM)],
        out_shape=[jax.ShapeDtypeStruct((n, D_MODEL), BF16), jax.ShapeDtypeStruct((n, KV_DIM), F32),
                   jax.ShapeDtypeStruct((n, KV_DIM), F32)],
        compiler_params=_params("parallel"),
        name="attn_qkv",
    )(x, w, *tables)


def _rope_inv_lane():
    inv = ROPE_THETA ** (-jnp.arange(0, HEAD_DIM, 2, dtype=F32) / HEAD_DIM)
    return jnp.tile(inv, LANES // (HEAD_DIM // 2))


def _pair_heads(w, axis):
    shape = w.shape
    w = w.reshape(shape[:axis] + (KV_PAIRS, 2, GROUP, HEAD_DIM) + shape[axis + 1:])
    return jnp.swapaxes(w, axis + 1, axis + 2).reshape(shape)


def _pair_operands(kall, vall):
    s = kall.shape[0]
    low = lax.broadcasted_iota(jnp.int32, (s, LANES), 1) < HEAD_DIM
    kb, vb = kall.astype(BF16), vall.astype(BF16)
    zero = jnp.zeros_like(kb)
    one_lo = jnp.where(low, 1.0, 0.0).astype(BF16)
    one_hi = jnp.where(low, 0.0, 1.0).astype(BF16)
    kcat = jnp.concatenate([jnp.where(low, kb, zero), jnp.where(low, zero, kb)], axis=0)
    vcat = jnp.concatenate([jnp.concatenate([jnp.where(low, vb, zero), one_lo], axis=1),
                            jnp.concatenate([jnp.where(low, zero, vb), one_hi], axis=1)], axis=0)
    return kcat, vcat


def _pair_softmax(s, sink_a, sink_b):
    half = s.shape[1] // 2
    sa, sb = s[:, :half], s[:, half:]
    sink_a, sink_b = sink_a * LOG2E, sink_b * LOG2E
    ma = jnp.maximum(jnp.max(sa, axis=-1, keepdims=True), sink_a)
    mb = jnp.maximum(jnp.max(sb, axis=-1, keepdims=True), sink_b)
    e = jnp.concatenate([jnp.exp2(sa - ma), jnp.exp2(sb - mb)], axis=1).astype(BF16)
    low = lax.broadcasted_iota(jnp.int32, (s.shape[0], LANES), 1) < HEAD_DIM
    return e, jnp.where(low, jnp.exp2(sink_a - ma), jnp.exp2(sink_b - mb))


SWA_SAMPLE_KEYS = 2 * WINDOW


SWA_STEP = 2


def _swa_kernel(sinks_ref, bias_ref, sbias_ref, q_ref, kp_ref, kc_ref, vp_ref, vc_ref,
                qs_ref, kn_ref, vn_ref, ck_ref, cv_ref,
                o_ref, os_ref, nk_ref, nv_ref, s_ref, e_ref, t_ref, os_scr):
    first = jnp.minimum(pl.program_id(0), 1)
    qall = qs_ref[...].astype(F32)
    for j in range(SWA_STEP):
        seq_j = slice(j * DEC_SEQ, (j + 1) * DEC_SEQ)
        _swa_sample_seq(sinks_ref, sbias_ref[...], qall[seq_j], kn_ref[seq_j, :], vn_ref[seq_j, :],
                        ck_ref[j].T, cv_ref[j].T, os_scr.at[seq_j], nk_ref.at[j], nv_ref.at[j])
        rows_j = slice(j * WINDOW, (j + 1) * WINDOW)
        bias = bias_ref[first] if j == 0 else bias_ref[1]
        for p in range(KV_PAIRS):
            lanes = slice(p * LANES, (p + 1) * LANES)
            k_prev = kp_ref[:, lanes] if j == 0 else kc_ref[(j - 1) * WINDOW:j * WINDOW, lanes]
            v_prev = vp_ref[:, lanes] if j == 0 else vc_ref[(j - 1) * WINDOW:j * WINDOW, lanes]
            kcat, vcat = _pair_operands(jnp.concatenate([k_prev, kc_ref[rows_j, lanes]], axis=0),
                                        jnp.concatenate([v_prev, vc_ref[rows_j, lanes]], axis=0))
            q = q_ref[rows_j, p * GROUP * LANES:(p + 1) * GROUP * LANES]
            qs = jnp.concatenate([q[:, g * LANES:(g + 1) * LANES] for g in range(GROUP)], axis=0)
            c = j * KV_PAIRS + p
            s_ref[c] = lax.dot_general(qs, kcat, NT_DIMS, preferred_element_type=F32)
            for g in range(GROUP):
                rows = slice(g * WINDOW, (g + 1) * WINDOW)
                head_a = (2 * p) * GROUP + g
                e, t = _pair_softmax(s_ref[c, rows, :] + bias, sinks_ref[head_a], sinks_ref[head_a + GROUP])
                e_ref[c, rows, :] = e
                t_ref[c, rows, :] = t
            r = jnp.dot(e_ref[c], vcat, preferred_element_type=F32)
            o = r[:, :LANES] / (r[:, LANES:] + t_ref[c])
            for g in range(GROUP):
                blk = p * GROUP + g
                o_ref[rows_j, blk * LANES:(blk + 1) * LANES] = o[g * WINDOW:(g + 1) * WINDOW].astype(BF16)
    os_ref[...] = os_scr[...].astype(BF16)


def _window_bias(n_rows, row_of, key_dist_and_pos):
    dist, kpos = key_dist_and_pos(row_of(np.arange(n_rows))[:, None])
    valid = (dist >= 0) & (dist < WINDOW) & (kpos >= 0)
    m = np.where(valid, 0.0, -np.inf).astype(np.float32)
    return np.concatenate([m, m], axis=1)


def _swa_sample_seq(sinks_ref, bias, qb, kn, vn, kc, vc, o_ref, nk_ref, nv_ref):
    L = DEC_SEQ
    pad = jnp.zeros((SWA_SAMPLE_KEYS - WINDOW - L, LANES), F32)
    nk_ref[0:WINDOW - L, :] = kc[L:, :]
    nk_ref[WINDOW - L:WINDOW, :] = kn
    nv_ref[0:WINDOW - L, :] = vc[L:, :]
    nv_ref[WINDOW - L:WINDOW, :] = vn
    for p in range(KV_PAIRS):
        lanes = slice(p * LANES, (p + 1) * LANES)
        kcat, vcat = _pair_operands(jnp.concatenate([kc[:, lanes], kn[:, lanes], pad], axis=0),
                                    jnp.concatenate([vc[:, lanes], vn[:, lanes], pad], axis=0))
        qs = jnp.concatenate(
            [qb[:, (p * GROUP + g) * LANES:(p * GROUP + g + 1) * LANES] for g in range(GROUP)],
            axis=0).astype(BF16)
        sink_col = lambda first: jnp.concatenate(
            [jnp.full((L, 1), sinks_ref[first + g], F32) for g in range(GROUP)], axis=0)
        s = lax.dot_general(qs, kcat, NT_DIMS, preferred_element_type=F32) + bias
        e, t = _pair_softmax(s, sink_col(2 * p * GROUP), sink_col((2 * p + 1) * GROUP))
        r = jnp.dot(e, vcat, preferred_element_type=F32)
        o = r[:, :LANES] / (r[:, LANES:] + t)
        for g in range(GROUP):
            blk = p * GROUP + g
            o_ref[:, blk * LANES:(blk + 1) * LANES] = o[g * L:(g + 1) * L]


def _swa(q, k, v, cache_k, cache_v, sinks):
    assert SEQ // WINDOW == DEC_BATCH
    rows = GROUP * WINDOW
    blk_rows, seq_rows = SWA_STEP * WINDOW, SWA_STEP * DEC_SEQ
    si = np.arange(2 * WINDOW)[None, :]
    bias = jnp.asarray(np.stack([
        _window_bias(WINDOW, lambda r: r, lambda qi: (qi + WINDOW - si, blk * WINDOW - WINDOW + si))
        for blk in (0, 1)]))
    cj = np.arange(SWA_SAMPLE_KEYS)[None, :]
    sbias = jnp.asarray(_window_bias(
        GROUP * DEC_SEQ, lambda r: r % DEC_SEQ,
        lambda qi: (np.where(cj < WINDOW + DEC_SEQ, PAST_LEN + qi - (PAST_LEN - WINDOW + cj), -1),
                    PAST_LEN - WINDOW + cj)))
    cur = lambda i: (i, 0)
    prev = lambda i: (jnp.maximum(SWA_STEP * i - 1, 0), 0)
    seq = lambda i: (i + SEQ // seq_rows, 0)
    cache = lambda i: (i, 0, 0)
    n_sets = SWA_STEP * KV_PAIRS
    return pl.pallas_call(
        _swa_kernel,
        grid=(DEC_BATCH // SWA_STEP,),
        in_specs=[_SMEM, _resident((2, WINDOW, 4 * WINDOW)),
                  _resident((GROUP * DEC_SEQ, 2 * SWA_SAMPLE_KEYS)),
                  pl.BlockSpec((blk_rows, D_MODEL), cur),
                  pl.BlockSpec((WINDOW, KV_DIM), prev), pl.BlockSpec((blk_rows, KV_DIM), cur),
                  pl.BlockSpec((WINDOW, KV_DIM), prev), pl.BlockSpec((blk_rows, KV_DIM), cur),
                  pl.BlockSpec((seq_rows, D_MODEL), seq),
                  pl.BlockSpec((seq_rows, KV_DIM), seq), pl.BlockSpec((seq_rows, KV_DIM), seq),
                  pl.BlockSpec((SWA_STEP, KV_DIM, WINDOW), cache), pl.BlockSpec((SWA_STEP, KV_DIM, WINDOW), cache)],
        out_specs=[pl.BlockSpec((blk_rows, D_MODEL), cur), pl.BlockSpec((seq_rows, D_MODEL), cur),
                   pl.BlockSpec((SWA_STEP, WINDOW, KV_DIM), cache), pl.BlockSpec((SWA_STEP, WINDOW, KV_DIM), cache)],
        out_shape=[jax.ShapeDtypeStruct((SEQ, D_MODEL), BF16), jax.ShapeDtypeStruct((N_SAMPLE, D_MODEL), BF16),
                   jax.ShapeDtypeStruct((DEC_BATCH, WINDOW, KV_DIM), F32),
                   jax.ShapeDtypeStruct((DEC_BATCH, WINDOW, KV_DIM), F32)],
        scratch_shapes=[pltpu.VMEM((n_sets, rows, 4 * WINDOW), F32),
                        pltpu.VMEM((n_sets, rows, 4 * WINDOW), BF16),
                        pltpu.VMEM((n_sets, rows, LANES), F32),
                        pltpu.VMEM((seq_rows, D_MODEL), F32)],
        compiler_params=_params("parallel"),
        name="swa",
    )(sinks, bias, sbias, q, k, k, v, v, q, k, v, cache_k, cache_v)


RET_IN_WIDTH = 2 * RET_QK + 2 * RET_V


def _ret_project(x, w_ref, tt_ref, rt_ref):
    xb = x.astype(BF16)
    hqk = jnp.dot(xb, w_ref[:, :2 * RET_QK].astype(BF16), preferred_element_type=F32)
    cos, sin = _tile_cos_sin(tt_ref, rt_ref)
    lane = lax.broadcasted_iota(jnp.int32, (1, LANES), 1)
    rot = _rotate_blocks(hqk, 2 * RET_QK // LANES, cos, sin, 1, (lane % 2) == 0)
    nq = RET_QK // LANES
    q = jnp.concatenate(rot[:nq], axis=1).astype(BF16)
    k = jnp.concatenate(rot[nq:], axis=1) * (RET_QK_DIM ** -0.5)
    v = jnp.dot(xb, w_ref[:, 2 * RET_QK:2 * RET_QK + RET_V].astype(BF16),
                preferred_element_type=F32).astype(BF16)
    gate = _silu(jnp.dot(xb, w_ref[:, 2 * RET_QK + RET_V:].astype(BF16), preferred_element_type=F32)).astype(BF16)
    return q, k, v, gate


def _ret_proj_kernel(x_ref, w_ref, tt_ref, rt_ref, q_ref, k_ref, v_ref, g_ref):
    q_ref[...], k_ref[...], v_ref[...], g_ref[...] = _ret_project(x_ref[...], w_ref, tt_ref, rt_ref)


def _ret_proj(x, w, tables):
    n = x.shape[0]
    return pl.pallas_call(
        _ret_proj_kernel,
        grid=(n // ROW_TILE,),
        in_specs=[_rows(D_MODEL), _layer_weight((D_MODEL, RET_IN_WIDTH), 0), *_table_specs(RET_QK_DIM)],
        out_specs=[_rows(RET_QK), _rows(RET_QK), _rows(RET_V), _rows(RET_V)],
        out_shape=[jax.ShapeDtypeStruct((n, RET_QK), BF16), jax.ShapeDtypeStruct((n, RET_QK), F32),
                   jax.ShapeDtypeStruct((n, RET_V), BF16), jax.ShapeDtypeStruct((n, RET_V), BF16)],
        compiler_params=_params("parallel"),
        name="ret_proj",
    )(x, w, *tables)


def _xpos_inv_lane():
    inv = 1.0 / (RET_THETA ** jnp.linspace(0.0, 1.0, RET_QK_DIM // 2, dtype=F32))
    return jnp.repeat(inv, 2)


def _ret_log_decay():
    return jnp.log(1.0 - 2.0 ** (-5.0 - jnp.arange(RET_HEADS, dtype=F32)))


def _ret_tables(L):
    lg = _ret_log_decay()
    idx = jnp.arange(L, dtype=F32)
    diff = idx[:, None] - idx[None, :]
    causal = diff >= 0
    dmat = jnp.where(causal[None], jnp.exp(lg[:, None, None] * jnp.where(causal, diff, 0.0)[None]), 0.0)
    xi = jnp.exp(lg[:, None] * (idx[None, :] + 1.0))
    zeta = jnp.exp(lg[:, None] * (L - 1.0 - idx[None, :]))
    decay = jnp.exp(lg * L)
    rep = lambda a: jnp.broadcast_to(a[:, :, None], (RET_HEADS, L, LANES))
    return dmat, rep(xi), rep(zeta), decay


def _lanes(t, width):
    return jnp.concatenate([t] * (width // LANES), axis=1)


def _group_norm_gate(o, gate):
    mu = jnp.mean(o, axis=-1, keepdims=True)
    d = o - mu
    var = jnp.mean(d * d, axis=-1, keepdims=True)
    return gate * (d * lax.rsqrt(var + LN_EPS))


def _retention_head(qh, kh, vh, s_old, dmat, xi, zeta, decay):
    att = lax.dot_general(qh, kh.astype(BF16), NT_DIMS, preferred_element_type=F32) * dmat
    o = jnp.dot(att.astype(BF16), vh, preferred_element_type=F32)
    o = o + jnp.dot(qh, s_old.astype(BF16), preferred_element_type=F32) * _lanes(xi, RET_V_DIM)
    kz = (kh * _lanes(zeta, RET_QK_DIM)).astype(BF16)
    upd = lax.dot_general(kz, vh, TN_DIMS, preferred_element_type=F32)
    return o, decay * s_old + upd


RET_SAMPLE_BATCH = DEC_BATCH // (SEQ // RET_CHUNK)


def _ret_kernel(q_ref, k_ref, v_ref, g_ref, dmat_ref, xi_ref, zeta_ref, decay_ref,
                x_ref, wo_ref, lg_ref, lb_ref,
                qs_ref, ks_ref, vs_ref, gs_ref, s_in_ref, sdmat_ref, sxi_ref, szeta_ref, sdecay_ref,
                xo_ref, sfin_ref, ys_ref, s_out_ref, s_ref, ys_scr):
    i = pl.program_id(0)

    @pl.when(i == 0)
    def _():
        s_ref[...] = jnp.zeros_like(s_ref)

    L = DEC_SEQ
    qa = qs_ref[...].astype(F32)
    va = vs_ref[...].astype(F32)
    ga = gs_ref[...].astype(F32)
    mixed = None
    for h in range(RET_HEADS):
        qk = slice(h * RET_QK_DIM, (h + 1) * RET_QK_DIM)
        vg = slice(h * RET_V_DIM, (h + 1) * RET_V_DIM)
        o, s_new = _retention_head(q_ref[:, qk], k_ref[:, qk], v_ref[:, vg], s_ref[h],
                                   dmat_ref[h], xi_ref[h], zeta_ref[h], decay_ref[h])
        s_ref[h] = s_new
        yh = _group_norm_gate(o, g_ref[:, vg]).astype(BF16)
        d = jnp.dot(yh, wo_ref[vg, :].astype(BF16), preferred_element_type=F32)
        mixed = d if mixed is None else mixed + d
        for b in range(RET_SAMPLE_BATCH):
            rows = slice(b * L, (b + 1) * L)
            o, s_new = _retention_head(qa[rows, qk].astype(BF16), ks_ref[rows, qk], va[rows, vg].astype(BF16),
                                       s_in_ref[b, h], sdmat_ref[h], sxi_ref[h], szeta_ref[h], sdecay_ref[h])
            s_out_ref[b, h] = s_new
            ys_scr[rows, vg] = _group_norm_gate(o, ga[rows, vg])
    ys_ref[...] = ys_scr[...].astype(BF16)
    xo_ref[...] = _layer_norm(ALPHA * x_ref[...] + mixed, lg_ref[...], lb_ref[...])

    @pl.when(i == pl.num_programs(0) - 1)
    def _():
        sfin_ref[...] = s_ref[...]


def _ret(q, k, v, g, state, tables, sample_tables, x, w_o, ln):
    nc = SEQ // RET_CHUNK
    bb = RET_SAMPLE_BATCH
    rows = bb * DEC_SEQ
    cur = lambda i: (i, 0)
    tok = lambda i: (i + SEQ // rows, 0)
    st = lambda i: (i, 0, 0, 0)
    state_shape = (RET_HEADS, RET_QK_DIM, RET_V_DIM)
    state_block = (bb,) + state_shape
    table_specs = lambda n: [_resident((RET_HEADS, n, n)), _resident((RET_HEADS, n, LANES)),
                             _resident((RET_HEADS, n, LANES)), _SMEM]
    return pl.pallas_call(
        _ret_kernel,
        grid=(nc,),
        in_specs=[pl.BlockSpec((RET_CHUNK, RET_QK), cur), pl.BlockSpec((RET_CHUNK, RET_QK), cur),
                  pl.BlockSpec((RET_CHUNK, RET_V), cur), pl.BlockSpec((RET_CHUNK, RET_V), cur),
                  *table_specs(RET_CHUNK),
                  pl.BlockSpec((RET_CHUNK, D_MODEL), cur), _layer_weight((RET_V, D_MODEL), 0),
                  _resident((1, D_MODEL)), _resident((1, D_MODEL)),
                  pl.BlockSpec((rows, RET_QK), tok), pl.BlockSpec((rows, RET_QK), tok),
                  pl.BlockSpec((rows, RET_V), tok), pl.BlockSpec((rows, RET_V), tok),
                  pl.BlockSpec(state_block, st), *table_specs(DEC_SEQ)],
        out_specs=[pl.BlockSpec((RET_CHUNK, D_MODEL), cur), pl.BlockSpec(state_shape, lambda i: (0, 0, 0)),
                   pl.BlockSpec((rows, RET_V), cur), pl.BlockSpec(state_block, st)],
        out_shape=[jax.ShapeDtypeStruct((SEQ, D_MODEL), F32), jax.ShapeDtypeStruct(state_shape, F32),
                   jax.ShapeDtypeStruct((N_SAMPLE, RET_V), BF16),
                   jax.ShapeDtypeStruct((DEC_BATCH,) + state_shape, F32)],
        scratch_shapes=[pltpu.VMEM(state_shape, F32), pltpu.VMEM((rows, RET_V), F32)],
        compiler_params=_params("arbitrary"),
        name="ret",
    )(q, k, v, g, *tables, x, w_o, *ln, q, k, v, g, state, *sample_tables)


def kernel(x_prompt, x_sample, cache_k_win, cache_v_win, state_ret, ffn1_w_gate, ffn1_w_up, ffn1_w_down,
           ffn2_w_gate, ffn2_w_up, ffn2_w_down, ln_g, ln_b, attn_w_qkv, attn_w_o, attn_sinks, ret_w_in,
           ret_w_o):
    ln = lambda i, j: (ln_g[i, j][None, :], ln_b[i, j][None, :])
    ffn1 = (ffn1_w_gate, ffn1_w_up, ffn1_w_down)
    ffn2 = (ffn2_w_gate, ffn2_w_up, ffn2_w_down)

    x = _half_ffn((x_prompt.reshape(SEQ, D_MODEL), x_sample.reshape(N_SAMPLE, D_MODEL)), ffn1, 0, ln(0, 0))
    w_qkv = jnp.concatenate([_pair_heads(attn_w_qkv[0][:, :D_MODEL], 1), attn_w_qkv[0][:, D_MODEL:]],
                            axis=1).astype(BF16)
    w_o = _pair_heads(attn_w_o, 1).astype(BF16)
    q, k, v = _qkv_proj(x, w_qkv, _angle_tables(_rope_inv_lane(), QKV_TILE))
    sinks = attn_sinks[0]
    kv_major = lambda c: jnp.transpose(c[0], (0, 2, 3, 1)).reshape(DEC_BATCH, KV_DIM, WINDOW)
    o_p, o_s, nk_s, nv_s = _swa(q, k, v, kv_major(cache_k_win), kv_major(cache_v_win), sinks)
    x = _half_ffn((x,), ffn2, 0, ln(0, 2), mix=((o_p, o_s), w_o, ln(0, 1)))

    kv_shape = (1, 1, WINDOW, N_KV_HEADS, HEAD_DIM)
    new_k_p = k[SEQ - WINDOW:SEQ].reshape(kv_shape)
    new_v_p = v[SEQ - WINDOW:SEQ].reshape(kv_shape)
    new_k_s = nk_s.reshape(1, DEC_BATCH, WINDOW, N_KV_HEADS, HEAD_DIM)
    new_v_s = nv_s.reshape(1, DEC_BATCH, WINDOW, N_KV_HEADS, HEAD_DIM)

    x = _half_ffn((x,), ffn1, 1, ln(1, 0))
    rq, rk, rv, rg = _ret_proj(x, ret_w_in, _angle_tables(_xpos_inv_lane()))
    x_p, s_p, y_s, s_s = _ret(rq, rk, rv, rg, state_ret[0], _ret_tables(RET_CHUNK), _ret_tables(DEC_SEQ),
                              x, ret_w_o, ln(1, 1))
    x_s = _mix_out_sample(y_s, x, ret_w_o, ln(1, 1))
    y_p, y_s = _half_ffn((x_p, x_s), ffn2, 1, ln(1, 2), split_out=True)

    return (y_p.reshape(1, SEQ, D_MODEL), y_s.reshape(DEC_BATCH, DEC_SEQ, D_MODEL),
            new_k_p, new_v_p, new_k_s, new_v_s, s_p[None, None], s_s[None])
```
